```python
import math
import jax, jax.numpy as jnp
from jax import lax
import numpy as np

D_MODEL = 2048
BATCH = 4
SEQ = 2048
DEPTH = 2

HEAD_DIM = 128
MIX_WIDTH = D_MODEL
N_HEADS_DIFF = MIX_WIDTH // 2 // HEAD_DIM
DIFF_QK_DIM = HEAD_DIM // 2
N_HEADS_SWA = MIX_WIDTH // 2 // HEAD_DIM
N_KV_SWA = N_HEADS_SWA // 4
WINDOW = 128
BLOCK = 128
D_FF = 4 * D_MODEL
ROPE_THETA = 10000.0
EPS = 1e-6
N_MOD = 6

DIFF_W = N_HEADS_DIFF * HEAD_DIM
SWA_Q_W = N_HEADS_SWA * HEAD_DIM
SWA_KV_W = N_KV_SWA * HEAD_DIM
IN_WIDTH = 3 * DIFF_W + SWA_Q_W + 2 * SWA_KV_W

kernel_name = "hymba_style_diffattn_swa_encoder"


def rmsnorm(x, g):
    xf = x.astype(jnp.float32)
    y = xf * lax.rsqrt(jnp.mean(xf * xf, axis=-1, keepdims=True) + EPS)
    return y.astype(x.dtype) * g


def rope_tables(positions, dim):
    inv = ROPE_THETA ** (-jnp.arange(0, dim, 2, dtype=jnp.float32) / dim)
    ang = positions.astype(jnp.float32)[..., None] * inv
    return jnp.cos(ang), jnp.sin(ang)


def apply_rope(x, cos, sin):
    shp = cos.shape[:2] + (1,) * (x.ndim - 3) + cos.shape[-1:]
    cos = cos.reshape(shp).astype(x.dtype)
    sin = sin.reshape(shp).astype(x.dtype)
    x1, x2 = jnp.split(x, 2, axis=-1)
    return jnp.concatenate([x1 * cos - x2 * sin, x2 * cos + x1 * sin], axis=-1)


def diff_attention(q, k, v, lam, lam_init, subln_g, cos, sin):
    b, s, h = q.shape[:3]
    nb = s // BLOCK
    q = apply_rope(q, cos, sin) * (DIFF_QK_DIM ** -0.5)
    k = apply_rope(k, cos, sin)
    qb = jnp.moveaxis(q.reshape(b, nb, BLOCK, h, 2, DIFF_QK_DIM), 1, 0)

    def one_block(qblk):
        logits = jnp.einsum('bqhmd,bkhmd->bhmqk', qblk, k).astype(jnp.float32)
        p = jax.nn.softmax(logits, axis=-1)
        w = (p[:, :, 0] - lam * p[:, :, 1]).astype(v.dtype)
        return jnp.einsum('bhqk,bkhd->bqhd', w, v)

    o = lax.map(one_block, qb)
    o = jnp.moveaxis(o, 0, 1).reshape(b, s, h, HEAD_DIM)
    o = rmsnorm(o, subln_g) * (1.0 - lam_init)
    return o.reshape(b, s, h * HEAD_DIM)


def window_gqa_sink(q, k, v, sink, cos, sin):
    b, s, h, d = q.shape
    g = k.shape[2]
    r = h // g
    nb = s // BLOCK
    q = apply_rope(q, cos, sin) * (d ** -0.5)
    k = apply_rope(k, cos, sin)
    qb = q.reshape(b, nb, BLOCK, g, r, d)

    def band(t):
        tp = jnp.pad(t, ((0, 0), (BLOCK, BLOCK), (0, 0), (0, 0))).reshape(b, nb + 2, BLOCK, g, d)
        return jnp.concatenate([tp[:, :-2], tp[:, 1:-1], tp[:, 2:]], axis=2)

    kb, vb = band(k), band(v)
    qi = jnp.arange(BLOCK)[:, None]
    kj = jnp.arange(3 * BLOCK)[None, :]
    rel = kj - BLOCK - qi
    kpos = jnp.arange(nb)[:, None, None] * BLOCK + kj[None] - BLOCK
    mask = (jnp.abs(rel)[None] <= WINDOW) & (kpos >= 0) & (kpos < s)

    logits = jnp.einsum('bnqgrd,bnkgd->bngrqk', qb, kb).astype(jnp.float32)
    logits = jnp.where(mask[None, :, None, None], logits, -jnp.inf)
    sink_l = sink.astype(jnp.float32).reshape(1, 1, g, r, 1, 1)
    m = jnp.maximum(logits.max(axis=-1, keepdims=True), sink_l)
    e = jnp.exp(logits - m)
    p = e / (e.sum(axis=-1, keepdims=True) + jnp.exp(sink_l - m))
    o = jnp.einsum('bngrqk,bnkgd->bnqgrd', p.astype(v.dtype), vb)
    return o.reshape(b, s, h * d)


def setup_inputs(seed: int = 0) -> dict:
    key = jax.random.key(seed)
    ks = jax.random.split(key, 16)
    nrm = jax.random.normal
    x = nrm(ks[0], (BATCH, SEQ, D_MODEL), jnp.float32)
    c = nrm(ks[1], (BATCH, D_MODEL), jnp.float32)
    offsets = jax.random.randint(ks[2], (BATCH, 1), 0, SEQ, dtype=jnp.int32)
    positions = (offsets + jnp.arange(SEQ, dtype=jnp.int32)[None, :]).astype(jnp.int32)
    ada_w = nrm(ks[3], (DEPTH, D_MODEL, N_MOD * D_MODEL), jnp.float32) * (0.5 * D_MODEL ** -0.5)
    ada_b = nrm(ks[4], (DEPTH, N_MOD * D_MODEL), jnp.float32) * 0.1
    norm_mix = 1.0 + 0.05 * nrm(ks[5], (DEPTH, D_MODEL), jnp.float32)
    w_in = nrm(ks[6], (DEPTH, D_MODEL, IN_WIDTH), jnp.float32) * (D_MODEL ** -0.5)
    diff_lambda = nrm(ks[7], (DEPTH, 4, DIFF_QK_DIM), jnp.float32) * 0.1
    diff_subln = 1.0 + 0.05 * nrm(ks[8], (DEPTH, HEAD_DIM), jnp.float32)
    swa_sink = nrm(ks[9], (DEPTH, N_HEADS_SWA), jnp.float32)
    w_out = nrm(ks[10], (DEPTH, MIX_WIDTH, D_MODEL), jnp.float32) * (MIX_WIDTH ** -0.5)
    norm_mlp = 1.0 + 0.05 * nrm(ks[11], (DEPTH, D_MODEL), jnp.float32)
    w_up = nrm(ks[12], (DEPTH, D_MODEL, D_FF), jnp.float32) * (D_MODEL ** -0.5)
    w_down = nrm(ks[13], (DEPTH, D_FF, D_MODEL), jnp.float32) * (D_FF ** -0.5)
    final_norm = 1.0 + 0.05 * nrm(ks[14], (D_MODEL,), jnp.float32)
    return {"x": x, "c": c, "positions": positions, "ada_w": ada_w, "ada_b": ada_b,
            "norm_mix": norm_mix, "w_in": w_in, "diff_lambda": diff_lambda,
            "diff_subln": diff_subln, "swa_sink": swa_sink, "w_out": w_out,
            "norm_mlp": norm_mlp, "w_up": w_up, "w_down": w_down, "final_norm": final_norm}


def reference(x, c, positions, ada_w, ada_b, norm_mix, w_in, diff_lambda, diff_subln,
              swa_sink, w_out, norm_mlp, w_up, w_down, final_norm):
    b, s, _ = x.shape
    cos_a, sin_a = rope_tables(positions, DIFF_QK_DIM)
    cos_b, sin_b = rope_tables(positions, HEAD_DIM)
    c_act = jax.nn.silu(c)
    splits = np.cumsum([DIFF_W, DIFF_W, DIFF_W, SWA_Q_W, SWA_KV_W]).tolist()

    for layer in range(DEPTH):
        mod = c_act @ ada_w[layer] + ada_b[layer]
        sh1, sc1, g1, sh2, sc2, g2 = [t[:, None, :] for t in jnp.split(mod, N_MOD, axis=-1)]

        h = rmsnorm(x, norm_mix[layer]) * (1.0 + sc1) + sh1
        proj = h @ w_in[layer]
        qa, ka, va, qb, kb, vb = jnp.split(proj, splits, axis=-1)
        qa = qa.reshape(b, s, N_HEADS_DIFF, 2, DIFF_QK_DIM)
        ka = ka.reshape(b, s, N_HEADS_DIFF, 2, DIFF_QK_DIM)
        va = va.reshape(b, s, N_HEADS_DIFF, HEAD_DIM)
        lam_init = 0.8 - 0.6 * math.exp(-0.3 * layer)
        lp = diff_lambda[layer].astype(jnp.float32)
        lam = jnp.exp(jnp.sum(lp[0] * lp[1])) - jnp.exp(jnp.sum(lp[2] * lp[3])) + lam_init
        out_a = diff_attention(qa, ka, va, lam, lam_init, diff_subln[layer], cos_a, sin_a)

        qb = qb.reshape(b, s, N_HEADS_SWA, HEAD_DIM)
        kb = kb.reshape(b, s, N_KV_SWA, HEAD_DIM)
        vb = vb.reshape(b, s, N_KV_SWA, HEAD_DIM)
        out_b = window_gqa_sink(qb, kb, vb, swa_sink[layer], cos_b, sin_b)

        mixed = jnp.concatenate([out_a, out_b], axis=-1) @ w_out[layer]
        x = x + g1 * mixed

        h2 = rmsnorm(x, norm_mlp[layer]) * (1.0 + sc2) + sh2
        x = x + g2 * (jnp.square(jax.nn.relu(h2 @ w_up[layer])) @ w_down[layer])

    return rmsnorm(x, final_norm)
```

```python
import functools
import math

import jax
import jax.numpy as jnp
from jax import lax
from jax.experimental import pallas as pl
from jax.experimental.pallas import tpu as pltpu

D_MODEL = 2048
BATCH = 4
SEQ = 2048
DEPTH = 2
HEAD_DIM = 128
N_HEADS_DIFF = 8
DIFF_QK_DIM = 64
N_HEADS_SWA = 8
N_KV_SWA = 2
GQA_RATIO = N_HEADS_SWA // N_KV_SWA
WINDOW = 128
BLOCK = 128
D_FF = 4 * D_MODEL
ROPE_THETA = 10000.0
EPS = 1e-6
N_MOD = 6
DIFF_W = N_HEADS_DIFF * HEAD_DIM
SWA_Q_W = N_HEADS_SWA * HEAD_DIM
SWA_KV_W = N_KV_SWA * HEAD_DIM
IN_WIDTH = 3 * DIFF_W + SWA_Q_W + 2 * SWA_KV_W
TOKENS = BATCH * SEQ

LANES = 128
SUBLANES = 8
VMEM_LIMIT = 52 * 1024 * 1024

QA_BLK = 0
KA_BLK = DIFF_W // LANES
VA_BLK = 2 * DIFF_W // LANES
QB_BLK = 3 * DIFF_W // LANES
KB_BLK = QB_BLK + SWA_Q_W // LANES
VB_BLK = KB_BLK + SWA_KV_W // LANES
N_PROJ_BLK = IN_WIDTH // LANES

F32 = jnp.float32
BF16 = jnp.bfloat16


def _params(*sem):
    return pltpu.CompilerParams(dimension_semantics=sem, vmem_limit_bytes=VMEM_LIMIT)


MOD_TN = 1024


def _mod_kernel(c_ref, w_ref, b_ref, o_ref):
    c = c_ref[...]
    c_act = (c * jax.nn.sigmoid(c)).astype(BF16)
    o_ref[...] = jnp.dot(c_act, w_ref[...].astype(BF16), preferred_element_type=F32) + b_ref[...]


def _modulation(c, ada_w, ada_b):
    c_pad = jnp.pad(c, ((0, SUBLANES - BATCH), (0, 0)))
    per_chunk = D_MODEL // MOD_TN
    return pl.pallas_call(
        _mod_kernel,
        grid=(DEPTH, N_MOD * per_chunk),
        in_specs=[
            pl.BlockSpec((SUBLANES, D_MODEL), lambda l, j: (0, 0)),
            pl.BlockSpec((None, D_MODEL, MOD_TN), lambda l, j: (l, 0, j)),
            pl.BlockSpec((None, 1, MOD_TN), lambda l, j: (l, 0, j)),
        ],
        out_specs=pl.BlockSpec((None, SUBLANES, MOD_TN),
                               lambda l, j: (l * N_MOD + j // per_chunk, 0, j % per_chunk)),
        out_shape=jax.ShapeDtypeStruct((DEPTH * N_MOD, SUBLANES, D_MODEL), F32),
        compiler_params=_params("arbitrary", "arbitrary"),
        name="adaln_mod",
    )(c_pad, ada_w, ada_b.reshape(DEPTH, 1, N_MOD * D_MODEL))


ROPE_TM = 1024


def _rope_kernel(pos_ref, inv_a_ref, inv_b_ref, ca_ref, sa_ref, cb_ref, sb_ref):
    pos = pos_ref[...].astype(F32)
    lane = lax.broadcasted_iota(jnp.int32, (1, LANES), 1)
    ang_a = pos * inv_a_ref[...]
    ang_b = pos * inv_b_ref[...]
    sign_a = jnp.where((lane & (DIFF_QK_DIM // 2)) == 0, -1.0, 1.0).astype(F32)
    sign_b = jnp.where((lane & (HEAD_DIM // 2)) == 0, -1.0, 1.0).astype(F32)
    ca_ref[...] = jnp.cos(ang_a)
    sa_ref[...] = jnp.sin(ang_a) * sign_a
    cb_ref[...] = jnp.cos(ang_b)
    sb_ref[...] = jnp.sin(ang_b) * sign_b


def _rope_tables(positions):
    def inv_freq(dim):
        return ROPE_THETA ** (-jnp.arange(0, dim, 2, dtype=F32) / dim)
    inv_a = jnp.tile(inv_freq(DIFF_QK_DIM), LANES // (DIFF_QK_DIM // 2)).reshape(1, LANES)
    inv_b = jnp.tile(inv_freq(HEAD_DIM), LANES // (HEAD_DIM // 2)).reshape(1, LANES)
    tab = jax.ShapeDtypeStruct((TOKENS, LANES), F32)
    row = pl.BlockSpec((ROPE_TM, LANES), lambda m: (m, 0))
    const = pl.BlockSpec((1, LANES), lambda m: (0, 0))
    return pl.pallas_call(
        _rope_kernel,
        grid=(TOKENS // ROPE_TM,),
        in_specs=[pl.BlockSpec((ROPE_TM, 1), lambda m: (m, 0)), const, const],
        out_specs=[row, row, row, row],
        out_shape=[tab, tab, tab, tab],
        compiler_params=_params("arbitrary"),
        name="rope_tables",
    )(positions.reshape(TOKENS, 1), inv_a, inv_b)


NORM_TM = 512


def _norm_kernel(x_ref, g_ref, *rest, modulated):
    x = x_ref[...]
    y = x * lax.rsqrt(jnp.mean(x * x, axis=-1, keepdims=True) + EPS)
    y = y * g_ref[...]
    if modulated:
        sc_ref, sh_ref, o_ref = rest
        b = (pl.program_id(0) * NORM_TM) // SEQ
        y = y * (1.0 + sc_ref[pl.ds(b, 1), :]) + sh_ref[pl.ds(b, 1), :]
    else:
        (o_ref,) = rest
    o_ref[...] = y.astype(o_ref.dtype)


def _norm(x, gain, mod=None, sc_idx=None, sh_idx=None, out_dtype=BF16):
    in_specs = [pl.BlockSpec((NORM_TM, D_MODEL), lambda m: (m, 0)),
                pl.BlockSpec((1, D_MODEL), lambda m: (0, 0))]
    args = [x, gain.reshape(1, D_MODEL)]
    if mod is not None:
        in_specs += [pl.BlockSpec((None, SUBLANES, D_MODEL), lambda m: (sc_idx, 0, 0)),
                     pl.BlockSpec((None, SUBLANES, D_MODEL), lambda m: (sh_idx, 0, 0))]
        args += [mod, mod]
    return pl.pallas_call(
        functools.partial(_norm_kernel, modulated=mod is not None),
        grid=(TOKENS // NORM_TM,),
        in_specs=in_specs,
        out_specs=pl.BlockSpec((NORM_TM, D_MODEL), lambda m: (m, 0)),
        out_shape=jax.ShapeDtypeStruct((TOKENS, D_MODEL), out_dtype),
        compiler_params=_params("arbitrary"),
        name="rmsnorm_mod" if mod is not None else "rmsnorm_final",
    )(*args)


MM_TM = 1024
MM_TN = 512
MM_TK = 2048


def _swap_halves(x, half):
    if 2 * half == LANES:
        return pltpu.roll(x, half, 1)
    lane = lax.broadcasted_iota(jnp.int32, (1, LANES), 1)
    return jnp.where((lane & half) == 0,
                     pltpu.roll(x, LANES - half, 1), pltpu.roll(x, half, 1))


def _rope(x, cos, sin_signed, half):
    return x * cos + _swap_halves(x, half) * sin_signed


def _in_proj_epilogue(acc, n, refs, o_ref):
    ca_ref, sa_ref, cb_ref, sb_ref = refs
    groups = MM_TN // LANES

    def cols(j):
        return acc[:, j * LANES:(j + 1) * LANES]

    def emit(kind_of_group):
        for j in range(groups):
            kind = kind_of_group(j)
            x = cols(j)
            if kind == "qa":
                x = _rope(x, ca_ref[...], sa_ref[...], DIFF_QK_DIM // 2) * (DIFF_QK_DIM ** -0.5)
            elif kind == "ka":
                x = _rope(x, ca_ref[...], sa_ref[...], DIFF_QK_DIM // 2)
            elif kind == "qb":
                x = _rope(x, cb_ref[...], sb_ref[...], HEAD_DIM // 2) * (HEAD_DIM ** -0.5)
            elif kind == "kb":
                x = _rope(x, cb_ref[...], sb_ref[...], HEAD_DIM // 2)
            o_ref[j] = x.astype(o_ref.dtype)

    def kind_at(blk):
        if blk < KA_BLK:
            return "qa"
        if blk < VA_BLK:
            return "ka"
        if blk < QB_BLK:
            return "v"
        if blk < KB_BLK:
            return "qb"
        if blk < VB_BLK:
            return "kb"
        return "v"

    for tile in range(N_PROJ_BLK // groups):
        pl.when(n == tile)(functools.partial(emit, lambda j, t=tile: kind_at(t * groups + j)))


def _gated_residual_epilogue(acc, m, refs, o_ref):
    x_ref, g_ref = refs
    b = (m * MM_TM) // SEQ
    o_ref[...] = x_ref[...] + g_ref[pl.ds(b, 1), :] * acc


def _relu2_epilogue(acc, o_ref):
    r = jnp.maximum(acc, 0.0)
    o_ref[...] = (r * r).astype(o_ref.dtype)


def _matmul_kernel(*refs, n_a, n_extra, nk, epilogue):
    a_refs = refs[:n_a]
    w_ref = refs[n_a]
    extra = refs[n_a + 1:n_a + 1 + n_extra]
    o_ref = refs[n_a + 1 + n_extra]
    scratch = refs[n_a + 2 + n_extra:]
    wb_ref = scratch[0]
    n, k, m = pl.program_id(0), pl.program_id(1), pl.program_id(2)

    @pl.when(m == 0)
    def _():
        wb_ref[...] = w_ref[...].astype(BF16)

    part = None
    off = 0
    for a_ref in a_refs:
        ka = a_ref.shape[1]
        d = jnp.dot(a_ref[...], wb_ref[off:off + ka, :], preferred_element_type=F32)
        part = d if part is None else part + d
        off += ka

    def finish(acc):
        if epilogue == "in_proj":
            _in_proj_epilogue(acc, n, extra, o_ref)
        elif epilogue == "gated_residual":
            _gated_residual_epilogue(acc, m, extra, o_ref)
        else:
            _relu2_epilogue(acc, o_ref)

    if nk == 1:
        finish(part)
    else:
        acc_ref = scratch[1]
        rows = pl.ds(pl.multiple_of(m * MM_TM, MM_TM), MM_TM)

        @pl.when(k == 0)
        def _():
            acc_ref[rows, :] = part

        @pl.when(jnp.logical_and(k > 0, k < nk - 1))
        def _():
            acc_ref[rows, :] += part

        @pl.when(k == nk - 1)
        def _():
            finish(acc_ref[rows, :] + part)


def _matmul(a_list, w, layer, epilogue, extra_args, extra_specs, out_spec, out_shape, name):
    _, k_total, n_total = w.shape
    assert sum(a.shape[1] for a in a_list) == k_total
    nk = k_total // MM_TK if len(a_list) == 1 else 1
    tk = k_total // nk
    grid = (n_total // MM_TN, nk, TOKENS // MM_TM)
    a_specs = [pl.BlockSpec((MM_TM, a.shape[1] // nk), lambda n, k, m: (m, k)) for a in a_list]
    w_spec = pl.BlockSpec((None, tk, MM_TN), lambda n, k, m: (layer, k, n))
    scratch = [pltpu.VMEM((tk, MM_TN), BF16)]
    if nk > 1:
        scratch.append(pltpu.VMEM((TOKENS, MM_TN), F32))
    return pl.pallas_call(
        functools.partial(_matmul_kernel, n_a=len(a_list), n_extra=len(extra_args), nk=nk,
                          epilogue=epilogue),
        grid=grid,
        in_specs=a_specs + [w_spec] + extra_specs,
        out_specs=out_spec,
        out_shape=out_shape,
        scratch_shapes=scratch,
        compiler_params=_params("arbitrary", "arbitrary", "arbitrary"),
        name=name,
    )(*a_list, w, *extra_args)


def _in_proj(h, w_in, layer, tables):
    groups = MM_TN // LANES
    first_b_tile = QB_BLK // groups
    last_a_tile = VA_BLK // groups - 1

    def tab_a(n, k, m):
        return (jnp.where(n <= last_a_tile, m, 0), 0)

    def tab_b(n, k, m):
        return (jnp.where(n >= first_b_tile, m, 0), 0)

    specs = [pl.BlockSpec((MM_TM, LANES), tab_a), pl.BlockSpec((MM_TM, LANES), tab_a),
             pl.BlockSpec((MM_TM, LANES), tab_b), pl.BlockSpec((MM_TM, LANES), tab_b)]
    return _matmul(
        [h], w_in, layer, "in_proj", list(tables), specs,
        pl.BlockSpec((groups, MM_TM, LANES), lambda n, k, m: (n, m, 0)),
        jax.ShapeDtypeStruct((N_PROJ_BLK, TOKENS, LANES), BF16), "in_proj")


def _gated_residual_matmul(a_list, w, layer, x, mod, gate_idx, name):
    nk = w.shape[1] // MM_TK if len(a_list) == 1 else 1
    last = nk - 1

    def xo_map(n, k, m):
        return (jnp.where(k == last, m, 0), n)

    specs = [pl.BlockSpec((MM_TM, MM_TN), xo_map),
             pl.BlockSpec((None, SUBLANES, MM_TN), lambda n, k, m: (gate_idx, 0, n))]
    return _matmul(
        a_list, w, layer, "gated_residual", [x, mod], specs,
        pl.BlockSpec((MM_TM, MM_TN), xo_map),
        jax.ShapeDtypeStruct((TOKENS, D_MODEL), F32), name)


def _up_proj(h, w_up, layer):
    return _matmul(
        [h], w_up, layer, "relu2", [], [],
        pl.BlockSpec((MM_TM, MM_TN), lambda n, k, m: (m, n)),
        jax.ShapeDtypeStruct((TOKENS, D_FF), BF16), "up_proj")


DIFF_TQ = 256
_NT = (((1,), (1,)), ((), ()))


def _diff_attn_kernel(q_ref, k_ref, v_ref, lam_ref, g_ref, o_ref, *, lam_init):
    q = q_ref[...]
    k = k_ref[...]
    lane = lax.broadcasted_iota(jnp.int32, (1, LANES), 1)
    zero = jnp.zeros_like(q)
    lp = lam_ref[...]
    lam = (jnp.exp(jnp.sum(lp[0:1] * lp[1:2], axis=-1, keepdims=True))
           - jnp.exp(jnp.sum(lp[2:3] * lp[3:4], axis=-1, keepdims=True)) + lam_init)

    def softmax_map(first):
        keep = (lane < DIFF_QK_DIM) if first else (lane >= DIFF_QK_DIM)
        qm = jnp.where(keep, q, zero)
        s = lax.dot_general(qm, k, _NT, preferred_element_type=F32)
        e = jnp.exp(s - jnp.max(s, axis=-1, keepdims=True))
        return e, jnp.sum(e, axis=-1, keepdims=True)

    e0, l0 = softmax_map(True)
    e1, l1 = softmax_map(False)
    w = e0 * (1.0 / l0) - e1 * (lam / l1)
    o = jnp.dot(w.astype(BF16), v_ref[...], preferred_element_type=F32)
    o = o * lax.rsqrt(jnp.mean(o * o, axis=-1, keepdims=True) + EPS)
    o_ref[...] = (o * g_ref[...] * (1.0 - lam_init)).astype(o_ref.dtype)


def _diff_attention(proj, diff_lambda, subln, layer):
    nq = SEQ // DIFF_TQ
    lam_init = 0.8 - 0.6 * math.exp(-0.3 * layer)
    return pl.pallas_call(
        functools.partial(_diff_attn_kernel, lam_init=lam_init),
        grid=(BATCH, N_HEADS_DIFF, nq),
        in_specs=[
            pl.BlockSpec((None, DIFF_TQ, LANES), lambda b, h, i: (QA_BLK + h, b * nq + i, 0)),
            pl.BlockSpec((None, SEQ, LANES), lambda b, h, i: (KA_BLK + h, b, 0)),
            pl.BlockSpec((None, SEQ, LANES), lambda b, h, i: (VA_BLK + h, b, 0)),
            pl.BlockSpec((None, 4, DIFF_QK_DIM), lambda b, h, i: (layer, 0, 0)),
            pl.BlockSpec((None, 1, HEAD_DIM), lambda b, h, i: (layer, 0, 0)),
        ],
        out_specs=pl.BlockSpec((DIFF_TQ, LANES), lambda b, h, i: (b * nq + i, h)),
        out_shape=jax.ShapeDtypeStruct((TOKENS, DIFF_W), BF16),
        compiler_params=_params("arbitrary", "arbitrary", "arbitrary"),
        name="diff_attn",
    )(proj, proj, proj, diff_lambda, subln.reshape(DEPTH, 1, HEAD_DIM))


SWA_BAND = 3 * BLOCK


def _swa_attn_kernel(q_ref, k_ref, v_ref, sink_ref, o_ref):
    n = pl.program_id(2)
    start = pl.multiple_of(jnp.clip(n - 1, 0, SEQ // BLOCK - 3) * BLOCK, BLOCK)
    q = q_ref[...].reshape(GQA_RATIO * BLOCK, HEAD_DIM)
    kb = k_ref[pl.ds(start, SWA_BAND), :]
    vb = v_ref[pl.ds(start, SWA_BAND), :]
    s = lax.dot_general(q, kb, _NT, preferred_element_type=F32)
    s = s.reshape(GQA_RATIO, BLOCK, SWA_BAND)
    qpos = n * BLOCK + lax.broadcasted_iota(jnp.int32, (1, BLOCK, SWA_BAND), 1)
    kpos = start + lax.broadcasted_iota(jnp.int32, (1, BLOCK, SWA_BAND), 2)
    s = jnp.where(jnp.abs(kpos - qpos) <= WINDOW, s, -jnp.inf)
    sink = sink_ref[...].reshape(GQA_RATIO, BLOCK, 1)
    mx = jnp.maximum(jnp.max(s, axis=-1, keepdims=True), sink)
    e = jnp.exp(s - mx)
    p = e / (jnp.sum(e, axis=-1, keepdims=True) + jnp.exp(sink - mx))
    p = p.reshape(GQA_RATIO * BLOCK, SWA_BAND).astype(BF16)
    o = jnp.dot(p, vb, preferred_element_type=F32)
    for r in range(GQA_RATIO):
        o_ref[:, r * HEAD_DIM:(r + 1) * HEAD_DIM] = o[r * BLOCK:(r + 1) * BLOCK].astype(o_ref.dtype)


def _swa_attention(proj, sink):
    nb = SEQ // BLOCK
    sink_rows = jnp.repeat(sink, BLOCK).reshape(N_KV_SWA, GQA_RATIO * BLOCK, 1)
    return pl.pallas_call(
        _swa_attn_kernel,
        grid=(BATCH, N_KV_SWA, nb),
        in_specs=[
            pl.BlockSpec((GQA_RATIO, BLOCK, LANES),
                         lambda b, g, n: (QB_BLK // GQA_RATIO + g, b * nb + n, 0)),
            pl.BlockSpec((None, SEQ, LANES), lambda b, g, n: (KB_BLK + g, b, 0)),
            pl.BlockSpec((None, SEQ, LANES), lambda b, g, n: (VB_BLK + g, b, 0)),
            pl.BlockSpec((None, GQA_RATIO * BLOCK, 1), lambda b, g, n: (g, 0, 0)),
        ],
        out_specs=pl.BlockSpec((BLOCK, GQA_RATIO * HEAD_DIM), lambda b, g, n: (b * nb + n, g)),
        out_shape=jax.ShapeDtypeStruct((TOKENS, SWA_Q_W), BF16),
        compiler_params=_params("arbitrary", "arbitrary", "arbitrary"),
        name="swa_attn",
    )(proj, proj, proj, sink_rows)


def kernel(x, c, positions, ada_w, ada_b, norm_mix, w_in, diff_lambda, diff_subln, swa_sink,
           w_out, norm_mlp, w_up, w_down, final_norm):
    mod = _modulation(c, ada_w, ada_b)
    tables = _rope_tables(positions)
    xr = x.reshape(TOKENS, D_MODEL)
    for layer in range(DEPTH):
        base = layer * N_MOD
        h = _norm(xr, norm_mix[layer], mod, base + 1, base + 0)
        proj = _in_proj(h, w_in, layer, tables)
        out_a = _diff_attention(proj, diff_lambda, diff_subln, layer)
        out_b = _swa_attention(proj, swa_sink[layer])
        xr = _gated_residual_matmul([out_a, out_b], w_out, layer, xr, mod, base + 2, "out_proj")
        h2 = _norm(xr, norm_mlp[layer], mod, base + 4, base + 3)
        hidden = _up_proj(h2, w_up, layer)
        xr = _gated_residual_matmul([hidden], w_down, layer, xr, mod, base + 5, "down_proj")
    out = _norm(xr, final_norm, out_dtype=F32)
    return out.reshape(BATCH, SEQ, D_MODEL)
```

```python
import functools
import math

import jax
import jax.numpy as jnp
from jax import lax
from jax.experimental import pallas as pl
from jax.experimental.pallas import tpu as pltpu

D_MODEL = 2048
BATCH = 4
SEQ = 2048
DEPTH = 2
HEAD_DIM = 128
N_HEADS_DIFF = 8
DIFF_QK_DIM = 64
N_HEADS_SWA = 8
N_KV_SWA = 2
GQA_RATIO = N_HEADS_SWA // N_KV_SWA
WINDOW = 128
BLOCK = 128
D_FF = 4 * D_MODEL
ROPE_THETA = 10000.0
EPS = 1e-6
N_MOD = 6
DIFF_W = N_HEADS_DIFF * HEAD_DIM
SWA_Q_W = N_HEADS_SWA * HEAD_DIM
SWA_KV_W = N_KV_SWA * HEAD_DIM
IN_WIDTH = 3 * DIFF_W + SWA_Q_W + 2 * SWA_KV_W
TOKENS = BATCH * SEQ

LANES = 128
SUBLANES = 8
VMEM_LIMIT = 52 * 1024 * 1024

QA_BLK = 0
KA_BLK = DIFF_W // LANES
VA_BLK = 2 * DIFF_W // LANES
QB_BLK = 3 * DIFF_W // LANES
KB_BLK = QB_BLK + SWA_Q_W // LANES
VB_BLK = KB_BLK + SWA_KV_W // LANES
N_PROJ_BLK = IN_WIDTH // LANES

F32 = jnp.float32
BF16 = jnp.bfloat16
LOG2E = math.log2(math.e)


def _params(*sem):
    return pltpu.CompilerParams(dimension_semantics=sem, vmem_limit_bytes=VMEM_LIMIT)


MOD_TN = 1024


def _mod_kernel(c_ref, w_ref, b_ref, o_ref):
    c = c_ref[...]
    c_act = (c * jax.nn.sigmoid(c)).astype(BF16)
    o_ref[...] = jnp.dot(c_act, w_ref[...].astype(BF16), preferred_element_type=F32) + b_ref[...]


def _modulation(c, ada_w, ada_b):
    c_pad = jnp.pad(c, ((0, SUBLANES - BATCH), (0, 0)))
    per_chunk = D_MODEL // MOD_TN
    return pl.pallas_call(
        _mod_kernel,
        grid=(DEPTH, N_MOD * per_chunk),
        in_specs=[
            pl.BlockSpec((SUBLANES, D_MODEL), lambda l, j: (0, 0)),
            pl.BlockSpec((None, D_MODEL, MOD_TN), lambda l, j: (l, 0, j)),
            pl.BlockSpec((None, 1, MOD_TN), lambda l, j: (l, 0, j)),
        ],
        out_specs=pl.BlockSpec((None, SUBLANES, MOD_TN),
                               lambda l, j: (l * N_MOD + j // per_chunk, 0, j % per_chunk)),
        out_shape=jax.ShapeDtypeStruct((DEPTH * N_MOD, SUBLANES, D_MODEL), F32),
        compiler_params=_params("arbitrary", "arbitrary"),
        name="adaln_mod",
    )(c_pad, ada_w, ada_b.reshape(DEPTH, 1, N_MOD * D_MODEL))


ROPE_TM = 1024


def _rope_kernel(pos_ref, inv_a_ref, inv_b_ref, ca_ref, sa_ref, cb_ref, sb_ref):
    pos = pos_ref[...].astype(F32)
    lane = lax.broadcasted_iota(jnp.int32, (1, LANES), 1)
    ang_a = pos * inv_a_ref[...]
    ang_b = pos * inv_b_ref[...]
    sign_a = jnp.where((lane & (DIFF_QK_DIM // 2)) == 0, -1.0, 1.0).astype(F32)
    sign_b = jnp.where((lane & (HEAD_DIM // 2)) == 0, -1.0, 1.0).astype(F32)
    ca_ref[...] = jnp.cos(ang_a)
    sa_ref[...] = jnp.sin(ang_a) * sign_a
    cb_ref[...] = jnp.cos(ang_b)
    sb_ref[...] = jnp.sin(ang_b) * sign_b


def _rope_tables(positions):
    def inv_freq(dim):
        return ROPE_THETA ** (-jnp.arange(0, dim, 2, dtype=F32) / dim)
    inv_a = jnp.tile(inv_freq(DIFF_QK_DIM), LANES // (DIFF_QK_DIM // 2)).reshape(1, LANES)
    inv_b = jnp.tile(inv_freq(HEAD_DIM), LANES // (HEAD_DIM // 2)).reshape(1, LANES)
    tab = jax.ShapeDtypeStruct((TOKENS, LANES), F32)
    row = pl.BlockSpec((ROPE_TM, LANES), lambda m: (m, 0))
    const = pl.BlockSpec((1, LANES), lambda m: (0, 0))
    return pl.pallas_call(
        _rope_kernel,
        grid=(TOKENS // ROPE_TM,),
        in_specs=[pl.BlockSpec((ROPE_TM, 1), lambda m: (m, 0)), const, const],
        out_specs=[row, row, row, row],
        out_shape=[tab, tab, tab, tab],
        compiler_params=_params("arbitrary"),
        name="rope_tables",
    )(positions.reshape(TOKENS, 1), inv_a, inv_b)


NORM_TM = 512


def _norm_kernel(x_ref, g_ref, *rest, modulated):
    x = x_ref[...]
    y = x * lax.rsqrt(jnp.mean(x * x, axis=-1, keepdims=True) + EPS)
    y = y * g_ref[...]
    if modulated:
        sc_ref, sh_ref, o_ref = rest
        b = (pl.program_id(0) * NORM_TM) // SEQ
        y = y * (1.0 + sc_ref[pl.ds(b, 1), :]) + sh_ref[pl.ds(b, 1), :]
    else:
        (o_ref,) = rest
    o_ref[...] = y.astype(o_ref.dtype)


def _norm(x, gain, mod=None, sc_idx=None, sh_idx=None, out_dtype=BF16):
    in_specs = [pl.BlockSpec((NORM_TM, D_MODEL), lambda m: (m, 0)),
                pl.BlockSpec((1, D_MODEL), lambda m: (0, 0))]
    args = [x, gain.reshape(1, D_MODEL)]
    if mod is not None:
        in_specs += [pl.BlockSpec((None, SUBLANES, D_MODEL), lambda m: (sc_idx, 0, 0)),
                     pl.BlockSpec((None, SUBLANES, D_MODEL), lambda m: (sh_idx, 0, 0))]
        args += [mod, mod]
    return pl.pallas_call(
        functools.partial(_norm_kernel, modulated=mod is not None),
        grid=(TOKENS // NORM_TM,),
        in_specs=in_specs,
        out_specs=pl.BlockSpec((NORM_TM, D_MODEL), lambda m: (m, 0)),
        out_shape=jax.ShapeDtypeStruct((TOKENS, D_MODEL), out_dtype),
        compiler_params=_params("arbitrary"),
        name="rmsnorm_mod" if mod is not None else "rmsnorm_final",
    )(*args)


MM_TM = 1024
MM_TN = 512
MM_TK = 2048


def _swap_halves(x, half):
    if 2 * half == LANES:
        return pltpu.roll(x, half, 1)
    lane = lax.broadcasted_iota(jnp.int32, (1, LANES), 1)
    return jnp.where((lane & half) == 0,
                     pltpu.roll(x, LANES - half, 1), pltpu.roll(x, half, 1))


def _rope(x, cos, sin_signed, half):
    return x * cos + _swap_halves(x, half) * sin_signed


def _in_proj_epilogue(acc, n, refs, o_ref):
    ca_ref, sa_ref, cb_ref, sb_ref = refs
    groups = MM_TN // LANES

    def cols(j):
        return acc[:, j * LANES:(j + 1) * LANES]

    def emit(kind_of_group):
        for j in range(groups):
            kind = kind_of_group(j)
            x = cols(j)
            if kind == "qa":
                x = _rope(x, ca_ref[...], sa_ref[...], DIFF_QK_DIM // 2) * (DIFF_QK_DIM ** -0.5 * LOG2E)
            elif kind == "ka":
                x = _rope(x, ca_ref[...], sa_ref[...], DIFF_QK_DIM // 2)
            elif kind == "qb":
                x = _rope(x, cb_ref[...], sb_ref[...], HEAD_DIM // 2) * (HEAD_DIM ** -0.5 * LOG2E)
            elif kind == "kb":
                x = _rope(x, cb_ref[...], sb_ref[...], HEAD_DIM // 2)
            o_ref[j] = x.astype(o_ref.dtype)

    def kind_at(blk):
        if blk < KA_BLK:
            return "qa"
        if blk < VA_BLK:
            return "ka"
        if blk < QB_BLK:
            return "v"
        if blk < KB_BLK:
            return "qb"
        if blk < VB_BLK:
            return "kb"
        return "v"

    for tile in range(N_PROJ_BLK // groups):
        pl.when(n == tile)(functools.partial(emit, lambda j, t=tile: kind_at(t * groups + j)))


def _gated_residual_epilogue(acc, m, refs, o_ref):
    x_ref, g_ref = refs
    b = (m * MM_TM) // SEQ
    o_ref[...] = x_ref[...] + g_ref[pl.ds(b, 1), :] * acc


def _relu2_epilogue(acc, o_ref):
    r = jnp.maximum(acc, 0.0)
    o_ref[...] = (r * r).astype(o_ref.dtype)


def _matmul_kernel(*refs, n_a, n_extra, nk, epilogue):
    a_refs = refs[:n_a]
    w_ref = refs[n_a]
    extra = refs[n_a + 1:n_a + 1 + n_extra]
    o_ref = refs[n_a + 1 + n_extra]
    scratch = refs[n_a + 2 + n_extra:]
    wb_ref = scratch[0]
    n, k, m = pl.program_id(0), pl.program_id(1), pl.program_id(2)

    @pl.when(m == 0)
    def _():
        wb_ref[...] = w_ref[...].astype(BF16)

    part = None
    off = 0
    for a_ref in a_refs:
        ka = a_ref.shape[1]
        d = jnp.dot(a_ref[...], wb_ref[off:off + ka, :], preferred_element_type=F32)
        part = d if part is None else part + d
        off += ka

    def finish(acc):
        if epilogue == "in_proj":
            _in_proj_epilogue(acc, n, extra, o_ref)
        elif epilogue == "gated_residual":
            _gated_residual_epilogue(acc, m, extra, o_ref)
        else:
            _relu2_epilogue(acc, o_ref)

    if nk == 1:
        finish(part)
    else:
        acc_ref = scratch[1]
        rows = pl.ds(pl.multiple_of(m * MM_TM, MM_TM), MM_TM)

        @pl.when(k == 0)
        def _():
            acc_ref[rows, :] = part

        @pl.when(jnp.logical_and(k > 0, k < nk - 1))
        def _():
            acc_ref[rows, :] += part

        @pl.when(k == nk - 1)
        def _():
            finish(acc_ref[rows, :] + part)


def _matmul(a_list, w, layer, epilogue, extra_args, extra_specs, out_spec, out_shape, name):
    _, k_total, n_total = w.shape
    assert sum(a.shape[1] for a in a_list) == k_total
    nk = k_total // MM_TK if len(a_list) == 1 else 1
    tk = k_total // nk
    grid = (n_total // MM_TN, nk, TOKENS // MM_TM)
    a_specs = [pl.BlockSpec((MM_TM, a.shape[1] // nk), lambda n, k, m: (m, k)) for a in a_list]
    w_spec = pl.BlockSpec((None, tk, MM_TN), lambda n, k, m: (layer, k, n))
    scratch = [pltpu.VMEM((tk, MM_TN), BF16)]
    if nk > 1:
        scratch.append(pltpu.VMEM((TOKENS, MM_TN), F32))
    return pl.pallas_call(
        functools.partial(_matmul_kernel, n_a=len(a_list), n_extra=len(extra_args), nk=nk,
                          epilogue=epilogue),
        grid=grid,
        in_specs=a_specs + [w_spec] + extra_specs,
        out_specs=out_spec,
        out_shape=out_shape,
        scratch_shapes=scratch,
        compiler_params=_params("arbitrary", "arbitrary", "arbitrary"),
        name=name,
    )(*a_list, w, *extra_args)


def _in_proj(h, w_in, layer, tables):
    groups = MM_TN // LANES
    first_b_tile = QB_BLK // groups
    last_a_tile = VA_BLK // groups - 1

    def tab_a(n, k, m):
        return (jnp.where(n <= last_a_tile, m, 0), 0)

    def tab_b(n, k, m):
        return (jnp.where(n >= first_b_tile, m, 0), 0)

    specs = [pl.BlockSpec((MM_TM, LANES), tab_a), pl.BlockSpec((MM_TM, LANES), tab_a),
             pl.BlockSpec((MM_TM, LANES), tab_b), pl.BlockSpec((MM_TM, LANES), tab_b)]
    return _matmul(
        [h], w_in, layer, "in_proj", list(tables), specs,
        pl.BlockSpec((groups, MM_TM, LANES), lambda n, k, m: (n, m, 0)),
        jax.ShapeDtypeStruct((N_PROJ_BLK, TOKENS, LANES), BF16), "in_proj")


def _gated_residual_matmul(a_list, w, layer, x, mod, gate_idx, name):
    nk = w.shape[1] // MM_TK if len(a_list) == 1 else 1
    last = nk - 1

    def xo_map(n, k, m):
        return (jnp.where(k == last, m, 0), n)

    specs = [pl.BlockSpec((MM_TM, MM_TN), xo_map),
             pl.BlockSpec((None, SUBLANES, MM_TN), lambda n, k, m: (gate_idx, 0, n))]
    return _matmul(
        a_list, w, layer, "gated_residual", [x, mod], specs,
        pl.BlockSpec((MM_TM, MM_TN), xo_map),
        jax.ShapeDtypeStruct((TOKENS, D_MODEL), F32), name)


def _up_proj(h, w_up, layer):
    return _matmul(
        [h], w_up, layer, "relu2", [], [],
        pl.BlockSpec((MM_TM, MM_TN), lambda n, k, m: (m, n)),
        jax.ShapeDtypeStruct((TOKENS, D_FF), BF16), "up_proj")


DIFF_TQ = 2048
DIFF_CHUNK = 256
ATTN_LOOKAHEAD = 2
_NT = (((1,), (1,)), ((), ()))
_TN = (((0,), (0,)), ((), ()))


def _software_pipeline(n_items, produce, consume):
    pending = {}
    for t in range(n_items + ATTN_LOOKAHEAD):
        if t < n_items:
            pending[t] = produce(t)
        if t >= ATTN_LOOKAHEAD:
            consume(t - ATTN_LOOKAHEAD, pending.pop(t - ATTN_LOOKAHEAD))


def _diff_attn_kernel(q_ref, k_ref, v_ref, lam_ref, g_ref, o_ref, *, lam_init):
    k = k_ref[...]
    v = v_ref[...]
    lane = lax.broadcasted_iota(jnp.int32, (1, LANES), 1)
    lp = lam_ref[...]
    lam = (jnp.exp(jnp.sum(lp[0:1] * lp[1:2], axis=-1, keepdims=True))
           - jnp.exp(jnp.sum(lp[2:3] * lp[3:4], axis=-1, keepdims=True)) + lam_init)
    n_chunks = DIFF_TQ // DIFF_CHUNK
    results = {}

    def logits(t):
        chunk, second = divmod(t, 2)
        q = q_ref[chunk * DIFF_CHUNK:(chunk + 1) * DIFF_CHUNK, :]
        keep = (lane >= DIFF_QK_DIM) if second else (lane < DIFF_QK_DIM)
        qm = jnp.where(keep, q, jnp.zeros_like(q))
        return lax.dot_general(k, qm, _NT, preferred_element_type=F32)

    def softmax_pv(t, st):
        e = jnp.exp2(st - jnp.max(st, axis=0, keepdims=True))
        denom = jnp.sum(e, axis=0, keepdims=True)
        ot = lax.dot_general(v, e.astype(BF16), _TN, preferred_element_type=F32)
        results[t] = (ot, denom)

    _software_pipeline(2 * n_chunks, logits, softmax_pv)
    for chunk in range(n_chunks):
        (ot0, l0), (ot1, l1) = results[2 * chunk], results[2 * chunk + 1]
        o = (ot0 * (1.0 / l0) - ot1 * (lam / l1)).T
        o = o * lax.rsqrt(jnp.mean(o * o, axis=-1, keepdims=True) + EPS)
        o_ref[chunk * DIFF_CHUNK:(chunk + 1) * DIFF_CHUNK, :] = (
            o * g_ref[...] * (1.0 - lam_init)).astype(o_ref.dtype)


def _diff_attention(proj, diff_lambda, subln, layer):
    nq = SEQ // DIFF_TQ
    lam_init = 0.8 - 0.6 * math.exp(-0.3 * layer)
    return pl.pallas_call(
        functools.partial(_diff_attn_kernel, lam_init=lam_init),
        grid=(BATCH, N_HEADS_DIFF, nq),
        in_specs=[
            pl.BlockSpec((None, DIFF_TQ, LANES), lambda b, h, i: (QA_BLK + h, b * nq + i, 0)),
            pl.BlockSpec((None, SEQ, LANES), lambda b, h, i: (KA_BLK + h, b, 0)),
            pl.BlockSpec((None, SEQ, LANES), lambda b, h, i: (VA_BLK + h, b, 0)),
            pl.BlockSpec((None, 4, DIFF_QK_DIM), lambda b, h, i: (layer, 0, 0)),
            pl.BlockSpec((None, 1, HEAD_DIM), lambda b, h, i: (layer, 0, 0)),
        ],
        out_specs=pl.BlockSpec((DIFF_TQ, LANES), lambda b, h, i: (b * nq + i, h)),
        out_shape=jax.ShapeDtypeStruct((TOKENS, DIFF_W), BF16),
        compiler_params=_params("arbitrary", "arbitrary", "arbitrary"),
        name="diff_attn",
    )(proj, proj, proj, diff_lambda, subln.reshape(DEPTH, 1, HEAD_DIM))


SWA_BAND = 3 * BLOCK
SWA_QB = 8


def _swa_attn_kernel(q_ref, k_ref, v_ref, sink_ref, o_ref):
    step = pl.program_id(2)
    sink = sink_ref[...] * LOG2E

    def band_start(j):
        n = step * SWA_QB + j
        return n, pl.multiple_of(jnp.clip(n - 1, 0, SEQ // BLOCK - 3) * BLOCK, BLOCK)

    def logits(j):
        n, start = band_start(j)
        q = q_ref[:, j * BLOCK:(j + 1) * BLOCK, :].reshape(GQA_RATIO * BLOCK, HEAD_DIM)
        kb = k_ref[pl.ds(start, SWA_BAND), :]
        return lax.dot_general(kb, q, _NT, preferred_element_type=F32)

    def softmax_pv(j, st):
        n, start = band_start(j)
        kpos = start + lax.broadcasted_iota(jnp.int32, (SWA_BAND, BLOCK), 0)
        qpos = n * BLOCK + lax.broadcasted_iota(jnp.int32, (SWA_BAND, BLOCK), 1)
        bias = jnp.where(jnp.abs(kpos - qpos) <= WINDOW, 0.0, -jnp.inf).astype(F32)
        st = st + jnp.concatenate([bias] * GQA_RATIO, axis=1)
        mx = jnp.maximum(jnp.max(st, axis=0, keepdims=True), sink)
        e = jnp.exp2(st - mx)
        denom = jnp.sum(e, axis=0, keepdims=True) + jnp.exp2(sink - mx)
        vb = v_ref[pl.ds(start, SWA_BAND), :]
        ot = lax.dot_general(vb, e.astype(BF16), _TN, preferred_element_type=F32)
        ot = ot * (1.0 / denom)
        for r in range(GQA_RATIO):
            o_ref[j * BLOCK:(j + 1) * BLOCK, r * HEAD_DIM:(r + 1) * HEAD_DIM] = (
                ot[:, r * BLOCK:(r + 1) * BLOCK].T.astype(o_ref.dtype))

    _software_pipeline(SWA_QB, logits, softmax_pv)


def _swa_attention(proj, sink):
    steps = SEQ // BLOCK // SWA_QB
    rows = SWA_QB * BLOCK
    sink_rows = jnp.repeat(sink, BLOCK).reshape(N_KV_SWA, 1, GQA_RATIO * BLOCK)
    return pl.pallas_call(
        _swa_attn_kernel,
        grid=(BATCH, N_KV_SWA, steps),
        in_specs=[
            pl.BlockSpec((GQA_RATIO, rows, LANES),
                         lambda b, g, n: (QB_BLK // GQA_RATIO + g, b * steps + n, 0)),
            pl.BlockSpec((None, SEQ, LANES), lambda b, g, n: (KB_BLK + g, b, 0)),
            pl.BlockSpec((None, SEQ, LANES), lambda b, g, n: (VB_BLK + g, b, 0)),
            pl.BlockSpec((None, 1, GQA_RATIO * BLOCK), lambda b, g, n: (g, 0, 0)),
        ],
        out_specs=pl.BlockSpec((rows, GQA_RATIO * HEAD_DIM), lambda b, g, n: (b * steps + n, g)),
        out_shape=jax.ShapeDtypeStruct((TOKENS, SWA_Q_W), BF16),
        compiler_params=_params("arbitrary", "arbitrary", "arbitrary"),
        name="swa_attn",
    )(proj, proj, proj, sink_rows)


def kernel(x, c, positions, ada_w, ada_b, norm_mix, w_in, diff_lambda, diff_subln, swa_sink,
           w_out, norm_mlp, w_up, w_down, final_norm):
    mod = _modulation(c, ada_w, ada_b)
    tables = _rope_tables(positions)
    xr = x.reshape(TOKENS, D_MODEL)
    for layer in range(DEPTH):
        base = layer * N_MOD
        h = _norm(xr, norm_mix[layer], mod, base + 1, base + 0)
        proj = _in_proj(h, w_in, layer, tables)
        out_a = _diff_attention(proj, diff_lambda, diff_subln, layer)
        out_b = _swa_attention(proj, swa_sink[layer])
        xr = _gated_residual_matmul([out_a, out_b], w_out, layer, xr, mod, base + 2, "out_proj")
        h2 = _norm(xr, norm_mlp[layer], mod, base + 4, base + 3)
        hidden = _up_proj(h2, w_up, layer)
        xr = _gated_residual_matmul([hidden], w_down, layer, xr, mod, base + 5, "down_proj")
    out = _norm(xr, final_norm, out_dtype=F32)
    return out.reshape(BATCH, SEQ, D_MODEL)
```

```python
import functools
import math

import jax
import jax.numpy as jnp
from jax import lax
from jax.experimental import pallas as pl
from jax.experimental.pallas import tpu as pltpu

D_MODEL = 2048
BATCH = 4
SEQ = 2048
DEPTH = 2
HEAD_DIM = 128
N_HEADS_DIFF = 8
DIFF_QK_DIM = 64
N_HEADS_SWA = 8
N_KV_SWA = 2
GQA_RATIO = N_HEADS_SWA // N_KV_SWA
WINDOW = 128
BLOCK = 128
D_FF = 4 * D_MODEL
ROPE_THETA = 10000.0
EPS = 1e-6
N_MOD = 6
DIFF_W = N_HEADS_DIFF * HEAD_DIM
SWA_Q_W = N_HEADS_SWA * HEAD_DIM
SWA_KV_W = N_KV_SWA * HEAD_DIM
IN_WIDTH = 3 * DIFF_W + SWA_Q_W + 2 * SWA_KV_W
TOKENS = BATCH * SEQ

LANES = 128
SUBLANES = 8
VMEM_LIMIT = 52 * 1024 * 1024

F32 = jnp.float32
BF16 = jnp.bfloat16
LOG2E = math.log2(math.e)


def _params(*sem):
    return pltpu.CompilerParams(dimension_semantics=sem, vmem_limit_bytes=VMEM_LIMIT)


def _software_pipeline(n_items, produce, consume, lookahead):
    pending = {}
    for t in range(n_items + lookahead):
        if t < n_items:
            pending[t] = produce(t)
        if t >= lookahead:
            consume(t - lookahead, pending.pop(t - lookahead))


MOD_TN = 1024


def _mod_kernel(c_ref, w_ref, b_ref, o_ref):
    c = c_ref[...]
    c_act = (c * jax.nn.sigmoid(c)).astype(BF16)
    o_ref[...] = jnp.dot(c_act, w_ref[...].astype(BF16), preferred_element_type=F32) + b_ref[...]


def _modulation(c, ada_w, ada_b):
    c_pad = jnp.pad(c, ((0, SUBLANES - BATCH), (0, 0)))
    per_chunk = D_MODEL // MOD_TN
    return pl.pallas_call(
        _mod_kernel,
        grid=(DEPTH, N_MOD * per_chunk),
        in_specs=[
            pl.BlockSpec((SUBLANES, D_MODEL), lambda l, j: (0, 0)),
            pl.BlockSpec((None, D_MODEL, MOD_TN), lambda l, j: (l, 0, j)),
            pl.BlockSpec((None, 1, MOD_TN), lambda l, j: (l, 0, j)),
        ],
        out_specs=pl.BlockSpec((None, SUBLANES, MOD_TN),
                               lambda l, j: (l * N_MOD + j // per_chunk, 0, j % per_chunk)),
        out_shape=jax.ShapeDtypeStruct((DEPTH * N_MOD, SUBLANES, D_MODEL), F32),
        compiler_params=_params("arbitrary", "arbitrary"),
        name="adaln_mod",
    )(c_pad, ada_w, ada_b.reshape(DEPTH, 1, N_MOD * D_MODEL))


ROPE_TM = 1024


def _rope_kernel(pos_ref, inv_a_ref, inv_b_ref, ca_ref, sa_ref, cb_ref, sb_ref):
    pos = pos_ref[...].astype(F32)
    lane = lax.broadcasted_iota(jnp.int32, (1, LANES), 1)
    ang_a = pos * inv_a_ref[...]
    ang_b = pos * inv_b_ref[...]
    sign_a = jnp.where((lane & (DIFF_QK_DIM // 2)) == 0, -1.0, 1.0).astype(F32)
    sign_b = jnp.where((lane & (HEAD_DIM // 2)) == 0, -1.0, 1.0).astype(F32)
    ca_ref[...] = jnp.cos(ang_a)
    sa_ref[...] = jnp.sin(ang_a) * sign_a
    cb_ref[...] = jnp.cos(ang_b)
    sb_ref[...] = jnp.sin(ang_b) * sign_b


def _rope_tables(positions):
    def inv_freq(dim):
        return ROPE_THETA ** (-jnp.arange(0, dim, 2, dtype=F32) / dim)
    inv_a = jnp.tile(inv_freq(DIFF_QK_DIM), LANES // (DIFF_QK_DIM // 2)).reshape(1, LANES)
    inv_b = jnp.tile(inv_freq(HEAD_DIM), LANES // (HEAD_DIM // 2)).reshape(1, LANES)
    tab = jax.ShapeDtypeStruct((TOKENS, LANES), F32)
    row = pl.BlockSpec((ROPE_TM, LANES), lambda m: (m, 0))
    const = pl.BlockSpec((1, LANES), lambda m: (0, 0))
    return pl.pallas_call(
        _rope_kernel,
        grid=(TOKENS // ROPE_TM,),
        in_specs=[pl.BlockSpec((ROPE_TM, 1), lambda m: (m, 0)), const, const],
        out_specs=[row, row, row, row],
        out_shape=[tab, tab, tab, tab],
        compiler_params=_params("arbitrary"),
        name="rope_tables",
    )(positions.reshape(TOKENS, 1), inv_a, inv_b)


NORM_TM = 1024


def _norm_kernel(x_ref, g_ref, *rest, modulated):
    x = x_ref[...]
    y = x * lax.rsqrt(jnp.mean(x * x, axis=-1, keepdims=True) + EPS)
    y = y * g_ref[...]
    if modulated:
        sc_ref, sh_ref, o_ref = rest
        b = (pl.program_id(0) * NORM_TM) // SEQ
        y = y * (1.0 + sc_ref[pl.ds(b, 1), :]) + sh_ref[pl.ds(b, 1), :]
    else:
        (o_ref,) = rest
    o_ref[...] = y.astype(o_ref.dtype)


def _norm(x, gain, mod=None, sc_idx=None, sh_idx=None, out_dtype=BF16):
    in_specs = [pl.BlockSpec((NORM_TM, D_MODEL), lambda m: (m, 0)),
                pl.BlockSpec((1, D_MODEL), lambda m: (0, 0))]
    args = [x, gain.reshape(1, D_MODEL)]
    if mod is not None:
        in_specs += [pl.BlockSpec((None, SUBLANES, D_MODEL), lambda m: (sc_idx, 0, 0)),
                     pl.BlockSpec((None, SUBLANES, D_MODEL), lambda m: (sh_idx, 0, 0))]
        args += [mod, mod]
    return pl.pallas_call(
        functools.partial(_norm_kernel, modulated=mod is not None),
        grid=(TOKENS // NORM_TM,),
        in_specs=in_specs,
        out_specs=pl.BlockSpec((NORM_TM, D_MODEL), lambda m: (m, 0)),
        out_shape=jax.ShapeDtypeStruct((TOKENS, D_MODEL), out_dtype),
        compiler_params=_params("arbitrary"),
        name="rmsnorm_mod" if mod is not None else "rmsnorm_final",
    )(*args)


MM_TN = 512
MM_TK = 2048
MM_SUB = 512
MM_GROUPS = MM_TN // LANES


def _swap_halves(x, half):
    if 2 * half == LANES:
        return pltpu.roll(x, half, 1)
    lane = lax.broadcasted_iota(jnp.int32, (1, LANES), 1)
    return jnp.where((lane & half) == 0,
                     pltpu.roll(x, LANES - half, 1), pltpu.roll(x, half, 1))


def _head_epilogue(acc, rows, row0, n, refs, o_ref, *, half, kinds, q_tiles, q_scale):
    if refs:
        cos = refs[0][rows, :]
        sin_signed = refs[1][rows, :]
    for j, kind in enumerate(kinds):
        x = acc[:, j * LANES:(j + 1) * LANES]
        if kind == "rope":
            x = x * cos + _swap_halves(x, half) * sin_signed
            if q_tiles:
                x = x * jnp.where(n < q_tiles, q_scale, 1.0)
        o_ref[j, rows, :] = x.astype(o_ref.dtype)


def _gated_residual_epilogue(acc, rows, row0, n, refs, o_ref):
    x_ref, g_ref = refs
    o_ref[rows, :] = x_ref[rows, :] + g_ref[pl.ds(row0 // SEQ, 1), :] * acc


def _relu2_epilogue(acc, rows, row0, n, refs, o_ref):
    r = jnp.maximum(acc, 0.0)
    o_ref[rows, :] = (r * r).astype(o_ref.dtype)


def _matmul_kernel(*refs, n_a, n_extra, nk, tm, epilogue):
    a_refs = refs[:n_a]
    w_ref = refs[n_a]
    extra = refs[n_a + 1:n_a + 1 + n_extra]
    o_ref = refs[n_a + 1 + n_extra]
    scratch = refs[n_a + 2 + n_extra:]
    wb_ref = scratch[0]
    n, k, m = pl.program_id(0), pl.program_id(1), pl.program_id(2)

    if nk > 1:
        acc_ref = scratch[1]

        @pl.when(jnp.logical_and(n == 0, jnp.logical_and(k == 0, m == 0)))
        def _():
            acc_ref[...] = jnp.zeros_like(acc_ref)

    @pl.when(m == 0)
    def _():
        wb_ref[...] = w_ref[...].astype(BF16)

    def product(t):
        rows = slice(t * MM_SUB, (t + 1) * MM_SUB)
        part = None
        off = 0
        for a_ref in a_refs:
            ka = a_ref.shape[1]
            d = jnp.dot(a_ref[rows, :], wb_ref[off:off + ka, :], preferred_element_type=F32)
            part = d if part is None else part + d
            off += ka
        return part

    def finish(t, part):
        rows = slice(t * MM_SUB, (t + 1) * MM_SUB)
        row0 = m * tm + t * MM_SUB
        if nk > 1:
            arows = pl.ds(pl.multiple_of(row0, MM_SUB), MM_SUB)
            part = jnp.where(k == 0, 0.0, acc_ref[arows, :]) + part
            acc_ref[arows, :] = part
        epilogue(part, rows, row0, n, extra, o_ref)

    _software_pipeline(tm // MM_SUB, product, finish, lookahead=1)


def _matmul(a_list, w, layer, n_off, n_cols, tm, epilogue, extra_args, extra_specs, out_spec,
            out_shape, name):
    k_total = w.shape[1]
    assert sum(a.shape[1] for a in a_list) == k_total
    nk = k_total // MM_TK if len(a_list) == 1 else 1
    tk = k_total // nk
    grid = (n_cols // MM_TN, nk, TOKENS // tm)
    a_specs = [pl.BlockSpec((tm, a.shape[1] // nk), lambda n, k, m: (m, k)) for a in a_list]
    w_spec = pl.BlockSpec((None, tk, MM_TN), lambda n, k, m: (layer, k, n + n_off))
    scratch = [pltpu.VMEM((tk, MM_TN), BF16)]
    if nk > 1:
        scratch.append(pltpu.VMEM((TOKENS, MM_TN), F32))
    return pl.pallas_call(
        functools.partial(_matmul_kernel, n_a=len(a_list), n_extra=len(extra_args), nk=nk, tm=tm,
                          epilogue=epilogue),
        grid=grid,
        in_specs=a_specs + [w_spec] + extra_specs,
        out_specs=out_spec,
        out_shape=out_shape,
        scratch_shapes=scratch,
        compiler_params=_params("arbitrary", "arbitrary", "arbitrary"),
        name=name,
    )(*a_list, w, *extra_args)


IN_TM = 2048


def _head_proj(h, w_in, layer, col0, n_cols, tables, half, kinds, q_tiles, q_scale, name):
    specs = [pl.BlockSpec((IN_TM, LANES), lambda n, k, m: (m, 0)) for _ in tables]
    epilogue = functools.partial(_head_epilogue, half=half, kinds=kinds, q_tiles=q_tiles,
                                 q_scale=q_scale)
    return _matmul(
        [h], w_in, layer, col0 // MM_TN, n_cols, IN_TM, epilogue, list(tables), specs,
        pl.BlockSpec((MM_GROUPS, IN_TM, LANES), lambda n, k, m: (n, m, 0)),
        jax.ShapeDtypeStruct((n_cols // LANES, TOKENS, LANES), BF16), name)


def _in_proj(h, w_in, layer, tables):
    ca, sa, cb, sb = tables
    rope = ("rope",) * MM_GROUPS
    plain = ("plain",) * MM_GROUPS
    qk_a = _head_proj(h, w_in, layer, 0, 2 * DIFF_W, (ca, sa), DIFF_QK_DIM // 2, rope,
                      DIFF_W // MM_TN, DIFF_QK_DIM ** -0.5 * LOG2E, "in_proj_qk_diff")
    v_a = _head_proj(h, w_in, layer, 2 * DIFF_W, DIFF_W, (), 0, plain, 0, 1.0, "in_proj_v_diff")
    q_b = _head_proj(h, w_in, layer, 3 * DIFF_W, SWA_Q_W, (cb, sb), HEAD_DIM // 2, rope,
                     SWA_Q_W // MM_TN, HEAD_DIM ** -0.5 * LOG2E, "in_proj_q_swa")
    kv_kinds = ("rope",) * (SWA_KV_W // LANES) + ("plain",) * (SWA_KV_W // LANES)
    kv_b = _head_proj(h, w_in, layer, 3 * DIFF_W + SWA_Q_W, 2 * SWA_KV_W, (cb, sb), HEAD_DIM // 2,
                      kv_kinds, 0, 1.0, "in_proj_kv_swa")
    return qk_a, v_a, q_b, kv_b


def _gated_residual_matmul(a_list, w, layer, x, mod, gate_idx, tm, name):
    nk = w.shape[1] // MM_TK if len(a_list) == 1 else 1
    last = nk - 1

    def xo_map(n, k, m):
        return (jnp.where(k == last, m, 0), n)

    specs = [pl.BlockSpec((tm, MM_TN), xo_map),
             pl.BlockSpec((None, SUBLANES, MM_TN), lambda n, k, m: (gate_idx, 0, n))]
    return _matmul(
        a_list, w, layer, 0, D_MODEL, tm, _gated_residual_epilogue, [x, mod], specs,
        pl.BlockSpec((tm, MM_TN), xo_map),
        jax.ShapeDtypeStruct((TOKENS, D_MODEL), F32), name)


UP_TM = 2048
OUT_TM = 2048
DOWN_TM = 1024


def _up_proj(h, w_up, layer):
    return _matmul(
        [h], w_up, layer, 0, D_FF, UP_TM, _relu2_epilogue, [], [],
        pl.BlockSpec((UP_TM, MM_TN), lambda n, k, m: (m, n)),
        jax.ShapeDtypeStruct((TOKENS, D_FF), BF16), "up_proj")


DIFF_CHUNK = 256
ATTN_LOOKAHEAD = 2
_NT = (((1,), (1,)), ((), ()))
_TN = (((0,), (0,)), ((), ()))


def _diff_attn_kernel(q_ref, k_ref, v_ref, lam_ref, g_ref, o_ref, *, lam_init):
    k = k_ref[...]
    v = v_ref[...]
    lane = lax.broadcasted_iota(jnp.int32, (1, LANES), 1)
    lp = lam_ref[...]
    lam = (jnp.exp(jnp.sum(lp[0:1] * lp[1:2], axis=-1, keepdims=True))
           - jnp.exp(jnp.sum(lp[2:3] * lp[3:4], axis=-1, keepdims=True)) + lam_init)
    n_chunks = SEQ // DIFF_CHUNK
    results = {}

    def logits(t):
        chunk, second = divmod(t, 2)
        q = q_ref[chunk * DIFF_CHUNK:(chunk + 1) * DIFF_CHUNK, :]
        keep = (lane >= DIFF_QK_DIM) if second else (lane < DIFF_QK_DIM)
        qm = jnp.where(keep, q, jnp.zeros_like(q))
        return lax.dot_general(k, qm, _NT, preferred_element_type=F32)

    def softmax_pv(t, st):
        e = jnp.exp2(st - jnp.max(st, axis=0, keepdims=True))
        denom = jnp.sum(e, axis=0, keepdims=True)
        ot = lax.dot_general(v, e.astype(BF16), _TN, preferred_element_type=F32)
        results[t] = (ot, denom)

    _software_pipeline(2 * n_chunks, logits, softmax_pv, ATTN_LOOKAHEAD)
    for chunk in range(n_chunks):
        (ot0, l0), (ot1, l1) = results[2 * chunk], results[2 * chunk + 1]
        o = (ot0 * (1.0 / l0) - ot1 * (lam / l1)).T
        o = o * lax.rsqrt(jnp.mean(o * o, axis=-1, keepdims=True) + EPS)
        o_ref[chunk * DIFF_CHUNK:(chunk + 1) * DIFF_CHUNK, :] = (
            o * g_ref[...] * (1.0 - lam_init)).astype(o_ref.dtype)


def _diff_attention(qk_a, v_a, diff_lambda, subln, layer):
    lam_init = 0.8 - 0.6 * math.exp(-0.3 * layer)
    return pl.pallas_call(
        functools.partial(_diff_attn_kernel, lam_init=lam_init),
        grid=(BATCH, N_HEADS_DIFF),
        in_specs=[
            pl.BlockSpec((None, SEQ, LANES), lambda b, h: (h, b, 0)),
            pl.BlockSpec((None, SEQ, LANES), lambda b, h: (N_HEADS_DIFF + h, b, 0)),
            pl.BlockSpec((None, SEQ, LANES), lambda b, h: (h, b, 0)),
            pl.BlockSpec((None, 4, DIFF_QK_DIM), lambda b, h: (layer, 0, 0)),
            pl.BlockSpec((None, 1, HEAD_DIM), lambda b, h: (layer, 0, 0)),
        ],
        out_specs=pl.BlockSpec((SEQ, LANES), lambda b, h: (b, h)),
        out_shape=jax.ShapeDtypeStruct((TOKENS, DIFF_W), BF16),
        compiler_params=_params("arbitrary", "arbitrary"),
        name="diff_attn",
    )(qk_a, qk_a, v_a, diff_lambda, subln.reshape(DEPTH, 1, HEAD_DIM))


SWA_BAND = 3 * BLOCK
SWA_QB = 8


def _swa_attn_kernel(q_ref, k_ref, v_ref, sink_ref, o_ref):
    step = pl.program_id(2)
    sink = sink_ref[...] * LOG2E

    def band_start(j):
        n = step * SWA_QB + j
        return n, pl.multiple_of(jnp.clip(n - 1, 0, SEQ // BLOCK - 3) * BLOCK, BLOCK)

    def logits(j):
        n, start = band_start(j)
        q = q_ref[:, j * BLOCK:(j + 1) * BLOCK, :].reshape(GQA_RATIO * BLOCK, HEAD_DIM)
        kb = k_ref[pl.ds(start, SWA_BAND), :]
        return lax.dot_general(kb, q, _NT, preferred_element_type=F32)

    def softmax_pv(j, st):
        n, start = band_start(j)
        kpos = start + lax.broadcasted_iota(jnp.int32, (SWA_BAND, BLOCK), 0)
        qpos = n * BLOCK + lax.broadcasted_iota(jnp.int32, (SWA_BAND, BLOCK), 1)
        bias = jnp.where(jnp.abs(kpos - qpos) <= WINDOW, 0.0, -jnp.inf).astype(F32)
        st = st + jnp.concatenate([bias] * GQA_RATIO, axis=1)
        mx = jnp.maximum(jnp.max(st, axis=0, keepdims=True), sink)
        e = jnp.exp2(st - mx)
        denom = jnp.sum(e, axis=0, keepdims=True) + jnp.exp2(sink - mx)
        vb = v_ref[pl.ds(start, SWA_BAND), :]
        ot = lax.dot_general(vb, e.astype(BF16), _TN, preferred_element_type=F32)
        ot = ot * (1.0 / denom)
        for r in range(GQA_RATIO):
            o_ref[j * BLOCK:(j + 1) * BLOCK, r * HEAD_DIM:(r + 1) * HEAD_DIM] = (
                ot[:, r * BLOCK:(r + 1) * BLOCK].T.astype(o_ref.dtype))

    _software_pipeline(SWA_QB, logits, softmax_pv, ATTN_LOOKAHEAD)


def _swa_attention(q_b, kv_b, sink):
    steps = SEQ // BLOCK // SWA_QB
    rows = SWA_QB * BLOCK
    sink_rows = jnp.repeat(sink, BLOCK).reshape(N_KV_SWA, 1, GQA_RATIO * BLOCK)
    return pl.pallas_call(
        _swa_attn_kernel,
        grid=(BATCH, N_KV_SWA, steps),
        in_specs=[
            pl.BlockSpec((GQA_RATIO, rows, LANES), lambda b, g, n: (g, b * steps + n, 0)),
            pl.BlockSpec((None, SEQ, LANES), lambda b, g, n: (g, b, 0)),
            pl.BlockSpec((None, SEQ, LANES), lambda b, g, n: (N_KV_SWA + g, b, 0)),
            pl.BlockSpec((None, 1, GQA_RATIO * BLOCK), lambda b, g, n: (g, 0, 0)),
        ],
        out_specs=pl.BlockSpec((rows, GQA_RATIO * HEAD_DIM), lambda b, g, n: (b * steps + n, g)),
        out_shape=jax.ShapeDtypeStruct((TOKENS, SWA_Q_W), BF16),
        compiler_params=_params("arbitrary", "arbitrary", "arbitrary"),
        name="swa_attn",
    )(q_b, kv_b, kv_b, sink_rows)


def kernel(x, c, positions, ada_w, ada_b, norm_mix, w_in, diff_lambda, diff_subln, swa_sink,
           w_out, norm_mlp, w_up, w_down, final_norm):
    mod = _modulation(c, ada_w, ada_b)
    tables = _rope_tables(positions)
    xr = x.reshape(TOKENS, D_MODEL)
    for layer in range(DEPTH):
        base = layer * N_MOD
        h = _norm(xr, norm_mix[layer], mod, base + 1, base + 0)
        qk_a, v_a, q_b, kv_b = _in_proj(h, w_in, layer, tables)
        out_a = _diff_attention(qk_a, v_a, diff_lambda, diff_subln, layer)
        out_b = _swa_attention(q_b, kv_b, swa_sink[layer])
        xr = _gated_residual_matmul([out_a, out_b], w_out, layer, xr, mod, base + 2, OUT_TM,
                                    "out_proj")
        h2 = _norm(xr, norm_mlp[layer], mod, base + 4, base + 3)
        hidden = _up_proj(h2, w_up, layer)
        xr = _gated_residual_matmul([hidden], w_down, layer, xr, mod, base + 5, DOWN_TM,
                                    "down_proj")
    out = _norm(xr, final_norm, out_dtype=F32)
    return out.reshape(BATCH, SEQ, D_MODEL)
```

```python
import functools
import math

import jax
import jax.numpy as jnp
from jax import lax
from jax.experimental import pallas as pl
from jax.experimental.pallas import tpu as pltpu

D_MODEL = 2048
BATCH = 4
SEQ = 2048
DEPTH = 2
HEAD_DIM = 128
N_HEADS_DIFF = 8
DIFF_QK_DIM = 64
N_HEADS_SWA = 8
N_KV_SWA = 2
GQA_RATIO = N_HEADS_SWA // N_KV_SWA
WINDOW = 128
BLOCK = 128
D_FF = 4 * D_MODEL
ROPE_THETA = 10000.0
EPS = 1e-6
N_MOD = 6
DIFF_W = N_HEADS_DIFF * HEAD_DIM
SWA_Q_W = N_HEADS_SWA * HEAD_DIM
SWA_KV_W = N_KV_SWA * HEAD_DIM
IN_WIDTH = 3 * DIFF_W + SWA_Q_W + 2 * SWA_KV_W
TOKENS = BATCH * SEQ

LANES = 128
SUBLANES = 8
VMEM_LIMIT = 52 * 1024 * 1024

F32 = jnp.float32
BF16 = jnp.bfloat16
LOG2E = math.log2(math.e)


def _params(*sem):
    return pltpu.CompilerParams(dimension_semantics=sem, vmem_limit_bytes=VMEM_LIMIT)


def _software_pipeline(n_items, produce, consume, lookahead):
    pending = {}
    for t in range(n_items + lookahead):
        if t < n_items:
            pending[t] = produce(t)
        if t >= lookahead:
            consume(t - lookahead, pending.pop(t - lookahead))


MOD_TN = 1024


def _mod_kernel(c_ref, w_ref, b_ref, o_ref):
    c = c_ref[...]
    c_act = (c * jax.nn.sigmoid(c)).astype(BF16)
    o_ref[...] = jnp.dot(c_act, w_ref[...].astype(BF16), preferred_element_type=F32) + b_ref[...]


def _modulation(c, ada_w, ada_b):
    c_pad = jnp.pad(c, ((0, SUBLANES - BATCH), (0, 0)))
    per_chunk = D_MODEL // MOD_TN
    return pl.pallas_call(
        _mod_kernel,
        grid=(DEPTH, N_MOD * per_chunk),
        in_specs=[
            pl.BlockSpec((SUBLANES, D_MODEL), lambda l, j: (0, 0)),
            pl.BlockSpec((None, D_MODEL, MOD_TN), lambda l, j: (l, 0, j)),
            pl.BlockSpec((None, 1, MOD_TN), lambda l, j: (l, 0, j)),
        ],
        out_specs=pl.BlockSpec((None, SUBLANES, MOD_TN),
                               lambda l, j: (l * N_MOD + j // per_chunk, 0, j % per_chunk)),
        out_shape=jax.ShapeDtypeStruct((DEPTH * N_MOD, SUBLANES, D_MODEL), F32),
        compiler_params=_params("arbitrary", "arbitrary"),
        name="adaln_mod",
    )(c_pad, ada_w, ada_b.reshape(DEPTH, 1, N_MOD * D_MODEL))


ROPE_TM = 1024


def _rope_kernel(pos_ref, inv_a_ref, inv_b_ref, ca_ref, sa_ref, cb_ref, sb_ref):
    pos = pos_ref[...].astype(F32)
    lane = lax.broadcasted_iota(jnp.int32, (1, LANES), 1)
    ang_a = pos * inv_a_ref[...]
    ang_b = pos * inv_b_ref[...]
    sign_a = jnp.where((lane & (DIFF_QK_DIM // 2)) == 0, -1.0, 1.0).astype(F32)
    sign_b = jnp.where((lane & (HEAD_DIM // 2)) == 0, -1.0, 1.0).astype(F32)
    ca_ref[...] = jnp.cos(ang_a)
    sa_ref[...] = jnp.sin(ang_a) * sign_a
    cb_ref[...] = jnp.cos(ang_b)
    sb_ref[...] = jnp.sin(ang_b) * sign_b


def _rope_tables(positions):
    def inv_freq(dim):
        return ROPE_THETA ** (-jnp.arange(0, dim, 2, dtype=F32) / dim)
    inv_a = jnp.tile(inv_freq(DIFF_QK_DIM), LANES // (DIFF_QK_DIM // 2)).reshape(1, LANES)
    inv_b = jnp.tile(inv_freq(HEAD_DIM), LANES // (HEAD_DIM // 2)).reshape(1, LANES)
    tab = jax.ShapeDtypeStruct((TOKENS, LANES), F32)
    row = pl.BlockSpec((ROPE_TM, LANES), lambda m: (m, 0))
    const = pl.BlockSpec((1, LANES), lambda m: (0, 0))
    return pl.pallas_call(
        _rope_kernel,
        grid=(TOKENS // ROPE_TM,),
        in_specs=[pl.BlockSpec((ROPE_TM, 1), lambda m: (m, 0)), const, const],
        out_specs=[row, row, row, row],
        out_shape=[tab, tab, tab, tab],
        compiler_params=_params("arbitrary"),
        name="rope_tables",
    )(positions.reshape(TOKENS, 1), inv_a, inv_b)


NORM_TM = 1024


def _norm_kernel(x_ref, g_ref, *rest, modulated):
    x = x_ref[...]
    y = x * lax.rsqrt(jnp.mean(x * x, axis=-1, keepdims=True) + EPS)
    y = y * g_ref[...]
    if modulated:
        sc_ref, sh_ref, o_ref = rest
        b = (pl.program_id(0) * NORM_TM) // SEQ
        y = y * (1.0 + sc_ref[pl.ds(b, 1), :]) + sh_ref[pl.ds(b, 1), :]
    else:
        (o_ref,) = rest
    o_ref[...] = y.astype(o_ref.dtype)


def _norm(x, gain, mod=None, sc_idx=None, sh_idx=None, out_dtype=BF16):
    in_specs = [pl.BlockSpec((NORM_TM, D_MODEL), lambda m: (m, 0)),
                pl.BlockSpec((1, D_MODEL), lambda m: (0, 0))]
    args = [x, gain.reshape(1, D_MODEL)]
    if mod is not None:
        in_specs += [pl.BlockSpec((None, SUBLANES, D_MODEL), lambda m: (sc_idx, 0, 0)),
                     pl.BlockSpec((None, SUBLANES, D_MODEL), lambda m: (sh_idx, 0, 0))]
        args += [mod, mod]
    return pl.pallas_call(
        functools.partial(_norm_kernel, modulated=mod is not None),
        grid=(TOKENS // NORM_TM,),
        in_specs=in_specs,
        out_specs=pl.BlockSpec((NORM_TM, D_MODEL), lambda m: (m, 0)),
        out_shape=jax.ShapeDtypeStruct((TOKENS, D_MODEL), out_dtype),
        compiler_params=_params("arbitrary"),
        name="rmsnorm_mod" if mod is not None else "rmsnorm_final",
    )(*args)


MM_TK = 2048
MM_SUB = 512


def _swap_halves(x, half):
    if 2 * half == LANES:
        return pltpu.roll(x, half, 1)
    lane = lax.broadcasted_iota(jnp.int32, (1, LANES), 1)
    return jnp.where((lane & half) == 0,
                     pltpu.roll(x, LANES - half, 1), pltpu.roll(x, half, 1))


def _head_epilogue(acc, rows, row0, n, refs, o_ref, *, half, is_rope, q_scale):
    cos_ref, sin_ref = refs
    for j in range(o_ref.shape[0]):
        x = acc[:, j * LANES:(j + 1) * LANES]
        rotate = is_rope(n, j)
        if rotate is not False:
            cos = cos_ref[rows, :]
            sin_signed = sin_ref[rows, :]
            if rotate is not True:
                cos = jnp.where(rotate, cos, 1.0)
                sin_signed = jnp.where(rotate, sin_signed, 0.0)
            x = (x * cos + _swap_halves(x, half) * sin_signed) * q_scale(n)
        o_ref[j, rows, :] = x.astype(o_ref.dtype)


def _gated_residual_epilogue(acc, rows, row0, n, refs, o_ref):
    x_ref, g_ref = refs
    o_ref[rows, :] = x_ref[rows, :] + g_ref[pl.ds(row0 // SEQ, 1), :] * acc


def _relu2_epilogue(acc, rows, row0, n, refs, o_ref):
    r = jnp.maximum(acc, 0.0)
    o_ref[rows, :] = (r * r).astype(o_ref.dtype)


def _matmul_kernel(*refs, n_a, n_extra, nk, tm, epilogue):
    a_refs = refs[:n_a]
    w_ref = refs[n_a]
    extra = refs[n_a + 1:n_a + 1 + n_extra]
    o_ref = refs[n_a + 1 + n_extra]
    scratch = refs[n_a + 2 + n_extra:]
    wb_ref = scratch[0]
    n, k, m = pl.program_id(0), pl.program_id(1), pl.program_id(2)

    if nk > 1:
        acc_ref = scratch[1]

        @pl.when(jnp.logical_and(n == 0, jnp.logical_and(k == 0, m == 0)))
        def _():
            acc_ref[...] = jnp.zeros_like(acc_ref)

    @pl.when(m == 0)
    def _():
        wb_ref[...] = w_ref[...].astype(BF16)

    def product(t):
        rows = slice(t * MM_SUB, (t + 1) * MM_SUB)
        part = None
        off = 0
        for a_ref in a_refs:
            ka = a_ref.shape[1]
            d = jnp.dot(a_ref[rows, :], wb_ref[off:off + ka, :], preferred_element_type=F32)
            part = d if part is None else part + d
            off += ka
        return part

    def finish(t, part):
        rows = slice(t * MM_SUB, (t + 1) * MM_SUB)
        row0 = m * tm + t * MM_SUB
        if nk > 1:
            arows = pl.ds(pl.multiple_of(row0, MM_SUB), MM_SUB)
            part = jnp.where(k == 0, 0.0, acc_ref[arows, :]) + part
            acc_ref[arows, :] = part
        epilogue(part, rows, row0, n, extra, o_ref)

    _software_pipeline(tm // MM_SUB, product, finish, lookahead=1)


def _matmul(a_list, w, layer, col0, n_cols, tm, tn, epilogue, extra_args, extra_specs, out_spec,
            out_shape, name):
    k_total = w.shape[1]
    assert sum(a.shape[1] for a in a_list) == k_total
    assert col0 % tn == 0 and n_cols % tn == 0 and TOKENS % tm == 0 and tm % MM_SUB == 0
    n_off = col0 // tn
    nk = k_total // MM_TK if len(a_list) == 1 else 1
    tk = k_total // nk
    grid = (n_cols // tn, nk, TOKENS // tm)
    a_specs = [pl.BlockSpec((tm, a.shape[1] // nk), lambda n, k, m: (m, k)) for a in a_list]
    w_spec = pl.BlockSpec((None, tk, tn), lambda n, k, m: (layer, k, n + n_off))
    scratch = [pltpu.VMEM((tk, tn), BF16)]
    if nk > 1:
        scratch.append(pltpu.VMEM((TOKENS, tn), F32))
    return pl.pallas_call(
        functools.partial(_matmul_kernel, n_a=len(a_list), n_extra=len(extra_args), nk=nk, tm=tm,
                          epilogue=epilogue),
        grid=grid,
        in_specs=a_specs + [w_spec] + extra_specs,
        out_specs=out_spec,
        out_shape=out_shape,
        scratch_shapes=scratch,
        compiler_params=_params("arbitrary", "arbitrary", "arbitrary"),
        name=name,
    )(*a_list, w, *extra_args)


def _head_proj(h, w_in, layer, col0, n_cols, tm, tn, tables, half, is_rope, q_scale, name):
    specs = [pl.BlockSpec((tm, LANES), lambda n, k, m: (m, 0)) for _ in tables]
    epilogue = functools.partial(_head_epilogue, half=half, is_rope=is_rope, q_scale=q_scale)
    return _matmul(
        [h], w_in, layer, col0, n_cols, tm, tn, epilogue, list(tables), specs,
        pl.BlockSpec((tn // LANES, tm, LANES), lambda n, k, m: (n, m, 0)),
        jax.ShapeDtypeStruct((n_cols // LANES, TOKENS, LANES), BF16), name)


DIFF_QK_TN = 1024
REST_TN = 512
REST_QB_TILE0 = DIFF_W // REST_TN
REST_KV_TILE = (DIFF_W + SWA_Q_W) // REST_TN
REST_KB_GROUPS = SWA_KV_W // LANES
VA_BLK = 0
QB_BLK = DIFF_W // LANES
KB_BLK = QB_BLK + SWA_Q_W // LANES
VB_BLK = KB_BLK + SWA_KV_W // LANES


def _in_proj(h, w_in, layer, tables):
    ca, sa, cb, sb = tables
    qk_a = _head_proj(
        h, w_in, layer, 0, 2 * DIFF_W, 1024, DIFF_QK_TN, (ca, sa), DIFF_QK_DIM // 2,
        lambda n, j: True,
        lambda n: jnp.where(n == 0, DIFF_QK_DIM ** -0.5 * LOG2E, 1.0), "in_proj_qk_diff")

    def is_qb(n):
        return jnp.logical_and(n >= REST_QB_TILE0, n < REST_KV_TILE)

    def rest_is_rope(n, j):
        return n >= REST_QB_TILE0 if j < REST_KB_GROUPS else is_qb(n)

    rest = _head_proj(
        h, w_in, layer, 2 * DIFF_W, IN_WIDTH - 2 * DIFF_W, 2048, REST_TN, (cb, sb), HEAD_DIM // 2,
        rest_is_rope,
        lambda n: jnp.where(is_qb(n), HEAD_DIM ** -0.5 * LOG2E, 1.0), "in_proj_rest")
    return qk_a, rest


def _gated_residual_matmul(a_list, w, layer, x, mod, gate_idx, tm, tn, name):
    nk = w.shape[1] // MM_TK if len(a_list) == 1 else 1
    last = nk - 1

    def xo_map(n, k, m):
        return (jnp.where(k == last, m, 0), n)

    specs = [pl.BlockSpec((tm, tn), xo_map),
             pl.BlockSpec((None, SUBLANES, tn), lambda n, k, m: (gate_idx, 0, n))]
    return _matmul(
        a_list, w, layer, 0, D_MODEL, tm, tn, _gated_residual_epilogue, [x, mod], specs,
        pl.BlockSpec((tm, tn), xo_map),
        jax.ShapeDtypeStruct((TOKENS, D_MODEL), F32), name)


UP_TILE = (2048, 1024)
OUT_TILE = (1024, 1024)
DOWN_TILE = (1024, 512)


def _up_proj(h, w_up, layer):
    tm, tn = UP_TILE
    return _matmul(
        [h], w_up, layer, 0, D_FF, tm, tn, _relu2_epilogue, [], [],
        pl.BlockSpec((tm, tn), lambda n, k, m: (m, n)),
        jax.ShapeDtypeStruct((TOKENS, D_FF), BF16), "up_proj")


DIFF_CHUNK = 256
ATTN_LOOKAHEAD = 2
_NT = (((1,), (1,)), ((), ()))
_TN = (((0,), (0,)), ((), ()))


def _diff_attn_kernel(q_ref, k_ref, v_ref, lam_ref, g_ref, o_ref, *, lam_init):
    k = k_ref[...]
    v = v_ref[...]
    lane = lax.broadcasted_iota(jnp.int32, (1, LANES), 1)
    lp = lam_ref[...]
    lam = (jnp.exp(jnp.sum(lp[0:1] * lp[1:2], axis=-1, keepdims=True))
           - jnp.exp(jnp.sum(lp[2:3] * lp[3:4], axis=-1, keepdims=True)) + lam_init)
    n_chunks = SEQ // DIFF_CHUNK
    results = {}

    def logits(t):
        chunk, second = divmod(t, 2)
        q = q_ref[chunk * DIFF_CHUNK:(chunk + 1) * DIFF_CHUNK, :]
        keep = (lane >= DIFF_QK_DIM) if second else (lane < DIFF_QK_DIM)
        qm = jnp.where(keep, q, jnp.zeros_like(q))
        return lax.dot_general(k, qm, _NT, preferred_element_type=F32)

    def softmax_pv(t, st):
        e = jnp.exp2(st - jnp.max(st, axis=0, keepdims=True))
        denom = jnp.sum(e, axis=0, keepdims=True)
        ot = lax.dot_general(v, e.astype(BF16), _TN, preferred_element_type=F32)
        results[t] = (ot, denom)

    _software_pipeline(2 * n_chunks, logits, softmax_pv, ATTN_LOOKAHEAD)
    for chunk in range(n_chunks):
        (ot0, l0), (ot1, l1) = results[2 * chunk], results[2 * chunk + 1]
        o = (ot0 * (1.0 / l0) - ot1 * (lam / l1)).T
        o = o * lax.rsqrt(jnp.mean(o * o, axis=-1, keepdims=True) + EPS)
        o_ref[chunk * DIFF_CHUNK:(chunk + 1) * DIFF_CHUNK, :] = (
            o * g_ref[...] * (1.0 - lam_init)).astype(o_ref.dtype)


def _diff_attention(qk_a, rest, diff_lambda, subln, layer):
    lam_init = 0.8 - 0.6 * math.exp(-0.3 * layer)
    return pl.pallas_call(
        functools.partial(_diff_attn_kernel, lam_init=lam_init),
        grid=(BATCH, N_HEADS_DIFF),
        in_specs=[
            pl.BlockSpec((None, SEQ, LANES), lambda b, h: (h, b, 0)),
            pl.BlockSpec((None, SEQ, LANES), lambda b, h: (N_HEADS_DIFF + h, b, 0)),
            pl.BlockSpec((None, SEQ, LANES), lambda b, h: (VA_BLK + h, b, 0)),
            pl.BlockSpec((None, 4, DIFF_QK_DIM), lambda b, h: (layer, 0, 0)),
            pl.BlockSpec((None, 1, HEAD_DIM), lambda b, h: (layer, 0, 0)),
        ],
        out_specs=pl.BlockSpec((SEQ, LANES), lambda b, h: (b, h)),
        out_shape=jax.ShapeDtypeStruct((TOKENS, DIFF_W), BF16),
        compiler_params=_params("arbitrary", "arbitrary"),
        name="diff_attn",
    )(qk_a, qk_a, rest, diff_lambda, subln.reshape(DEPTH, 1, HEAD_DIM))


SWA_BAND = 3 * BLOCK
SWA_QB = 8


def _swa_attn_kernel(q_ref, k_ref, v_ref, sink_ref, o_ref):
    step = pl.program_id(2)
    sink = sink_ref[...] * LOG2E

    def band_start(j):
        n = step * SWA_QB + j
        return n, pl.multiple_of(jnp.clip(n - 1, 0, SEQ // BLOCK - 3) * BLOCK, BLOCK)

    def logits(j):
        n, start = band_start(j)
        q = q_ref[:, j * BLOCK:(j + 1) * BLOCK, :].reshape(GQA_RATIO * BLOCK, HEAD_DIM)
        kb = k_ref[pl.ds(start, SWA_BAND), :]
        return lax.dot_general(kb, q, _NT, preferred_element_type=F32)

    def softmax_pv(j, st):
        n, start = band_start(j)
        kpos = start + lax.broadcasted_iota(jnp.int32, (SWA_BAND, BLOCK), 0)
        qpos = n * BLOCK + lax.broadcasted_iota(jnp.int32, (SWA_BAND, BLOCK), 1)
        bias = jnp.where(jnp.abs(kpos - qpos) <= WINDOW, 0.0, -jnp.inf).astype(F32)
        st = st + jnp.concatenate([bias] * GQA_RATIO, axis=1)
        mx = jnp.maximum(jnp.max(st, axis=0, keepdims=True), sink)
        e = jnp.exp2(st - mx)
        denom = jnp.sum(e, axis=0, keepdims=True) + jnp.exp2(sink - mx)
        vb = v_ref[pl.ds(start, SWA_BAND), :]
        ot = lax.dot_general(vb, e.astype(BF16), _TN, preferred_element_type=F32)
        ot = ot * (1.0 / denom)
        for r in range(GQA_RATIO):
            o_ref[j * BLOCK:(j + 1) * BLOCK, r * HEAD_DIM:(r + 1) * HEAD_DIM] = (
                ot[:, r * BLOCK:(r + 1) * BLOCK].T.astype(o_ref.dtype))

    _software_pipeline(SWA_QB, logits, softmax_pv, ATTN_LOOKAHEAD)


def _swa_attention(rest, sink):
    steps = SEQ // BLOCK // SWA_QB
    rows = SWA_QB * BLOCK
    sink_rows = jnp.repeat(sink, BLOCK).reshape(N_KV_SWA, 1, GQA_RATIO * BLOCK)
    return pl.pallas_call(
        _swa_attn_kernel,
        grid=(BATCH, N_KV_SWA, steps),
        in_specs=[
            pl.BlockSpec((GQA_RATIO, rows, LANES),
                         lambda b, g, n: (QB_BLK // GQA_RATIO + g, b * steps + n, 0)),
            pl.BlockSpec((None, SEQ, LANES), lambda b, g, n: (KB_BLK + g, b, 0)),
            pl.BlockSpec((None, SEQ, LANES), lambda b, g, n: (VB_BLK + g, b, 0)),
            pl.BlockSpec((None, 1, GQA_RATIO * BLOCK), lambda b, g, n: (g, 0, 0)),
        ],
        out_specs=pl.BlockSpec((rows, GQA_RATIO * HEAD_DIM), lambda b, g, n: (b * steps + n, g)),
        out_shape=jax.ShapeDtypeStruct((TOKENS, SWA_Q_W), BF16),
        compiler_params=_params("arbitrary", "arbitrary", "arbitrary"),
        name="swa_attn",
    )(rest, rest, rest, sink_rows)


def kernel(x, c, positions, ada_w, ada_b, norm_mix, w_in, diff_lambda, diff_subln, swa_sink,
           w_out, norm_mlp, w_up, w_down, final_norm):
    mod = _modulation(c, ada_w, ada_b)
    tables = _rope_tables(positions)
    xr = x.reshape(TOKENS, D_MODEL)
    for layer in range(DEPTH):
        base = layer * N_MOD
        h = _norm(xr, norm_mix[layer], mod, base + 1, base + 0)
        qk_a, rest = _in_proj(h, w_in, layer, tables)
        out_a = _diff_attention(qk_a, rest, diff_lambda, diff_subln, layer)
        out_b = _swa_attention(rest, swa_sink[layer])
        xr = _gated_residual_matmul([out_a, out_b], w_out, layer, xr, mod, base + 2, *OUT_TILE,
                                    "out_proj")
        h2 = _norm(xr, norm_mlp[layer], mod, base + 4, base + 3)
        hidden = _up_proj(h2, w_up, layer)
        xr = _gated_residual_matmul([hidden], w_down, layer, xr, mod, base + 5, *DOWN_TILE,
                                    "down_proj")
    out = _norm(xr, final_norm, out_dtype=F32)
    return out.reshape(BATCH, SEQ, D_MODEL)
```

```python
import functools
import math

import jax
import jax.numpy as jnp
from jax import lax
from jax.experimental import pallas as pl
from jax.experimental.pallas import tpu as pltpu

D_MODEL = 2048
BATCH = 4
SEQ = 2048
DEPTH = 2
HEAD_DIM = 128
N_HEADS_DIFF = 8
DIFF_QK_DIM = 64
N_HEADS_SWA = 8
N_KV_SWA = 2
GQA_RATIO = N_HEADS_SWA // N_KV_SWA
WINDOW = 128
BLOCK = 128
D_FF = 4 * D_MODEL
ROPE_THETA = 10000.0
EPS = 1e-6
N_MOD = 6
DIFF_W = N_HEADS_DIFF * HEAD_DIM
SWA_Q_W = N_HEADS_SWA * HEAD_DIM
SWA_KV_W = N_KV_SWA * HEAD_DIM
IN_WIDTH = 3 * DIFF_W + SWA_Q_W + 2 * SWA_KV_W
TOKENS = BATCH * SEQ

LANES = 128
SUBLANES = 8
VMEM_LIMIT = 56 * 1024 * 1024

F32 = jnp.float32
BF16 = jnp.bfloat16
LOG2E = math.log2(math.e)


def _params(*sem):
    return pltpu.CompilerParams(dimension_semantics=sem, vmem_limit_bytes=VMEM_LIMIT)


def _software_pipeline(n_items, produce, consume, lookahead):
    pending = {}
    for t in range(n_items + lookahead):
        if t < n_items:
            pending[t] = produce(t)
        if t >= lookahead:
            consume(t - lookahead, pending.pop(t - lookahead))


MOD_TN = 1024


def _mod_kernel(c_ref, w_ref, b_ref, o_ref):
    c = c_ref[...]
    c_act = (c * jax.nn.sigmoid(c)).astype(BF16)
    o_ref[...] = jnp.dot(c_act, w_ref[...].astype(BF16), preferred_element_type=F32) + b_ref[...]


def _modulation(c, ada_w, ada_b):
    c_pad = jnp.pad(c, ((0, SUBLANES - BATCH), (0, 0)))
    per_chunk = D_MODEL // MOD_TN
    return pl.pallas_call(
        _mod_kernel,
        grid=(DEPTH, N_MOD * per_chunk),
        in_specs=[
            pl.BlockSpec((SUBLANES, D_MODEL), lambda l, j: (0, 0)),
            pl.BlockSpec((None, D_MODEL, MOD_TN), lambda l, j: (l, 0, j)),
            pl.BlockSpec((None, 1, MOD_TN), lambda l, j: (l, 0, j)),
        ],
        out_specs=pl.BlockSpec((None, SUBLANES, MOD_TN),
                               lambda l, j: (l * N_MOD + j // per_chunk, 0, j % per_chunk)),
        out_shape=jax.ShapeDtypeStruct((DEPTH * N_MOD, SUBLANES, D_MODEL), F32),
        compiler_params=_params("arbitrary", "arbitrary"),
        name="adaln_mod",
    )(c_pad, ada_w, ada_b.reshape(DEPTH, 1, N_MOD * D_MODEL))


ROPE_TM = 1024


def _rope_kernel(pos_ref, inv_a_ref, inv_b_ref, ca_ref, sa_ref, cb_ref, sb_ref):
    pos = pos_ref[...].astype(F32)
    lane = lax.broadcasted_iota(jnp.int32, (1, LANES), 1)
    ang_a = pos * inv_a_ref[...]
    ang_b = pos * inv_b_ref[...]
    sign_a = jnp.where((lane & (DIFF_QK_DIM // 2)) == 0, -1.0, 1.0).astype(F32)
    sign_b = jnp.where((lane & (HEAD_DIM // 2)) == 0, -1.0, 1.0).astype(F32)
    ca_ref[...] = jnp.cos(ang_a)
    sa_ref[...] = jnp.sin(ang_a) * sign_a
    cb_ref[...] = jnp.cos(ang_b)
    sb_ref[...] = jnp.sin(ang_b) * sign_b


def _rope_tables(positions):
    def inv_freq(dim):
        return ROPE_THETA ** (-jnp.arange(0, dim, 2, dtype=F32) / dim)
    inv_a = jnp.tile(inv_freq(DIFF_QK_DIM), LANES // (DIFF_QK_DIM // 2)).reshape(1, LANES)
    inv_b = jnp.tile(inv_freq(HEAD_DIM), LANES // (HEAD_DIM // 2)).reshape(1, LANES)
    tab = jax.ShapeDtypeStruct((TOKENS, LANES), F32)
    row = pl.BlockSpec((ROPE_TM, LANES), lambda m: (m, 0))
    const = pl.BlockSpec((1, LANES), lambda m: (0, 0))
    return pl.pallas_call(
        _rope_kernel,
        grid=(TOKENS // ROPE_TM,),
        in_specs=[pl.BlockSpec((ROPE_TM, 1), lambda m: (m, 0)), const, const],
        out_specs=[row, row, row, row],
        out_shape=[tab, tab, tab, tab],
        compiler_params=_params("arbitrary"),
        name="rope_tables",
    )(positions.reshape(TOKENS, 1), inv_a, inv_b)


NORM_TM = 1024


def _norm_kernel(x_ref, g_ref, *rest, modulated):
    x = x_ref[...]
    y = x * lax.rsqrt(jnp.mean(x * x, axis=-1, keepdims=True) + EPS)
    y = y * g_ref[...]
    if modulated:
        sc_ref, sh_ref, o_ref = rest
        b = (pl.program_id(0) * NORM_TM) // SEQ
        y = y * (1.0 + sc_ref[pl.ds(b, 1), :]) + sh_ref[pl.ds(b, 1), :]
    else:
        (o_ref,) = rest
    o_ref[...] = y.astype(o_ref.dtype)


def _norm(x, gain, mod=None, sc_idx=None, sh_idx=None, out_dtype=BF16):
    in_specs = [pl.BlockSpec((NORM_TM, D_MODEL), lambda m: (m, 0)),
                pl.BlockSpec((1, D_MODEL), lambda m: (0, 0))]
    args = [x, gain.reshape(1, D_MODEL)]
    if mod is not None:
        in_specs += [pl.BlockSpec((None, SUBLANES, D_MODEL), lambda m: (sc_idx, 0, 0)),
                     pl.BlockSpec((None, SUBLANES, D_MODEL), lambda m: (sh_idx, 0, 0))]
        args += [mod, mod]
    return pl.pallas_call(
        functools.partial(_norm_kernel, modulated=mod is not None),
        grid=(TOKENS // NORM_TM,),
        in_specs=in_specs,
        out_specs=pl.BlockSpec((NORM_TM, D_MODEL), lambda m: (m, 0)),
        out_shape=jax.ShapeDtypeStruct((TOKENS, D_MODEL), out_dtype),
        compiler_params=_params("arbitrary"),
        name="rmsnorm_mod" if mod is not None else "rmsnorm_final",
    )(*args)


MM_TK = 2048
MM_SUB = 512


def _swap_halves(x, half):
    if 2 * half == LANES:
        return pltpu.roll(x, half, 1)
    lane = lax.broadcasted_iota(jnp.int32, (1, LANES), 1)
    return jnp.where((lane & half) == 0,
                     pltpu.roll(x, LANES - half, 1), pltpu.roll(x, half, 1))


def _head_epilogue(acc, rows, row0, n, refs, o_ref, *, half, is_rope, q_scale):
    cos_ref, sin_ref = refs
    for j in range(o_ref.shape[0]):
        x = acc[:, j * LANES:(j + 1) * LANES]
        rotate = is_rope(n, j)
        if rotate is not False:
            cos = cos_ref[rows, :]
            sin_signed = sin_ref[rows, :]
            if rotate is not True:
                cos = jnp.where(rotate, cos, 1.0)
                sin_signed = jnp.where(rotate, sin_signed, 0.0)
            x = (x * cos + _swap_halves(x, half) * sin_signed) * q_scale(n)
        o_ref[j, rows, :] = x.astype(o_ref.dtype)


def _gated_residual_epilogue(acc, rows, row0, n, refs, o_ref):
    x_ref, g_ref = refs
    o_ref[rows, :] = x_ref[rows, :] + g_ref[pl.ds(row0 // SEQ, 1), :] * acc


def _relu2_epilogue(acc, rows, row0, n, refs, o_ref):
    r = jnp.maximum(acc, 0.0)
    o_ref[rows, :] = (r * r).astype(o_ref.dtype)


def _matmul_kernel(*refs, n_a, n_extra, nk, tm, m_inner, epilogue):
    a_refs = refs[:n_a]
    w_ref = refs[n_a]
    extra = refs[n_a + 1:n_a + 1 + n_extra]
    o_ref = refs[n_a + 1 + n_extra]
    scratch = refs[n_a + 2 + n_extra:]
    wb_ref = scratch[0]
    outer, n, k, m = (pl.program_id(i) for i in range(4))

    if nk > 1:
        acc_ref = scratch[1]

        first_nk = jnp.logical_and(n == 0, k == 0)
        @pl.when(jnp.logical_and(jnp.logical_and(outer == 0, m == 0), first_nk))
        def _():
            acc_ref[...] = jnp.zeros_like(acc_ref)

    @pl.when(m == 0)
    def _():
        wb_ref[...] = w_ref[...].astype(BF16)

    def product(t):
        rows = slice(t * MM_SUB, (t + 1) * MM_SUB)
        part = None
        off = 0
        for a_ref in a_refs:
            ka = a_ref.shape[1]
            d = jnp.dot(a_ref[rows, :], wb_ref[off:off + ka, :], preferred_element_type=F32)
            part = d if part is None else part + d
            off += ka
        return part

    def finish(t, part):
        rows = slice(t * MM_SUB, (t + 1) * MM_SUB)
        local_row0 = m * tm + t * MM_SUB
        if nk > 1:
            arows = pl.ds(pl.multiple_of(local_row0, MM_SUB), MM_SUB)
            part = jnp.where(k == 0, 0.0, acc_ref[arows, :]) + part
            acc_ref[arows, :] = part
        epilogue(part, rows, outer * (m_inner * tm) + local_row0, n, extra, o_ref)

    _software_pipeline(tm // MM_SUB, product, finish, lookahead=1)


def _block(shape, index_map, m_inner):
    return pl.BlockSpec(
        shape, lambda o, n, k, m: index_map(n, k, o * m_inner + m, o * m_inner))


def _matmul(a_list, a_maps, w, layer, col0, n_cols, tm, tn, m_outer, epilogue, extra_args,
            extra_blocks, out_block, out_shape, name):
    k_total = w.shape[1]
    assert col0 % tn == 0 and n_cols % tn == 0 and tm % MM_SUB == 0
    assert TOKENS % (tm * m_outer) == 0
    n_off = col0 // tn
    nk = k_total // MM_TK if len(a_list) == 1 else 1
    tk = k_total // nk
    m_inner = TOKENS // tm // m_outer
    grid = (m_outer, n_cols // tn, nk, m_inner)
    a_specs = [_block(shape, imap, m_inner) for shape, imap in a_maps]
    w_spec = pl.BlockSpec((None, tk, tn), lambda o, n, k, m: (layer, k, n + n_off))
    scratch = [pltpu.VMEM((tk, tn), BF16)]
    if nk > 1:
        scratch.append(pltpu.VMEM((m_inner * tm, tn), F32))
    return pl.pallas_call(
        functools.partial(_matmul_kernel, n_a=len(a_list), n_extra=len(extra_args), nk=nk, tm=tm,
                          m_inner=m_inner, epilogue=epilogue),
        grid=grid,
        in_specs=a_specs + [w_spec] + [_block(s, f, m_inner) for s, f in extra_blocks],
        out_specs=_block(*out_block, m_inner),
        out_shape=out_shape,
        scratch_shapes=scratch,
        compiler_params=_params("arbitrary", "arbitrary", "arbitrary", "arbitrary"),
        name=name,
    )(*a_list, w, *extra_args)


def _head_proj(h, w_in, layer, col0, n_cols, tm, tn, tables, half, is_rope, q_scale, name):
    table_blocks = [((tm, LANES), lambda n, k, r, r0: (r, 0)) for _ in tables]
    epilogue = functools.partial(_head_epilogue, half=half, is_rope=is_rope, q_scale=q_scale)
    return _matmul(
        [h], [((tm, D_MODEL), lambda n, k, r, r0: (r, 0))], w_in, layer, col0, n_cols, tm, tn, 1,
        epilogue, list(tables), table_blocks,
        ((tn // LANES, tm, LANES), lambda n, k, r, r0: (n, r, 0)),
        jax.ShapeDtypeStruct((n_cols // LANES, TOKENS, LANES), BF16), name)


DIFF_QK_TN = 1024
REST_TN = 512
REST_QB_TILE0 = DIFF_W // REST_TN
REST_KV_TILE = (DIFF_W + SWA_Q_W) // REST_TN
REST_KB_GROUPS = SWA_KV_W // LANES
VA_BLK = 0
QB_BLK = DIFF_W // LANES
KB_BLK = QB_BLK + SWA_Q_W // LANES
VB_BLK = KB_BLK + SWA_KV_W // LANES


def _in_proj(h, w_in, layer, tables):
    ca, sa, cb, sb = tables
    qk_a = _head_proj(
        h, w_in, layer, 0, 2 * DIFF_W, 1024, DIFF_QK_TN, (ca, sa), DIFF_QK_DIM // 2,
        lambda n, j: True,
        lambda n: jnp.where(n == 0, DIFF_QK_DIM ** -0.5 * LOG2E, 1.0), "in_proj_qk_diff")

    def is_qb(n):
        return jnp.logical_and(n >= REST_QB_TILE0, n < REST_KV_TILE)

    def rest_is_rope(n, j):
        return n >= REST_QB_TILE0 if j < REST_KB_GROUPS else is_qb(n)

    rest = _head_proj(
        h, w_in, layer, 2 * DIFF_W, IN_WIDTH - 2 * DIFF_W, 2048, REST_TN, (cb, sb), HEAD_DIM // 2,
        rest_is_rope,
        lambda n: jnp.where(is_qb(n), HEAD_DIM ** -0.5 * LOG2E, 1.0), "in_proj_rest")
    return qk_a, rest


def _gated_residual_matmul(a_list, a_maps, w, layer, x, mod, gate_idx, tm, tn, m_outer, name):
    last = (w.shape[1] // MM_TK if len(a_list) == 1 else 1) - 1

    def xo_map(n, k, r, r0):
        return (jnp.where(k == last, r, r0), n)

    blocks = [((tm, tn), xo_map),
              ((None, SUBLANES, tn), lambda n, k, r, r0: (gate_idx, 0, n))]
    return _matmul(
        a_list, a_maps, w, layer, 0, D_MODEL, tm, tn, m_outer, _gated_residual_epilogue,
        [x, mod], blocks, ((tm, tn), xo_map),
        jax.ShapeDtypeStruct((TOKENS, D_MODEL), F32), name)


UP_TILE = (2048, 1024)
OUT_TILE = (1024, 1024)
DOWN_TILE = (2048, 512)
DOWN_M_OUTER = 2
K_BLOCKS = D_FF // MM_TK


def _out_proj(out_a, out_b, w_out, layer, x, mod, gate_idx):
    tm, tn = OUT_TILE
    a_maps = [((tm, a.shape[1]), lambda n, k, r, r0: (r, 0)) for a in (out_a, out_b)]
    return _gated_residual_matmul([out_a, out_b], a_maps, w_out, layer, x, mod, gate_idx, tm, tn, 1,
                                  "out_proj")


def _up_proj(h, w_up, layer):
    tm, tn = UP_TILE
    per_block = MM_TK // tn
    return _matmul(
        [h], [((tm, D_MODEL), lambda n, k, r, r0: (r, 0))], w_up, layer, 0, D_FF, tm, tn, 1,
        _relu2_epilogue, [], [],
        ((None, tm, tn), lambda n, k, r, r0: (n // per_block, r, n % per_block)),
        jax.ShapeDtypeStruct((K_BLOCKS, TOKENS, MM_TK), BF16), "up_proj")


def _down_proj(hidden, w_down, layer, x, mod, gate_idx):
    tm, tn = DOWN_TILE
    a_maps = [((None, tm, MM_TK), lambda n, k, r, r0: (k, r, 0))]
    return _gated_residual_matmul([hidden], a_maps, w_down, layer, x, mod, gate_idx, tm, tn,
                                  DOWN_M_OUTER, "down_proj")


DIFF_CHUNK = 256
ATTN_LOOKAHEAD = 2
_NT = (((1,), (1,)), ((), ()))
_TN = (((0,), (0,)), ((), ()))


def _diff_attn_kernel(q_ref, k_ref, v_ref, lam_ref, g_ref, o_ref, *, lam_init):
    k = k_ref[...]
    v = v_ref[...]
    lane = lax.broadcasted_iota(jnp.int32, (1, LANES), 1)
    lp = lam_ref[...]
    lam = (jnp.exp(jnp.sum(lp[0:1] * lp[1:2], axis=-1, keepdims=True))
           - jnp.exp(jnp.sum(lp[2:3] * lp[3:4], axis=-1, keepdims=True)) + lam_init)
    n_chunks = SEQ // DIFF_CHUNK
    results = {}

    def logits(t):
        chunk, second = divmod(t, 2)
        q = q_ref[chunk * DIFF_CHUNK:(chunk + 1) * DIFF_CHUNK, :]
        keep = (lane >= DIFF_QK_DIM) if second else (lane < DIFF_QK_DIM)
        qm = jnp.where(keep, q, jnp.zeros_like(q))
        return lax.dot_general(k, qm, _NT, preferred_element_type=F32)

    def softmax_pv(t, st):
        e = jnp.exp2(st - jnp.max(st, axis=0, keepdims=True))
        denom = jnp.sum(e, axis=0, keepdims=True)
        ot = lax.dot_general(v, e.astype(BF16), _TN, preferred_element_type=F32)
        results[t] = (ot, denom)

    _software_pipeline(2 * n_chunks, logits, softmax_pv, ATTN_LOOKAHEAD)
    for chunk in range(n_chunks):
        (ot0, l0), (ot1, l1) = results[2 * chunk], results[2 * chunk + 1]
        o = (ot0 * (1.0 / l0) - ot1 * (lam / l1)).T
        o = o * lax.rsqrt(jnp.mean(o * o, axis=-1, keepdims=True) + EPS)
        o_ref[chunk * DIFF_CHUNK:(chunk + 1) * DIFF_CHUNK, :] = (
            o * g_ref[...] * (1.0 - lam_init)).astype(o_ref.dtype)


def _diff_attention(qk_a, rest, diff_lambda, subln, layer):
    lam_init = 0.8 - 0.6 * math.exp(-0.3 * layer)
    return pl.pallas_call(
        functools.partial(_diff_attn_kernel, lam_init=lam_init),
        grid=(BATCH, N_HEADS_DIFF),
        in_specs=[
            pl.BlockSpec((None, SEQ, LANES), lambda b, h: (h, b, 0)),
            pl.BlockSpec((None, SEQ, LANES), lambda b, h: (N_HEADS_DIFF + h, b, 0)),
            pl.BlockSpec((None, SEQ, LANES), lambda b, h: (VA_BLK + h, b, 0)),
            pl.BlockSpec((None, 4, DIFF_QK_DIM), lambda b, h: (layer, 0, 0)),
            pl.BlockSpec((None, 1, HEAD_DIM), lambda b, h: (layer, 0, 0)),
        ],
        out_specs=pl.BlockSpec((SEQ, LANES), lambda b, h: (b, h)),
        out_shape=jax.ShapeDtypeStruct((TOKENS, DIFF_W), BF16),
        compiler_params=_params("arbitrary", "arbitrary"),
        name="diff_attn",
    )(qk_a, qk_a, rest, diff_lambda, subln.reshape(DEPTH, 1, HEAD_DIM))


SWA_BAND = 3 * BLOCK
SWA_QB = 8


def _swa_attn_kernel(q_ref, k_ref, v_ref, sink_ref, o_ref):
    step = pl.program_id(2)
    sink = sink_ref[...] * LOG2E

    def band_start(j):
        n = step * SWA_QB + j
        return n, pl.multiple_of(jnp.clip(n - 1, 0, SEQ // BLOCK - 3) * BLOCK, BLOCK)

    def logits(j):
        n, start = band_start(j)
        q = q_ref[:, j * BLOCK:(j + 1) * BLOCK, :].reshape(GQA_RATIO * BLOCK, HEAD_DIM)
        kb = k_ref[pl.ds(start, SWA_BAND), :]
        return lax.dot_general(kb, q, _NT, preferred_element_type=F32)

    def softmax_pv(j, st):
        n, start = band_start(j)
        kpos = start + lax.broadcasted_iota(jnp.int32, (SWA_BAND, BLOCK), 0)
        qpos = n * BLOCK + lax.broadcasted_iota(jnp.int32, (SWA_BAND, BLOCK), 1)
        bias = jnp.where(jnp.abs(kpos - qpos) <= WINDOW, 0.0, -jnp.inf).astype(F32)
        st = st + jnp.concatenate([bias] * GQA_RATIO, axis=1)
        mx = jnp.maximum(jnp.max(st, axis=0, keepdims=True), sink)
        e = jnp.exp2(st - mx)
        denom = jnp.sum(e, axis=0, keepdims=True) + jnp.exp2(sink - mx)
        vb = v_ref[pl.ds(start, SWA_BAND), :]
        ot = lax.dot_general(vb, e.astype(BF16), _TN, preferred_element_type=F32)
        ot = ot * (1.0 / denom)
        for r in range(GQA_RATIO):
            o_ref[j * BLOCK:(j + 1) * BLOCK, r * HEAD_DIM:(r + 1) * HEAD_DIM] = (
                ot[:, r * BLOCK:(r + 1) * BLOCK].T.astype(o_ref.dtype))

    _software_pipeline(SWA_QB, logits, softmax_pv, ATTN_LOOKAHEAD)


def _swa_attention(rest, sink):
    steps = SEQ // BLOCK // SWA_QB
    rows = SWA_QB * BLOCK
    sink_rows = jnp.repeat(sink, BLOCK).reshape(N_KV_SWA, 1, GQA_RATIO * BLOCK)
    return pl.pallas_call(
        _swa_attn_kernel,
        grid=(BATCH, N_KV_SWA, steps),
        in_specs=[
            pl.BlockSpec((GQA_RATIO, rows, LANES),
                         lambda b, g, n: (QB_BLK // GQA_RATIO + g, b * steps + n, 0)),
            pl.BlockSpec((None, SEQ, LANES), lambda b, g, n: (KB_BLK + g, b, 0)),
            pl.BlockSpec((None, SEQ, LANES), lambda b, g, n: (VB_BLK + g, b, 0)),
            pl.BlockSpec((None, 1, GQA_RATIO * BLOCK), lambda b, g, n: (g, 0, 0)),
        ],
        out_specs=pl.BlockSpec((rows, GQA_RATIO * HEAD_DIM), lambda b, g, n: (b * steps + n, g)),
        out_shape=jax.ShapeDtypeStruct((TOKENS, SWA_Q_W), BF16),
        compiler_params=_params("arbitrary", "arbitrary", "arbitrary"),
        name="swa_attn",
    )(rest, rest, rest, sink_rows)


def kernel(x, c, positions, ada_w, ada_b, norm_mix, w_in, diff_lambda, diff_subln, swa_sink,
           w_out, norm_mlp, w_up, w_down, final_norm):
    mod = _modulation(c, ada_w, ada_b)
    tables = _rope_tables(positions)
    xr = x.reshape(TOKENS, D_MODEL)
    for layer in range(DEPTH):
        base = layer * N_MOD
        h = _norm(xr, norm_mix[layer], mod, base + 1, base + 0)
        qk_a, rest = _in_proj(h, w_in, layer, tables)
        out_a = _diff_attention(qk_a, rest, diff_lambda, diff_subln, layer)
        out_b = _swa_attention(rest, swa_sink[layer])
        xr = _out_proj(out_a, out_b, w_out, layer, xr, mod, base + 2)
        h2 = _norm(xr, norm_mlp[layer], mod, base + 4, base + 3)
        hidden = _up_proj(h2, w_up, layer)
        xr = _down_proj(hidden, w_down, layer, xr, mod, base + 5)
    out = _norm(xr, final_norm, out_dtype=F32)
    return out.reshape(BATCH, SEQ, D_MODEL)
```

```python
import functools
import math

import jax
import jax.numpy as jnp
from jax import lax
from jax.experimental import pallas as pl
from jax.experimental.pallas import tpu as pltpu

D_MODEL = 2048
BATCH = 4
SEQ = 2048
DEPTH = 2
HEAD_DIM = 128
N_HEADS_DIFF = 8
DIFF_QK_DIM = 64
N_HEADS_SWA = 8
N_KV_SWA = 2
GQA_RATIO = N_HEADS_SWA // N_KV_SWA
WINDOW = 128
BLOCK = 128
D_FF = 4 * D_MODEL
ROPE_THETA = 10000.0
EPS = 1e-6
N_MOD = 6
DIFF_W = N_HEADS_DIFF * HEAD_DIM
SWA_Q_W = N_HEADS_SWA * HEAD_DIM
SWA_KV_W = N_KV_SWA * HEAD_DIM
IN_WIDTH = 3 * DIFF_W + SWA_Q_W + 2 * SWA_KV_W
TOKENS = BATCH * SEQ

LANES = 128
SUBLANES = 8
VMEM_LIMIT = 56 * 1024 * 1024

F32 = jnp.float32
BF16 = jnp.bfloat16
LOG2E = math.log2(math.e)


def _params(*sem):
    return pltpu.CompilerParams(dimension_semantics=sem, vmem_limit_bytes=VMEM_LIMIT)


def _software_pipeline(n_items, produce, consume, lookahead):
    pending = {}
    for t in range(n_items + lookahead):
        if t < n_items:
            pending[t] = produce(t)
        if t >= lookahead:
            consume(t - lookahead, pending.pop(t - lookahead))


MOD_TN = 1024


def _mod_kernel(c_ref, w_ref, b_ref, o_ref):
    c = c_ref[...]
    c_act = (c * jax.nn.sigmoid(c)).astype(BF16)
    o_ref[...] = jnp.dot(c_act, w_ref[...].astype(BF16), preferred_element_type=F32) + b_ref[...]


def _modulation(c, ada_w, ada_b):
    c_pad = jnp.pad(c, ((0, SUBLANES - BATCH), (0, 0)))
    per_chunk = D_MODEL // MOD_TN
    return pl.pallas_call(
        _mod_kernel,
        grid=(DEPTH, N_MOD * per_chunk),
        in_specs=[
            pl.BlockSpec((SUBLANES, D_MODEL), lambda l, j: (0, 0)),
            pl.BlockSpec((None, D_MODEL, MOD_TN), lambda l, j: (l, 0, j)),
            pl.BlockSpec((None, 1, MOD_TN), lambda l, j: (l, 0, j)),
        ],
        out_specs=pl.BlockSpec((None, SUBLANES, MOD_TN),
                               lambda l, j: (l * N_MOD + j // per_chunk, 0, j % per_chunk)),
        out_shape=jax.ShapeDtypeStruct((DEPTH * N_MOD, SUBLANES, D_MODEL), F32),
        compiler_params=_params("arbitrary", "arbitrary"),
        name="adaln_mod",
    )(c_pad, ada_w, ada_b.reshape(DEPTH, 1, N_MOD * D_MODEL))


ROPE_TM = 1024


def _rope_kernel(pos_ref, inv_a_ref, inv_b_ref, ca_ref, sa_ref, cb_ref, sb_ref):
    pos = pos_ref[...].astype(F32)
    lane = lax.broadcasted_iota(jnp.int32, (1, LANES), 1)
    ang_a = pos * inv_a_ref[...]
    ang_b = pos * inv_b_ref[...]
    sign_a = jnp.where((lane & (DIFF_QK_DIM // 2)) == 0, -1.0, 1.0).astype(F32)
    sign_b = jnp.where((lane & (HEAD_DIM // 2)) == 0, -1.0, 1.0).astype(F32)
    ca_ref[...] = jnp.cos(ang_a)
    sa_ref[...] = jnp.sin(ang_a) * sign_a
    cb_ref[...] = jnp.cos(ang_b)
    sb_ref[...] = jnp.sin(ang_b) * sign_b


def _rope_tables(positions):
    def inv_freq(dim):
        return ROPE_THETA ** (-jnp.arange(0, dim, 2, dtype=F32) / dim)
    inv_a = jnp.tile(inv_freq(DIFF_QK_DIM), LANES // (DIFF_QK_DIM // 2)).reshape(1, LANES)
    inv_b = jnp.tile(inv_freq(HEAD_DIM), LANES // (HEAD_DIM // 2)).reshape(1, LANES)
    tab = jax.ShapeDtypeStruct((TOKENS, LANES), F32)
    row = pl.BlockSpec((ROPE_TM, LANES), lambda m: (m, 0))
    const = pl.BlockSpec((1, LANES), lambda m: (0, 0))
    return pl.pallas_call(
        _rope_kernel,
        grid=(TOKENS // ROPE_TM,),
        in_specs=[pl.BlockSpec((ROPE_TM, 1), lambda m: (m, 0)), const, const],
        out_specs=[row, row, row, row],
        out_shape=[tab, tab, tab, tab],
        compiler_params=_params("arbitrary"),
        name="rope_tables",
    )(positions.reshape(TOKENS, 1), inv_a, inv_b)


NORM_TM = 1024


def _norm_kernel(x_ref, g_ref, *rest, modulated):
    x = x_ref[...]
    y = x * lax.rsqrt(jnp.mean(x * x, axis=-1, keepdims=True) + EPS)
    y = y * g_ref[...]
    if modulated:
        sc_ref, sh_ref, o_ref = rest
        b = (pl.program_id(0) * NORM_TM) // SEQ
        y = y * (1.0 + sc_ref[pl.ds(b, 1), :]) + sh_ref[pl.ds(b, 1), :]
    else:
        (o_ref,) = rest
    o_ref[...] = y.astype(o_ref.dtype)


def _norm(x, gain, mod=None, sc_idx=None, sh_idx=None, out_dtype=BF16):
    in_specs = [pl.BlockSpec((NORM_TM, D_MODEL), lambda m: (m, 0)),
                pl.BlockSpec((1, D_MODEL), lambda m: (0, 0))]
    args = [x, gain.reshape(1, D_MODEL)]
    if mod is not None:
        in_specs += [pl.BlockSpec((None, SUBLANES, D_MODEL), lambda m: (sc_idx, 0, 0)),
                     pl.BlockSpec((None, SUBLANES, D_MODEL), lambda m: (sh_idx, 0, 0))]
        args += [mod, mod]
    return pl.pallas_call(
        functools.partial(_norm_kernel, modulated=mod is not None),
        grid=(TOKENS // NORM_TM,),
        in_specs=in_specs,
        out_specs=pl.BlockSpec((NORM_TM, D_MODEL), lambda m: (m, 0)),
        out_shape=jax.ShapeDtypeStruct((TOKENS, D_MODEL), out_dtype),
        compiler_params=_params("arbitrary"),
        name="rmsnorm_mod" if mod is not None else "rmsnorm_final",
    )(*args)


MM_TK = 2048
MM_SUB = 512


def _swap_halves(x, half):
    if 2 * half == LANES:
        return pltpu.roll(x, half, 1)
    lane = lax.broadcasted_iota(jnp.int32, (1, LANES), 1)
    return jnp.where((lane & half) == 0,
                     pltpu.roll(x, LANES - half, 1), pltpu.roll(x, half, 1))


def _head_epilogue(acc, rows, row0, n, refs, o_ref, *, half, is_rope, q_scale):
    cos_ref, sin_ref = refs
    for j in range(o_ref.shape[0]):
        x = acc[:, j * LANES:(j + 1) * LANES]
        rotate = is_rope(n, j)
        if rotate is not False:
            cos = cos_ref[rows, :]
            sin_signed = sin_ref[rows, :]
            if rotate is not True:
                cos = jnp.where(rotate, cos, 1.0)
                sin_signed = jnp.where(rotate, sin_signed, 0.0)
            x = (x * cos + _swap_halves(x, half) * sin_signed) * q_scale(n)
        o_ref[j, rows, :] = x.astype(o_ref.dtype)


def _gated_residual_epilogue(acc, rows, row0, n, refs, o_ref):
    x_ref, g_ref = refs
    o_ref[rows, :] = x_ref[rows, :] + g_ref[pl.ds(row0 // SEQ, 1), :] * acc


def _relu2_epilogue(acc, rows, row0, n, refs, o_ref):
    r = jnp.maximum(acc, 0.0)
    o_ref[rows, :] = (r * r).astype(o_ref.dtype)


def _matmul_kernel(*refs, n_a, n_extra, nk, tm, m_inner, epilogue):
    a_refs = refs[:n_a]
    w_ref = refs[n_a]
    extra = refs[n_a + 1:n_a + 1 + n_extra]
    o_ref = refs[n_a + 1 + n_extra]
    scratch = refs[n_a + 2 + n_extra:]
    wb_ref = scratch[0]
    outer, n, k, m = (pl.program_id(i) for i in range(4))

    if nk > 1:
        acc_ref = scratch[1]

        first_nk = jnp.logical_and(n == 0, k == 0)
        @pl.when(jnp.logical_and(jnp.logical_and(outer == 0, m == 0), first_nk))
        def _():
            acc_ref[...] = jnp.zeros_like(acc_ref)

    @pl.when(m == 0)
    def _():
        wb_ref[...] = w_ref[...].astype(BF16)

    def product(t):
        rows = slice(t * MM_SUB, (t + 1) * MM_SUB)
        part = None
        off = 0
        for a_ref in a_refs:
            ka = a_ref.shape[1]
            d = jnp.dot(a_ref[rows, :], wb_ref[off:off + ka, :], preferred_element_type=F32)
            part = d if part is None else part + d
            off += ka
        return part

    def finish(t, part):
        rows = slice(t * MM_SUB, (t + 1) * MM_SUB)
        local_row0 = m * tm + t * MM_SUB
        if nk > 1:
            arows = pl.ds(pl.multiple_of(local_row0, MM_SUB), MM_SUB)
            part = jnp.where(k == 0, 0.0, acc_ref[arows, :]) + part
            acc_ref[arows, :] = part
        epilogue(part, rows, outer * (m_inner * tm) + local_row0, n, extra, o_ref)

    _software_pipeline(tm // MM_SUB, product, finish, lookahead=1)


def _block(shape, index_map, m_inner):
    return pl.BlockSpec(
        shape, lambda o, n, k, m: index_map(n, k, o * m_inner + m, o * m_inner))


def _matmul(a_list, a_maps, w, layer, col0, n_cols, tm, tn, m_outer, epilogue, extra_args,
            extra_blocks, out_block, out_shape, name):
    k_total = w.shape[1]
    assert col0 % tn == 0 and n_cols % tn == 0 and tm % MM_SUB == 0
    assert TOKENS % (tm * m_outer) == 0
    n_off = col0 // tn
    nk = k_total // MM_TK if len(a_list) == 1 else 1
    tk = k_total // nk
    m_inner = TOKENS // tm // m_outer
    grid = (m_outer, n_cols // tn, nk, m_inner)
    a_specs = [_block(shape, imap, m_inner) for shape, imap in a_maps]
    w_spec = pl.BlockSpec((None, tk, tn), lambda o, n, k, m: (layer, k, n + n_off))
    scratch = [pltpu.VMEM((tk, tn), BF16)]
    if nk > 1:
        scratch.append(pltpu.VMEM((m_inner * tm, tn), F32))
    return pl.pallas_call(
        functools.partial(_matmul_kernel, n_a=len(a_list), n_extra=len(extra_args), nk=nk, tm=tm,
                          m_inner=m_inner, epilogue=epilogue),
        grid=grid,
        in_specs=a_specs + [w_spec] + [_block(s, f, m_inner) for s, f in extra_blocks],
        out_specs=_block(*out_block, m_inner),
        out_shape=out_shape,
        scratch_shapes=scratch,
        compiler_params=_params("arbitrary", "arbitrary", "arbitrary", "arbitrary"),
        name=name,
    )(*a_list, w, *extra_args)


def _head_proj(h, w_in, layer, col0, n_cols, tm, tn, tables, half, is_rope, q_scale, name):
    table_blocks = [((tm, LANES), lambda n, k, r, r0: (r, 0)) for _ in tables]
    epilogue = functools.partial(_head_epilogue, half=half, is_rope=is_rope, q_scale=q_scale)
    return _matmul(
        [h], [((tm, D_MODEL), lambda n, k, r, r0: (r, 0))], w_in, layer, col0, n_cols, tm, tn, 1,
        epilogue, list(tables), table_blocks,
        ((tn // LANES, tm, LANES), lambda n, k, r, r0: (n, r, 0)),
        jax.ShapeDtypeStruct((n_cols // LANES, TOKENS, LANES), BF16), name)


DIFF_QK_TN = 1024
REST_TN = 512
REST_QB_TILE0 = DIFF_W // REST_TN
REST_KV_TILE = (DIFF_W + SWA_Q_W) // REST_TN
REST_KB_GROUPS = SWA_KV_W // LANES
VA_BLK = 0
QB_BLK = DIFF_W // LANES
KB_BLK = QB_BLK + SWA_Q_W // LANES
VB_BLK = KB_BLK + SWA_KV_W // LANES


def _in_proj(h, w_in, layer, tables):
    ca, sa, cb, sb = tables
    qk_a = _head_proj(
        h, w_in, layer, 0, 2 * DIFF_W, 2048, DIFF_QK_TN, (ca, sa), DIFF_QK_DIM // 2,
        lambda n, j: True,
        lambda n: jnp.where(n == 0, DIFF_QK_DIM ** -0.5 * LOG2E, 1.0), "in_proj_qk_diff")

    def is_qb(n):
        return jnp.logical_and(n >= REST_QB_TILE0, n < REST_KV_TILE)

    def rest_is_rope(n, j):
        return n >= REST_QB_TILE0 if j < REST_KB_GROUPS else is_qb(n)

    rest = _head_proj(
        h, w_in, layer, 2 * DIFF_W, IN_WIDTH - 2 * DIFF_W, 2048, REST_TN, (cb, sb), HEAD_DIM // 2,
        rest_is_rope,
        lambda n: jnp.where(is_qb(n), HEAD_DIM ** -0.5 * LOG2E, 1.0), "in_proj_rest")
    return qk_a, rest


def _gated_residual_matmul(a_list, a_maps, w, layer, x, mod, gate_idx, tm, tn, m_outer, name):
    last = (w.shape[1] // MM_TK if len(a_list) == 1 else 1) - 1

    def xo_map(n, k, r, r0):
        return (jnp.where(k == last, r, r0), n)

    blocks = [((tm, tn), xo_map),
              ((None, SUBLANES, tn), lambda n, k, r, r0: (gate_idx, 0, n))]
    return _matmul(
        a_list, a_maps, w, layer, 0, D_MODEL, tm, tn, m_outer, _gated_residual_epilogue,
        [x, mod], blocks, ((tm, tn), xo_map),
        jax.ShapeDtypeStruct((TOKENS, D_MODEL), F32), name)


UP_TILE = (2048, 1024)
OUT_TILE = (1024, 1024)
DOWN_TILE = (2048, 512)
DOWN_M_OUTER = 2
K_BLOCKS = D_FF // MM_TK


def _out_proj(out_a, out_b, w_out, layer, x, mod, gate_idx):
    tm, tn = OUT_TILE
    a_maps = [((tm, a.shape[1]), lambda n, k, r, r0: (r, 0)) for a in (out_a, out_b)]
    return _gated_residual_matmul([out_a, out_b], a_maps, w_out, layer, x, mod, gate_idx, tm, tn, 1,
                                  "out_proj")


def _up_proj(h, w_up, layer):
    tm, tn = UP_TILE
    per_block = MM_TK // tn
    return _matmul(
        [h], [((tm, D_MODEL), lambda n, k, r, r0: (r, 0))], w_up, layer, 0, D_FF, tm, tn, 1,
        _relu2_epilogue, [], [],
        ((None, tm, tn), lambda n, k, r, r0: (n // per_block, r, n % per_block)),
        jax.ShapeDtypeStruct((K_BLOCKS, TOKENS, MM_TK), BF16), "up_proj")


def _down_proj(hidden, w_down, layer, x, mod, gate_idx):
    tm, tn = DOWN_TILE
    a_maps = [((None, tm, MM_TK), lambda n, k, r, r0: (k, r, 0))]
    return _gated_residual_matmul([hidden], a_maps, w_down, layer, x, mod, gate_idx, tm, tn,
                                  DOWN_M_OUTER, "down_proj")


DIFF_CHUNK = 512
ATTN_LOOKAHEAD = 3
_NT = (((1,), (1,)), ((), ()))
_TN = (((0,), (0,)), ((), ()))


def _diff_attn_kernel(q_ref, k_ref, v_ref, lam_ref, g_ref, o_ref, *, lam_init):
    k = k_ref[...]
    v = v_ref[...]
    lane = lax.broadcasted_iota(jnp.int32, (1, LANES), 1)
    lp = lam_ref[...]
    lam = (jnp.exp(jnp.sum(lp[0:1] * lp[1:2], axis=-1, keepdims=True))
           - jnp.exp(jnp.sum(lp[2:3] * lp[3:4], axis=-1, keepdims=True)) + lam_init)
    n_chunks = SEQ // DIFF_CHUNK
    results = {}

    def logits(t):
        chunk, second = divmod(t, 2)
        q = q_ref[chunk * DIFF_CHUNK:(chunk + 1) * DIFF_CHUNK, :]
        keep = (lane >= DIFF_QK_DIM) if second else (lane < DIFF_QK_DIM)
        qm = jnp.where(keep, q, jnp.zeros_like(q))
        return lax.dot_general(k, qm, _NT, preferred_element_type=F32)

    def softmax_pv(t, st):
        e = jnp.exp2(st - jnp.max(st, axis=0, keepdims=True))
        denom = jnp.sum(e, axis=0, keepdims=True)
        ot = lax.dot_general(v, e.astype(BF16), _TN, preferred_element_type=F32)
        results[t] = (ot, denom)

    _software_pipeline(2 * n_chunks, logits, softmax_pv, ATTN_LOOKAHEAD)
    for chunk in range(n_chunks):
        (ot0, l0), (ot1, l1) = results[2 * chunk], results[2 * chunk + 1]
        o = (ot0 * (1.0 / l0) - ot1 * (lam / l1)).T
        o = o * lax.rsqrt(jnp.mean(o * o, axis=-1, keepdims=True) + EPS)
        o_ref[chunk * DIFF_CHUNK:(chunk + 1) * DIFF_CHUNK, :] = (
            o * g_ref[...] * (1.0 - lam_init)).astype(o_ref.dtype)


def _diff_attention(qk_a, rest, diff_lambda, subln, layer):
    lam_init = 0.8 - 0.6 * math.exp(-0.3 * layer)
    return pl.pallas_call(
        functools.partial(_diff_attn_kernel, lam_init=lam_init),
        grid=(BATCH, N_HEADS_DIFF),
        in_specs=[
            pl.BlockSpec((None, SEQ, LANES), lambda b, h: (h, b, 0)),
            pl.BlockSpec((None, SEQ, LANES), lambda b, h: (N_HEADS_DIFF + h, b, 0)),
            pl.BlockSpec((None, SEQ, LANES), lambda b, h: (VA_BLK + h, b, 0)),
            pl.BlockSpec((None, 4, DIFF_QK_DIM), lambda b, h: (layer, 0, 0)),
            pl.BlockSpec((None, 1, HEAD_DIM), lambda b, h: (layer, 0, 0)),
        ],
        out_specs=pl.BlockSpec((SEQ, LANES), lambda b, h: (b, h)),
        out_shape=jax.ShapeDtypeStruct((TOKENS, DIFF_W), BF16),
        compiler_params=_params("arbitrary", "arbitrary"),
        name="diff_attn",
    )(qk_a, qk_a, rest, diff_lambda, subln.reshape(DEPTH, 1, HEAD_DIM))


SWA_BAND = 3 * BLOCK
SWA_QB = 16


def _swa_attn_kernel(q_ref, k_ref, v_ref, sink_ref, o_ref):
    step = pl.program_id(2)
    sink = sink_ref[...] * LOG2E

    def band_start(j):
        n = step * SWA_QB + j
        return n, pl.multiple_of(jnp.clip(n - 1, 0, SEQ // BLOCK - 3) * BLOCK, BLOCK)

    def logits(j):
        n, start = band_start(j)
        q = q_ref[:, j * BLOCK:(j + 1) * BLOCK, :].reshape(GQA_RATIO * BLOCK, HEAD_DIM)
        kb = k_ref[pl.ds(start, SWA_BAND), :]
        return lax.dot_general(kb, q, _NT, preferred_element_type=F32)

    def softmax_pv(j, st):
        n, start = band_start(j)
        kpos = start + lax.broadcasted_iota(jnp.int32, (SWA_BAND, BLOCK), 0)
        qpos = n * BLOCK + lax.broadcasted_iota(jnp.int32, (SWA_BAND, BLOCK), 1)
        bias = jnp.where(jnp.abs(kpos - qpos) <= WINDOW, 0.0, -jnp.inf).astype(F32)
        st = st + jnp.concatenate([bias] * GQA_RATIO, axis=1)
        mx = jnp.maximum(jnp.max(st, axis=0, keepdims=True), sink)
        e = jnp.exp2(st - mx)
        denom = jnp.sum(e, axis=0, keepdims=True) + jnp.exp2(sink - mx)
        vb = v_ref[pl.ds(start, SWA_BAND), :]
        ot = lax.dot_general(vb, e.astype(BF16), _TN, preferred_element_type=F32)
        ot = ot * (1.0 / denom)
        for r in range(GQA_RATIO):
            o_ref[j * BLOCK:(j + 1) * BLOCK, r * HEAD_DIM:(r + 1) * HEAD_DIM] = (
                ot[:, r * BLOCK:(r + 1) * BLOCK].T.astype(o_ref.dtype))

    _software_pipeline(SWA_QB, logits, softmax_pv, ATTN_LOOKAHEAD)


def _swa_attention(rest, sink):
    steps = SEQ // BLOCK // SWA_QB
    rows = SWA_QB * BLOCK
    sink_rows = jnp.repeat(sink, BLOCK).reshape(N_KV_SWA, 1, GQA_RATIO * BLOCK)
    return pl.pallas_call(
        _swa_attn_kernel,
        grid=(BATCH, N_KV_SWA, steps),
        in_specs=[
            pl.BlockSpec((GQA_RATIO, rows, LANES),
                         lambda b, g, n: (QB_BLK // GQA_RATIO + g, b * steps + n, 0)),
            pl.BlockSpec((None, SEQ, LANES), lambda b, g, n: (KB_BLK + g, b, 0)),
            pl.BlockSpec((None, SEQ, LANES), lambda b, g, n: (VB_BLK + g, b, 0)),
            pl.BlockSpec((None, 1, GQA_RATIO * BLOCK), lambda b, g, n: (g, 0, 0)),
        ],
        out_specs=pl.BlockSpec((rows, GQA_RATIO * HEAD_DIM), lambda b, g, n: (b * steps + n, g)),
        out_shape=jax.ShapeDtypeStruct((TOKENS, SWA_Q_W), BF16),
        compiler_params=_params("arbitrary", "arbitrary", "arbitrary"),
        name="swa_attn",
    )(rest, rest, rest, sink_rows)


def kernel(x, c, positions, ada_w, ada_b, norm_mix, w_in, diff_lambda, diff_subln, swa_sink,
           w_out, norm_mlp, w_up, w_down, final_norm):
    mod = _modulation(c, ada_w, ada_b)
    tables = _rope_tables(positions)
    xr = x.reshape(TOKENS, D_MODEL)
    for layer in range(DEPTH):
        base = layer * N_MOD
        h = _norm(xr, norm_mix[layer], mod, base + 1, base + 0)
        qk_a, rest = _in_proj(h, w_in, layer, tables)
        out_a = _diff_attention(qk_a, rest, diff_lambda, diff_subln, layer)
        out_b = _swa_attention(rest, swa_sink[layer])
        xr = _out_proj(out_a, out_b, w_out, layer, xr, mod, base + 2)
        h2 = _norm(xr, norm_mlp[layer], mod, base + 4, base + 3)
        hidden = _up_proj(h2, w_up, layer)
        xr = _down_proj(hidden, w_down, layer, xr, mod, base + 5)
    out = _norm(xr, final_norm, out_dtype=F32)
    return out.reshape(BATCH, SEQ, D_MODEL)
```

```python
import functools
import math

import jax
import jax.numpy as jnp
from jax import lax
from jax.experimental import pallas as pl
from jax.experimental.pallas import tpu as pltpu

D_MODEL = 2048
BATCH = 4
SEQ = 2048
DEPTH = 2
HEAD_DIM = 128
N_HEADS_DIFF = 8
DIFF_QK_DIM = 64
N_HEADS_SWA = 8
N_KV_SWA = 2
GQA_RATIO = N_HEADS_SWA // N_KV_SWA
WINDOW = 128
BLOCK = 128
D_FF = 4 * D_MODEL
ROPE_THETA = 10000.0
EPS = 1e-6
N_MOD = 6
DIFF_W = N_HEADS_DIFF * HEAD_DIM
SWA_Q_W = N_HEADS_SWA * HEAD_DIM
SWA_KV_W = N_KV_SWA * HEAD_DIM
IN_WIDTH = 3 * DIFF_W + SWA_Q_W + 2 * SWA_KV_W
TOKENS = BATCH * SEQ

LANES = 128
SUBLANES = 8
VMEM_LIMIT = 56 * 1024 * 1024

F32 = jnp.float32
BF16 = jnp.bfloat16
LOG2E = math.log2(math.e)


def _params(*sem):
    return pltpu.CompilerParams(dimension_semantics=sem, vmem_limit_bytes=VMEM_LIMIT)


def _software_pipeline(n_items, produce, consume, lookahead):
    pending = {}
    for t in range(n_items + lookahead):
        if t < n_items:
            pending[t] = produce(t)
        if t >= lookahead:
            consume(t - lookahead, pending.pop(t - lookahead))


MOD_TN = 1024


def _mod_kernel(c_ref, w_ref, b_ref, o_ref):
    c = c_ref[...]
    c_act = (c * jax.nn.sigmoid(c)).astype(BF16)
    o_ref[...] = jnp.dot(c_act, w_ref[...].astype(BF16), preferred_element_type=F32) + b_ref[...]


def _modulation(c, ada_w, ada_b):
    c_pad = jnp.pad(c, ((0, SUBLANES - BATCH), (0, 0)))
    per_chunk = D_MODEL // MOD_TN
    return pl.pallas_call(
        _mod_kernel,
        grid=(DEPTH, N_MOD * per_chunk),
        in_specs=[
            pl.BlockSpec((SUBLANES, D_MODEL), lambda l, j: (0, 0)),
            pl.BlockSpec((None, D_MODEL, MOD_TN), lambda l, j: (l, 0, j)),
            pl.BlockSpec((None, 1, MOD_TN), lambda l, j: (l, 0, j)),
        ],
        out_specs=pl.BlockSpec((None, SUBLANES, MOD_TN),
                               lambda l, j: (l * N_MOD + j // per_chunk, 0, j % per_chunk)),
        out_shape=jax.ShapeDtypeStruct((DEPTH * N_MOD, SUBLANES, D_MODEL), F32),
        compiler_params=_params("arbitrary", "arbitrary"),
        name="adaln_mod",
    )(c_pad, ada_w, ada_b.reshape(DEPTH, 1, N_MOD * D_MODEL))


ROPE_TM = 1024


def _rope_kernel(pos_ref, inv_a_ref, inv_b_ref, ca_ref, sa_ref, cb_ref, sb_ref):
    pos = pos_ref[...].astype(F32)
    lane = lax.broadcasted_iota(jnp.int32, (1, LANES), 1)
    ang_a = pos * inv_a_ref[...]
    ang_b = pos * inv_b_ref[...]
    sign_a = jnp.where((lane & (DIFF_QK_DIM // 2)) == 0, -1.0, 1.0).astype(F32)
    sign_b = jnp.where((lane & (HEAD_DIM // 2)) == 0, -1.0, 1.0).astype(F32)
    ca_ref[...] = jnp.cos(ang_a)
    sa_ref[...] = jnp.sin(ang_a) * sign_a
    cb_ref[...] = jnp.cos(ang_b)
    sb_ref[...] = jnp.sin(ang_b) * sign_b


def _rope_tables(positions):
    def inv_freq(dim):
        return ROPE_THETA ** (-jnp.arange(0, dim, 2, dtype=F32) / dim)
    inv_a = jnp.tile(inv_freq(DIFF_QK_DIM), LANES // (DIFF_QK_DIM // 2)).reshape(1, LANES)
    inv_b = jnp.tile(inv_freq(HEAD_DIM), LANES // (HEAD_DIM // 2)).reshape(1, LANES)
    tab = jax.ShapeDtypeStruct((TOKENS, LANES), F32)
    row = pl.BlockSpec((ROPE_TM, LANES), lambda m: (m, 0))
    const = pl.BlockSpec((1, LANES), lambda m: (0, 0))
    return pl.pallas_call(
        _rope_kernel,
        grid=(TOKENS // ROPE_TM,),
        in_specs=[pl.BlockSpec((ROPE_TM, 1), lambda m: (m, 0)), const, const],
        out_specs=[row, row, row, row],
        out_shape=[tab, tab, tab, tab],
        compiler_params=_params("arbitrary"),
        name="rope_tables",
    )(positions.reshape(TOKENS, 1), inv_a, inv_b)


NORM_TM = 1024


def _norm_kernel(x_ref, g_ref, *rest, modulated):
    x = x_ref[...]
    y = x * lax.rsqrt(jnp.mean(x * x, axis=-1, keepdims=True) + EPS)
    y = y * g_ref[...]
    if modulated:
        sc_ref, sh_ref, o_ref = rest
        b = (pl.program_id(0) * NORM_TM) // SEQ
        y = y * (1.0 + sc_ref[pl.ds(b, 1), :]) + sh_ref[pl.ds(b, 1), :]
    else:
        (o_ref,) = rest
    o_ref[...] = y.astype(o_ref.dtype)


def _norm(x, gain, mod=None, sc_idx=None, sh_idx=None, out_dtype=BF16):
    in_specs = [pl.BlockSpec((NORM_TM, D_MODEL), lambda m: (m, 0)),
                pl.BlockSpec((1, D_MODEL), lambda m: (0, 0))]
    args = [x, gain.reshape(1, D_MODEL)]
    if mod is not None:
        in_specs += [pl.BlockSpec((None, SUBLANES, D_MODEL), lambda m: (sc_idx, 0, 0)),
                     pl.BlockSpec((None, SUBLANES, D_MODEL), lambda m: (sh_idx, 0, 0))]
        args += [mod, mod]
    return pl.pallas_call(
        functools.partial(_norm_kernel, modulated=mod is not None),
        grid=(TOKENS // NORM_TM,),
        in_specs=in_specs,
        out_specs=pl.BlockSpec((NORM_TM, D_MODEL), lambda m: (m, 0)),
        out_shape=jax.ShapeDtypeStruct((TOKENS, D_MODEL), out_dtype),
        compiler_params=_params("arbitrary"),
        name="rmsnorm_mod" if mod is not None else "rmsnorm_final",
    )(*args)


MM_TK = 2048
MM_SUB = 512


def _swap_halves(x, half):
    if 2 * half == LANES:
        return pltpu.roll(x, half, 1)
    lane = lax.broadcasted_iota(jnp.int32, (1, LANES), 1)
    return jnp.where((lane & half) == 0,
                     pltpu.roll(x, LANES - half, 1), pltpu.roll(x, half, 1))


def _head_epilogue(acc, rows, row0, n, refs, outs, bias, *, half, is_rope, q_scale):
    cos_ref, sin_ref = refs
    (o_ref,) = outs
    for j in range(o_ref.shape[0]):
        x = acc[:, j * LANES:(j + 1) * LANES]
        rotate = is_rope(n, j)
        if rotate is not False:
            cos = cos_ref[rows, :]
            sin_signed = sin_ref[rows, :]
            if rotate is not True:
                cos = jnp.where(rotate, cos, 1.0)
                sin_signed = jnp.where(rotate, sin_signed, 0.0)
            x = (x * cos + _swap_halves(x, half) * sin_signed) * q_scale(n)
        o_ref[j, rows, :] = x.astype(o_ref.dtype)


def _gated_residual_epilogue(acc, rows, row0, n, refs, outs, bias):
    x_ref, g_ref = refs
    outs[0][rows, :] = x_ref[rows, :] + g_ref[pl.ds(row0 // SEQ, 1), :] * acc


def _gated_residual_prenorm_epilogue(acc, rows, row0, n, refs, outs, bias):
    x_ref, g_ref, gain_ref, sc_ref = refs
    x_out, xs_out, ss_out = outs
    b = row0 // SEQ
    xn = x_ref[rows, :] + g_ref[pl.ds(b, 1), :] * acc
    x_out[rows, :] = xn
    xs_out[rows, :] = (xn * (gain_ref[...] * (1.0 + sc_ref[pl.ds(b, 1), :]))).astype(xs_out.dtype)
    ss_out[rows, :] = jnp.broadcast_to(jnp.sum(xn * xn, axis=-1, keepdims=True),
                                       (rows.stop - rows.start, LANES))


def _relu2_epilogue(acc, rows, row0, n, refs, outs, bias):
    r = jnp.maximum(acc, 0.0)
    outs[0][rows, :] = (r * r).astype(outs[0].dtype)


def _postnorm_relu2_epilogue(acc, rows, row0, n, refs, outs, bias):
    ss_ref = refs[0]
    ss = ss_ref[0, rows, :]
    for i in range(1, ss_ref.shape[0]):
        ss = ss + ss_ref[i, rows, :]
    inv = lax.rsqrt(ss * (1.0 / D_MODEL) + EPS)
    inv = jnp.concatenate([inv] * (acc.shape[1] // LANES), axis=1)
    r = jnp.maximum(acc * inv + bias[pl.ds(row0 // SEQ, 1), :], 0.0)
    outs[0][rows, :] = (r * r).astype(outs[0].dtype)


def _matmul_kernel(*refs, n_a, n_extra, n_out, nk, tm, m_inner, epilogue, shift_idx):
    a_refs = refs[:n_a]
    w_ref = refs[n_a]
    extra = refs[n_a + 1:n_a + 1 + n_extra]
    outs = refs[n_a + 1 + n_extra:n_a + 1 + n_extra + n_out]
    scratch = refs[n_a + 1 + n_extra + n_out:]
    wb_ref = scratch[0]
    bias_ref = scratch[-1] if shift_idx is not None else None
    outer, n, k, m = (pl.program_id(i) for i in range(4))

    if nk > 1:
        acc_ref = scratch[1]

        first_nk = jnp.logical_and(n == 0, k == 0)
        @pl.when(jnp.logical_and(jnp.logical_and(outer == 0, m == 0), first_nk))
        def _():
            acc_ref[...] = jnp.zeros_like(acc_ref)

    @pl.when(m == 0)
    def _():
        wb_ref[...] = w_ref[...].astype(BF16)
        if shift_idx is not None:
            bias_ref[...] = jnp.dot(extra[shift_idx][...].astype(BF16), wb_ref[...],
                                    preferred_element_type=F32)

    def product(t):
        rows = slice(t * MM_SUB, (t + 1) * MM_SUB)
        part = None
        off = 0
        for a_ref in a_refs:
            ka = a_ref.shape[1]
            d = jnp.dot(a_ref[rows, :], wb_ref[off:off + ka, :], preferred_element_type=F32)
            part = d if part is None else part + d
            off += ka
        return part

    def finish(t, part):
        rows = slice(t * MM_SUB, (t + 1) * MM_SUB)
        local_row0 = m * tm + t * MM_SUB
        if nk > 1:
            arows = pl.ds(pl.multiple_of(local_row0, MM_SUB), MM_SUB)
            part = jnp.where(k == 0, 0.0, acc_ref[arows, :]) + part
            acc_ref[arows, :] = part
        epilogue(part, rows, outer * (m_inner * tm) + local_row0, n, extra, outs, bias_ref)

    _software_pipeline(tm // MM_SUB, product, finish, lookahead=1)


def _block(shape, index_map, m_inner):
    return pl.BlockSpec(
        shape, lambda o, n, k, m: index_map(n, k, o * m_inner + m, o * m_inner))


def _matmul(a_list, a_maps, w, layer, col0, n_cols, tm, tn, m_outer, epilogue, extra_args,
            extra_blocks, out_blocks, out_shapes, name, shift_idx=None):
    k_total = w.shape[1]
    assert col0 % tn == 0 and n_cols % tn == 0 and tm % MM_SUB == 0
    assert TOKENS % (tm * m_outer) == 0
    n_off = col0 // tn
    nk = k_total // MM_TK if len(a_list) == 1 else 1
    tk = k_total // nk
    m_inner = TOKENS // tm // m_outer
    grid = (m_outer, n_cols // tn, nk, m_inner)
    a_specs = [_block(shape, imap, m_inner) for shape, imap in a_maps]
    w_spec = pl.BlockSpec((None, tk, tn), lambda o, n, k, m: (layer, k, n + n_off))
    scratch = [pltpu.VMEM((tk, tn), BF16)]
    if nk > 1:
        scratch.append(pltpu.VMEM((m_inner * tm, tn), F32))
    if shift_idx is not None:
        assert nk == 1
        scratch.append(pltpu.VMEM((SUBLANES, tn), F32))
    return pl.pallas_call(
        functools.partial(_matmul_kernel, n_a=len(a_list), n_extra=len(extra_args),
                          n_out=len(out_blocks), nk=nk, tm=tm, m_inner=m_inner, epilogue=epilogue,
                          shift_idx=shift_idx),
        grid=grid,
        in_specs=a_specs + [w_spec] + [_block(s, f, m_inner) for s, f in extra_blocks],
        out_specs=[_block(s, f, m_inner) for s, f in out_blocks],
        out_shape=out_shapes,
        scratch_shapes=scratch,
        compiler_params=_params("arbitrary", "arbitrary", "arbitrary", "arbitrary"),
        name=name,
    )(*a_list, w, *extra_args)


def _head_proj(h, w_in, layer, col0, n_cols, tm, tn, tables, half, is_rope, q_scale, name):
    table_blocks = [((tm, LANES), lambda n, k, r, r0: (r, 0)) for _ in tables]
    epilogue = functools.partial(_head_epilogue, half=half, is_rope=is_rope, q_scale=q_scale)
    return _matmul(
        [h], [((tm, D_MODEL), lambda n, k, r, r0: (r, 0))], w_in, layer, col0, n_cols, tm, tn, 1,
        epilogue, list(tables), table_blocks,
        [((tn // LANES, tm, LANES), lambda n, k, r, r0: (n, r, 0))],
        [jax.ShapeDtypeStruct((n_cols // LANES, TOKENS, LANES), BF16)], name)[0]


DIFF_QK_TN = 1024
REST_TN = 512
REST_QB_TILE0 = DIFF_W // REST_TN
REST_KV_TILE = (DIFF_W + SWA_Q_W) // REST_TN
REST_KB_GROUPS = SWA_KV_W // LANES
VA_BLK = 0
QB_BLK = DIFF_W // LANES
KB_BLK = QB_BLK + SWA_Q_W // LANES
VB_BLK = KB_BLK + SWA_KV_W // LANES


def _in_proj(h, w_in, layer, tables):
    ca, sa, cb, sb = tables
    qk_a = _head_proj(
        h, w_in, layer, 0, 2 * DIFF_W, 2048, DIFF_QK_TN, (ca, sa), DIFF_QK_DIM // 2,
        lambda n, j: True,
        lambda n: jnp.where(n == 0, DIFF_QK_DIM ** -0.5 * LOG2E, 1.0), "in_proj_qk_diff")

    def is_qb(n):
        return jnp.logical_and(n >= REST_QB_TILE0, n < REST_KV_TILE)

    def rest_is_rope(n, j):
        return n >= REST_QB_TILE0 if j < REST_KB_GROUPS else is_qb(n)

    rest = _head_proj(
        h, w_in, layer, 2 * DIFF_W, IN_WIDTH - 2 * DIFF_W, 2048, REST_TN, (cb, sb), HEAD_DIM // 2,
        rest_is_rope,
        lambda n: jnp.where(is_qb(n), HEAD_DIM ** -0.5 * LOG2E, 1.0), "in_proj_rest")
    return qk_a, rest


def _gated_residual_matmul(a_list, a_maps, w, layer, x, mod, gate_idx, tm, tn, m_outer, name,
                           next_norm=None):
    last = (w.shape[1] // MM_TK if len(a_list) == 1 else 1) - 1

    def xo_map(n, k, r, r0):
        return (jnp.where(k == last, r, r0), n)

    def mod_row(idx):
        return ((None, SUBLANES, tn), lambda n, k, r, r0: (idx, 0, n))

    args = [x, mod]
    blocks = [((tm, tn), xo_map), mod_row(gate_idx)]
    out_blocks = [((tm, tn), xo_map)]
    out_shapes = [jax.ShapeDtypeStruct((TOKENS, D_MODEL), F32)]
    epilogue = _gated_residual_epilogue
    if next_norm is not None:
        assert last == 0
        gain, scale_idx = next_norm
        args += [gain, mod]
        blocks += [((1, tn), lambda n, k, r, r0: (0, n)), mod_row(scale_idx)]
        out_blocks += [((tm, tn), xo_map), ((None, tm, LANES), lambda n, k, r, r0: (n, r, 0))]
        out_shapes += [jax.ShapeDtypeStruct((TOKENS, D_MODEL), BF16),
                       jax.ShapeDtypeStruct((D_MODEL // tn, TOKENS, LANES), F32)]
        epilogue = _gated_residual_prenorm_epilogue
    return _matmul(a_list, a_maps, w, layer, 0, D_MODEL, tm, tn, m_outer, epilogue, args, blocks,
                   out_blocks, out_shapes, name)


UP_TILE = (2048, 1024)
OUT_TILE = (1024, 1024)
DOWN_TILE = (2048, 512)
DOWN_M_OUTER = 2
K_BLOCKS = D_FF // MM_TK


def _out_proj(out_a, out_b, w_out, layer, x, mod, gate_idx, mlp_gain, mlp_scale_idx):
    tm, tn = OUT_TILE
    a_maps = [((tm, a.shape[1]), lambda n, k, r, r0: (r, 0)) for a in (out_a, out_b)]
    return _gated_residual_matmul([out_a, out_b], a_maps, w_out, layer, x, mod, gate_idx, tm, tn, 1,
                                  "out_proj", next_norm=(mlp_gain, mlp_scale_idx))


def _up_proj(xs, ss, w_up, layer, mod, shift_idx):
    tm, tn = UP_TILE
    per_block = MM_TK // tn
    blocks = [((ss.shape[0], tm, LANES), lambda n, k, r, r0: (0, r, 0)),
              ((None, SUBLANES, D_MODEL), lambda n, k, r, r0: (shift_idx, 0, 0))]
    return _matmul(
        [xs], [((tm, D_MODEL), lambda n, k, r, r0: (r, 0))], w_up, layer, 0, D_FF, tm, tn, 1,
        _postnorm_relu2_epilogue, [ss, mod], blocks,
        [((None, tm, tn), lambda n, k, r, r0: (n // per_block, r, n % per_block))],
        [jax.ShapeDtypeStruct((K_BLOCKS, TOKENS, MM_TK), BF16)], "up_proj", shift_idx=1)[0]


def _down_proj(hidden, w_down, layer, x, mod, gate_idx):
    tm, tn = DOWN_TILE
    a_maps = [((None, tm, MM_TK), lambda n, k, r, r0: (k, r, 0))]
    return _gated_residual_matmul([hidden], a_maps, w_down, layer, x, mod, gate_idx, tm, tn,
                                  DOWN_M_OUTER, "down_proj")[0]


DIFF_CHUNK = 512
ATTN_LOOKAHEAD = 3
_NT = (((1,), (1,)), ((), ()))
_TN = (((0,), (0,)), ((), ()))


def _diff_attn_kernel(q_ref, k_ref, v_ref, lam_ref, g_ref, o_ref, *, lam_init):
    k = k_ref[...]
    v = v_ref[...]
    lane = lax.broadcasted_iota(jnp.int32, (1, LANES), 1)
    lp = lam_ref[...]
    lam = (jnp.exp(jnp.sum(lp[0:1] * lp[1:2], axis=-1, keepdims=True))
           - jnp.exp(jnp.sum(lp[2:3] * lp[3:4], axis=-1, keepdims=True)) + lam_init)
    n_chunks = SEQ // DIFF_CHUNK
    results = {}

    def logits(t):
        chunk, second = divmod(t, 2)
        q = q_ref[chunk * DIFF_CHUNK:(chunk + 1) * DIFF_CHUNK, :]
        keep = (lane >= DIFF_QK_DIM) if second else (lane < DIFF_QK_DIM)
        qm = jnp.where(keep, q, jnp.zeros_like(q))
        return lax.dot_general(k, qm, _NT, preferred_element_type=F32)

    def softmax_pv(t, st):
        e = jnp.exp2(st - jnp.max(st, axis=0, keepdims=True))
        denom = jnp.sum(e, axis=0, keepdims=True)
        ot = lax.dot_general(v, e.astype(BF16), _TN, preferred_element_type=F32)
        results[t] = (ot, denom)

    _software_pipeline(2 * n_chunks, logits, softmax_pv, ATTN_LOOKAHEAD)
    for chunk in range(n_chunks):
        (ot0, l0), (ot1, l1) = results[2 * chunk], results[2 * chunk + 1]
        o = (ot0 * (1.0 / l0) - ot1 * (lam / l1)).T
        o = o * lax.rsqrt(jnp.mean(o * o, axis=-1, keepdims=True) + EPS)
        o_ref[chunk * DIFF_CHUNK:(chunk + 1) * DIFF_CHUNK, :] = (
            o * g_ref[...] * (1.0 - lam_init)).astype(o_ref.dtype)


def _diff_attention(qk_a, rest, diff_lambda, subln, layer):
    lam_init = 0.8 - 0.6 * math.exp(-0.3 * layer)
    return pl.pallas_call(
        functools.partial(_diff_attn_kernel, lam_init=lam_init),
        grid=(BATCH, N_HEADS_DIFF),
        in_specs=[
            pl.BlockSpec((None, SEQ, LANES), lambda b, h: (h, b, 0)),
            pl.BlockSpec((None, SEQ, LANES), lambda b, h: (N_HEADS_DIFF + h, b, 0)),
            pl.BlockSpec((None, SEQ, LANES), lambda b, h: (VA_BLK + h, b, 0)),
            pl.BlockSpec((None, 4, DIFF_QK_DIM), lambda b, h: (layer, 0, 0)),
            pl.BlockSpec((None, 1, HEAD_DIM), lambda b, h: (layer, 0, 0)),
        ],
        out_specs=pl.BlockSpec((SEQ, LANES), lambda b, h: (b, h)),
        out_shape=jax.ShapeDtypeStruct((TOKENS, DIFF_W), BF16),
        compiler_params=_params("arbitrary", "arbitrary"),
        name="diff_attn",
    )(qk_a, qk_a, rest, diff_lambda, subln.reshape(DEPTH, 1, HEAD_DIM))


SWA_BAND = 3 * BLOCK
SWA_QB = 16


def _swa_attn_kernel(q_ref, k_ref, v_ref, sink_ref, o_ref):
    step = pl.program_id(2)
    sink = sink_ref[...] * LOG2E

    def band_start(j):
        n = step * SWA_QB + j
        return n, pl.multiple_of(jnp.clip(n - 1, 0, SEQ // BLOCK - 3) * BLOCK, BLOCK)

    def logits(j):
        n, start = band_start(j)
        q = q_ref[:, j * BLOCK:(j + 1) * BLOCK, :].reshape(GQA_RATIO * BLOCK, HEAD_DIM)
        kb = k_ref[pl.ds(start, SWA_BAND), :]
        return lax.dot_general(kb, q, _NT, preferred_element_type=F32)

    def softmax_pv(j, st):
        n, start = band_start(j)
        kpos = start + lax.broadcasted_iota(jnp.int32, (SWA_BAND, BLOCK), 0)
        qpos = n * BLOCK + lax.broadcasted_iota(jnp.int32, (SWA_BAND, BLOCK), 1)
        bias = jnp.where(jnp.abs(kpos - qpos) <= WINDOW, 0.0, -jnp.inf).astype(F32)
        st = st + jnp.concatenate([bias] * GQA_RATIO, axis=1)
        mx = jnp.maximum(jnp.max(st, axis=0, keepdims=True), sink)
        e = jnp.exp2(st - mx)
        denom = jnp.sum(e, axis=0, keepdims=True) + jnp.exp2(sink - mx)
        vb = v_ref[pl.ds(start, SWA_BAND), :]
        ot = lax.dot_general(vb, e.astype(BF16), _TN, preferred_element_type=F32)
        ot = ot * (1.0 / denom)
        for r in range(GQA_RATIO):
            o_ref[j * BLOCK:(j + 1) * BLOCK, r * HEAD_DIM:(r + 1) * HEAD_DIM] = (
                ot[:, r * BLOCK:(r + 1) * BLOCK].T.astype(o_ref.dtype))

    _software_pipeline(SWA_QB, logits, softmax_pv, ATTN_LOOKAHEAD)


def _swa_attention(rest, sink):
    steps = SEQ // BLOCK // SWA_QB
    rows = SWA_QB * BLOCK
    sink_rows = jnp.repeat(sink, BLOCK).reshape(N_KV_SWA, 1, GQA_RATIO * BLOCK)
    return pl.pallas_call(
        _swa_attn_kernel,
        grid=(BATCH, N_KV_SWA, steps),
        in_specs=[
            pl.BlockSpec((GQA_RATIO, rows, LANES),
                         lambda b, g, n: (QB_BLK // GQA_RATIO + g, b * steps + n, 0)),
            pl.BlockSpec((None, SEQ, LANES), lambda b, g, n: (KB_BLK + g, b, 0)),
            pl.BlockSpec((None, SEQ, LANES), lambda b, g, n: (VB_BLK + g, b, 0)),
            pl.BlockSpec((None, 1, GQA_RATIO * BLOCK), lambda b, g, n: (g, 0, 0)),
        ],
        out_specs=pl.BlockSpec((rows, GQA_RATIO * HEAD_DIM), lambda b, g, n: (b * steps + n, g)),
        out_shape=jax.ShapeDtypeStruct((TOKENS, SWA_Q_W), BF16),
        compiler_params=_params("arbitrary", "arbitrary", "arbitrary"),
        name="swa_attn",
    )(rest, rest, rest, sink_rows)


def kernel(x, c, positions, ada_w, ada_b, norm_mix, w_in, diff_lambda, diff_subln, swa_sink,
           w_out, norm_mlp, w_up, w_down, final_norm):
    mod = _modulation(c, ada_w, ada_b)
    tables = _rope_tables(positions)
    xr = x.reshape(TOKENS, D_MODEL)
    for layer in range(DEPTH):
        base = layer * N_MOD
        h = _norm(xr, norm_mix[layer], mod, base + 1, base + 0)
        qk_a, rest = _in_proj(h, w_in, layer, tables)
        out_a = _diff_attention(qk_a, rest, diff_lambda, diff_subln, layer)
        out_b = _swa_attention(rest, swa_sink[layer])
        xr, xs, ss = _out_proj(out_a, out_b, w_out, layer, xr, mod, base + 2,
                               norm_mlp[layer].reshape(1, D_MODEL), base + 4)
        hidden = _up_proj(xs, ss, w_up, layer, mod, base + 3)
        xr = _down_proj(hidden, w_down, layer, xr, mod, base + 5)
    out = _norm(xr, final_norm, out_dtype=F32)
    return out.reshape(BATCH, SEQ, D_MODEL)
```

```python
import functools
import math

import jax
import jax.numpy as jnp
from jax import lax
from jax.experimental import pallas as pl
from jax.experimental.pallas import tpu as pltpu

D_MODEL = 2048
BATCH = 4
SEQ = 2048
DEPTH = 2
HEAD_DIM = 128
N_HEADS_DIFF = 8
DIFF_QK_DIM = 64
N_HEADS_SWA = 8
N_KV_SWA = 2
GQA_RATIO = N_HEADS_SWA // N_KV_SWA
WINDOW = 128
BLOCK = 128
D_FF = 4 * D_MODEL
ROPE_THETA = 10000.0
EPS = 1e-6
N_MOD = 6
DIFF_W = N_HEADS_DIFF * HEAD_DIM
SWA_Q_W = N_HEADS_SWA * HEAD_DIM
SWA_KV_W = N_KV_SWA * HEAD_DIM
IN_WIDTH = 3 * DIFF_W + SWA_Q_W + 2 * SWA_KV_W
TOKENS = BATCH * SEQ

LANES = 128
SUBLANES = 8
VMEM_LIMIT = 60 * 1024 * 1024

F32 = jnp.float32
BF16 = jnp.bfloat16
LOG2E = math.log2(math.e)


def _params(*sem):
    return pltpu.CompilerParams(dimension_semantics=sem, vmem_limit_bytes=VMEM_LIMIT)


def _software_pipeline(n_items, produce, consume, lookahead):
    pending = {}
    for t in range(n_items + lookahead):
        if t < n_items:
            pending[t] = produce(t)
        if t >= lookahead:
            consume(t - lookahead, pending.pop(t - lookahead))


MOD_TN = 1024


def _mod_kernel(c_ref, w_ref, b_ref, o_ref):
    c = c_ref[...]
    c_act = (c * jax.nn.sigmoid(c)).astype(BF16)
    o_ref[...] = jnp.dot(c_act, w_ref[...].astype(BF16), preferred_element_type=F32) + b_ref[...]


def _modulation(c, ada_w, ada_b):
    c_pad = jnp.pad(c, ((0, SUBLANES - BATCH), (0, 0)))
    per_chunk = D_MODEL // MOD_TN
    return pl.pallas_call(
        _mod_kernel,
        grid=(DEPTH, N_MOD * per_chunk),
        in_specs=[
            pl.BlockSpec((SUBLANES, D_MODEL), lambda l, j: (0, 0)),
            pl.BlockSpec((None, D_MODEL, MOD_TN), lambda l, j: (l, 0, j)),
            pl.BlockSpec((None, 1, MOD_TN), lambda l, j: (l, 0, j)),
        ],
        out_specs=pl.BlockSpec((None, SUBLANES, MOD_TN),
                               lambda l, j: (l * N_MOD + j // per_chunk, 0, j % per_chunk)),
        out_shape=jax.ShapeDtypeStruct((DEPTH * N_MOD, SUBLANES, D_MODEL), F32),
        compiler_params=_params("arbitrary", "arbitrary"),
        name="adaln_mod",
    )(c_pad, ada_w, ada_b.reshape(DEPTH, 1, N_MOD * D_MODEL))


ROPE_TM = 1024


def _rope_kernel(pos_ref, inv_a_ref, inv_b_ref, ca_ref, sa_ref, cb_ref, sb_ref):
    pos = pos_ref[...].astype(F32)
    lane = lax.broadcasted_iota(jnp.int32, (1, LANES), 1)
    ang_a = pos * inv_a_ref[...]
    ang_b = pos * inv_b_ref[...]
    sign_a = jnp.where((lane & (DIFF_QK_DIM // 2)) == 0, -1.0, 1.0).astype(F32)
    sign_b = jnp.where((lane & (HEAD_DIM // 2)) == 0, -1.0, 1.0).astype(F32)
    ca_ref[...] = jnp.cos(ang_a)
    sa_ref[...] = jnp.sin(ang_a) * sign_a
    cb_ref[...] = jnp.cos(ang_b)
    sb_ref[...] = jnp.sin(ang_b) * sign_b


def _rope_tables(positions):
    def inv_freq(dim):
        return ROPE_THETA ** (-jnp.arange(0, dim, 2, dtype=F32) / dim)
    inv_a = jnp.tile(inv_freq(DIFF_QK_DIM), LANES // (DIFF_QK_DIM // 2)).reshape(1, LANES)
    inv_b = jnp.tile(inv_freq(HEAD_DIM), LANES // (HEAD_DIM // 2)).reshape(1, LANES)
    tab = jax.ShapeDtypeStruct((TOKENS, LANES), F32)
    row = pl.BlockSpec((ROPE_TM, LANES), lambda m: (m, 0))
    const = pl.BlockSpec((1, LANES), lambda m: (0, 0))
    return pl.pallas_call(
        _rope_kernel,
        grid=(TOKENS // ROPE_TM,),
        in_specs=[pl.BlockSpec((ROPE_TM, 1), lambda m: (m, 0)), const, const],
        out_specs=[row, row, row, row],
        out_shape=[tab, tab, tab, tab],
        compiler_params=_params("arbitrary"),
        name="rope_tables",
    )(positions.reshape(TOKENS, 1), inv_a, inv_b)


NORM_TM = 1024


def _norm_kernel(x_ref, g_ref, *rest, modulated):
    x = x_ref[...]
    y = x * lax.rsqrt(jnp.mean(x * x, axis=-1, keepdims=True) + EPS)
    y = y * g_ref[...]
    if modulated:
        sc_ref, sh_ref, o_ref = rest
        b = (pl.program_id(0) * NORM_TM) // SEQ
        y = y * (1.0 + sc_ref[pl.ds(b, 1), :]) + sh_ref[pl.ds(b, 1), :]
    else:
        (o_ref,) = rest
    o_ref[...] = y.astype(o_ref.dtype)


def _norm(x, gain, mod=None, sc_idx=None, sh_idx=None, out_dtype=BF16):
    in_specs = [pl.BlockSpec((NORM_TM, D_MODEL), lambda m: (m, 0)),
                pl.BlockSpec((1, D_MODEL), lambda m: (0, 0))]
    args = [x, gain.reshape(1, D_MODEL)]
    if mod is not None:
        in_specs += [pl.BlockSpec((None, SUBLANES, D_MODEL), lambda m: (sc_idx, 0, 0)),
                     pl.BlockSpec((None, SUBLANES, D_MODEL), lambda m: (sh_idx, 0, 0))]
        args += [mod, mod]
    return pl.pallas_call(
        functools.partial(_norm_kernel, modulated=mod is not None),
        grid=(TOKENS // NORM_TM,),
        in_specs=in_specs,
        out_specs=pl.BlockSpec((NORM_TM, D_MODEL), lambda m: (m, 0)),
        out_shape=jax.ShapeDtypeStruct((TOKENS, D_MODEL), out_dtype),
        compiler_params=_params("arbitrary"),
        name="rmsnorm_mod" if mod is not None else "rmsnorm_final",
    )(*args)


MM_TK = 2048
MM_SUB = 512


def _swap_halves(x, half):
    if 2 * half == LANES:
        return pltpu.roll(x, half, 1)
    lane = lax.broadcasted_iota(jnp.int32, (1, LANES), 1)
    return jnp.where((lane & half) == 0,
                     pltpu.roll(x, LANES - half, 1), pltpu.roll(x, half, 1))


def _head_epilogue(acc, rows, row0, n, refs, outs, bias, *, half, is_rope, q_scale):
    cos_ref, sin_ref = refs
    (o_ref,) = outs
    for j in range(o_ref.shape[0]):
        x = acc[:, j * LANES:(j + 1) * LANES]
        rotate = is_rope(n, j)
        if rotate is not False:
            cos = cos_ref[rows, :]
            sin_signed = sin_ref[rows, :]
            if rotate is not True:
                cos = jnp.where(rotate, cos, 1.0)
                sin_signed = jnp.where(rotate, sin_signed, 0.0)
            x = (x * cos + _swap_halves(x, half) * sin_signed) * q_scale(n)
        o_ref[j, rows, :] = x.astype(o_ref.dtype)


def _gated_residual_epilogue(acc, rows, row0, n, refs, outs, bias):
    x_ref, g_ref = refs
    outs[0][rows, :] = x_ref[rows, :] + g_ref[pl.ds(row0 // SEQ, 1), :] * acc


def _gated_residual_prenorm_epilogue(acc, rows, row0, n, refs, outs, bias):
    x_ref, g_ref, gain_ref, sc_ref = refs
    x_out, xs_out, ss_out = outs
    b = row0 // SEQ
    xn = x_ref[rows, :] + g_ref[pl.ds(b, 1), :] * acc
    x_out[rows, :] = xn
    xs_out[rows, :] = (xn * (gain_ref[...] * (1.0 + sc_ref[pl.ds(b, 1), :]))).astype(xs_out.dtype)
    ss_out[rows, :] = jnp.broadcast_to(jnp.sum(xn * xn, axis=-1, keepdims=True),
                                       (rows.stop - rows.start, LANES))


def _relu2_epilogue(acc, rows, row0, n, refs, outs, bias):
    r = jnp.maximum(acc, 0.0)
    outs[0][rows, :] = (r * r).astype(outs[0].dtype)


def _postnorm_relu2_epilogue(acc, rows, row0, n, refs, outs, bias):
    ss_ref = refs[0]
    ss = ss_ref[0, rows, :]
    for i in range(1, ss_ref.shape[0]):
        ss = ss + ss_ref[i, rows, :]
    inv = lax.rsqrt(ss * (1.0 / D_MODEL) + EPS)
    inv = jnp.concatenate([inv] * (acc.shape[1] // LANES), axis=1)
    r = jnp.maximum(acc * inv + bias[pl.ds(row0 // SEQ, 1), :], 0.0)
    outs[0][rows, :] = (r * r).astype(outs[0].dtype)


def _matmul_kernel(*refs, n_a, n_extra, n_out, nk, tm, m_inner, epilogue, shift_idx):
    a_refs = refs[:n_a]
    w_ref = refs[n_a]
    extra = refs[n_a + 1:n_a + 1 + n_extra]
    outs = refs[n_a + 1 + n_extra:n_a + 1 + n_extra + n_out]
    scratch = refs[n_a + 1 + n_extra + n_out:]
    wb_ref = scratch[0]
    bias_ref = scratch[-1] if shift_idx is not None else None
    outer, n, k, m = (pl.program_id(i) for i in range(4))

    if nk > 1:
        acc_ref = scratch[1]

        first_nk = jnp.logical_and(n == 0, k == 0)
        @pl.when(jnp.logical_and(jnp.logical_and(outer == 0, m == 0), first_nk))
        def _():
            acc_ref[...] = jnp.zeros_like(acc_ref)

    @pl.when(m == 0)
    def _():
        wb_ref[...] = w_ref[...].astype(BF16)
        if shift_idx is not None:
            bias_ref[...] = jnp.dot(extra[shift_idx][...].astype(BF16), wb_ref[...],
                                    preferred_element_type=F32)

    def product(t):
        rows = slice(t * MM_SUB, (t + 1) * MM_SUB)
        part = None
        off = 0
        for a_ref in a_refs:
            ka = a_ref.shape[1]
            d = jnp.dot(a_ref[rows, :], wb_ref[off:off + ka, :], preferred_element_type=F32)
            part = d if part is None else part + d
            off += ka
        return part

    def finish(t, part):
        rows = slice(t * MM_SUB, (t + 1) * MM_SUB)
        local_row0 = m * tm + t * MM_SUB
        if nk > 1:
            arows = pl.ds(pl.multiple_of(local_row0, MM_SUB), MM_SUB)
            part = jnp.where(k == 0, 0.0, acc_ref[arows, :]) + part
            acc_ref[arows, :] = part
        epilogue(part, rows, outer * (m_inner * tm) + local_row0, n, extra, outs, bias_ref)

    _software_pipeline(tm // MM_SUB, product, finish, lookahead=1)


def _block(shape, index_map, m_inner):
    return pl.BlockSpec(
        shape, lambda o, n, k, m: index_map(n, k, o * m_inner + m, o * m_inner))


def _matmul(a_list, a_maps, w, layer, col0, n_cols, tm, tn, m_outer, epilogue, extra_args,
            extra_blocks, out_blocks, out_shapes, name, shift_idx=None):
    k_total = w.shape[1]
    assert col0 % tn == 0 and n_cols % tn == 0 and tm % MM_SUB == 0
    assert TOKENS % (tm * m_outer) == 0
    n_off = col0 // tn
    nk = k_total // MM_TK if len(a_list) == 1 else 1
    tk = k_total // nk
    m_inner = TOKENS // tm // m_outer
    grid = (m_outer, n_cols // tn, nk, m_inner)
    a_specs = [_block(shape, imap, m_inner) for shape, imap in a_maps]
    w_spec = pl.BlockSpec((None, tk, tn), lambda o, n, k, m: (layer, k, n + n_off))
    scratch = [pltpu.VMEM((tk, tn), BF16)]
    if nk > 1:
        scratch.append(pltpu.VMEM((m_inner * tm, tn), F32))
    if shift_idx is not None:
        assert nk == 1
        scratch.append(pltpu.VMEM((SUBLANES, tn), F32))
    return pl.pallas_call(
        functools.partial(_matmul_kernel, n_a=len(a_list), n_extra=len(extra_args),
                          n_out=len(out_blocks), nk=nk, tm=tm, m_inner=m_inner, epilogue=epilogue,
                          shift_idx=shift_idx),
        grid=grid,
        in_specs=a_specs + [w_spec] + [_block(s, f, m_inner) for s, f in extra_blocks],
        out_specs=[_block(s, f, m_inner) for s, f in out_blocks],
        out_shape=out_shapes,
        scratch_shapes=scratch,
        compiler_params=_params("arbitrary", "arbitrary", "arbitrary", "arbitrary"),
        name=name,
    )(*a_list, w, *extra_args)


def _head_proj(h, w_in, layer, col0, n_cols, tm, tn, tables, half, is_rope, q_scale, name):
    table_blocks = [((tm, LANES), lambda n, k, r, r0: (r, 0)) for _ in tables]
    epilogue = functools.partial(_head_epilogue, half=half, is_rope=is_rope, q_scale=q_scale)
    return _matmul(
        [h], [((tm, D_MODEL), lambda n, k, r, r0: (r, 0))], w_in, layer, col0, n_cols, tm, tn, 1,
        epilogue, list(tables), table_blocks,
        [((tn // LANES, tm, LANES), lambda n, k, r, r0: (n, r, 0))],
        [jax.ShapeDtypeStruct((n_cols // LANES, TOKENS, LANES), BF16)], name)[0]


DIFF_QK_TN = 1024
REST_TN = 512
REST_QB_TILE0 = DIFF_W // REST_TN
REST_KV_TILE = (DIFF_W + SWA_Q_W) // REST_TN
REST_KB_GROUPS = SWA_KV_W // LANES
VA_BLK = 0
QB_BLK = DIFF_W // LANES
KB_BLK = QB_BLK + SWA_Q_W // LANES
VB_BLK = KB_BLK + SWA_KV_W // LANES


def _in_proj(h, w_in, layer, tables):
    ca, sa, cb, sb = tables
    qk_a = _head_proj(
        h, w_in, layer, 0, 2 * DIFF_W, 2048, DIFF_QK_TN, (ca, sa), DIFF_QK_DIM // 2,
        lambda n, j: True,
        lambda n: jnp.where(n == 0, DIFF_QK_DIM ** -0.5 * LOG2E, 1.0), "in_proj_qk_diff")

    def is_qb(n):
        return jnp.logical_and(n >= REST_QB_TILE0, n < REST_KV_TILE)

    def rest_is_rope(n, j):
        return n >= REST_QB_TILE0 if j < REST_KB_GROUPS else is_qb(n)

    rest = _head_proj(
        h, w_in, layer, 2 * DIFF_W, IN_WIDTH - 2 * DIFF_W, 2048, REST_TN, (cb, sb), HEAD_DIM // 2,
        rest_is_rope,
        lambda n: jnp.where(is_qb(n), HEAD_DIM ** -0.5 * LOG2E, 1.0), "in_proj_rest")
    return qk_a, rest


def _gated_residual_matmul(a_list, a_maps, w, layer, x, mod, gate_idx, tm, tn, m_outer, name,
                           next_norm=None):
    last = (w.shape[1] // MM_TK if len(a_list) == 1 else 1) - 1

    def xo_map(n, k, r, r0):
        return (jnp.where(k == last, r, r0), n)

    def mod_row(idx):
        return ((None, SUBLANES, tn), lambda n, k, r, r0: (idx, 0, n))

    args = [x, mod]
    blocks = [((tm, tn), xo_map), mod_row(gate_idx)]
    out_blocks = [((tm, tn), xo_map)]
    out_shapes = [jax.ShapeDtypeStruct((TOKENS, D_MODEL), F32)]
    epilogue = _gated_residual_epilogue
    if next_norm is not None:
        assert last == 0
        gain, scale_idx = next_norm
        args += [gain, mod]
        blocks += [((1, tn), lambda n, k, r, r0: (0, n)), mod_row(scale_idx)]
        out_blocks += [((tm, tn), xo_map), ((None, tm, LANES), lambda n, k, r, r0: (n, r, 0))]
        out_shapes += [jax.ShapeDtypeStruct((TOKENS, D_MODEL), BF16),
                       jax.ShapeDtypeStruct((D_MODEL // tn, TOKENS, LANES), F32)]
        epilogue = _gated_residual_prenorm_epilogue
    return _matmul(a_list, a_maps, w, layer, 0, D_MODEL, tm, tn, m_outer, epilogue, args, blocks,
                   out_blocks, out_shapes, name)


UP_TILE = (2048, 1024)
OUT_TILE = (1024, 1024)
DOWN_TILE = (1024, 1024)
DOWN_M_OUTER = 4
K_BLOCKS = D_FF // MM_TK


def _out_proj(out_a, out_b, w_out, layer, x, mod, gate_idx, mlp_gain, mlp_scale_idx):
    tm, tn = OUT_TILE
    a_maps = [((tm, a.shape[1]), lambda n, k, r, r0: (r, 0)) for a in (out_a, out_b)]
    return _gated_residual_matmul([out_a, out_b], a_maps, w_out, layer, x, mod, gate_idx, tm, tn, 1,
                                  "out_proj", next_norm=(mlp_gain, mlp_scale_idx))


def _up_proj(xs, ss, w_up, layer, mod, shift_idx):
    tm, tn = UP_TILE
    per_block = MM_TK // tn
    blocks = [((ss.shape[0], tm, LANES), lambda n, k, r, r0: (0, r, 0)),
              ((None, SUBLANES, D_MODEL), lambda n, k, r, r0: (shift_idx, 0, 0))]
    return _matmul(
        [xs], [((tm, D_MODEL), lambda n, k, r, r0: (r, 0))], w_up, layer, 0, D_FF, tm, tn, 1,
        _postnorm_relu2_epilogue, [ss, mod], blocks,
        [((None, tm, tn), lambda n, k, r, r0: (n // per_block, r, n % per_block))],
        [jax.ShapeDtypeStruct((K_BLOCKS, TOKENS, MM_TK), BF16)], "up_proj", shift_idx=1)[0]


def _down_proj(hidden, w_down, layer, x, mod, gate_idx):
    tm, tn = DOWN_TILE
    a_maps = [((None, tm, MM_TK), lambda n, k, r, r0: (k, r, 0))]
    return _gated_residual_matmul([hidden], a_maps, w_down, layer, x, mod, gate_idx, tm, tn,
                                  DOWN_M_OUTER, "down_proj")[0]


DIFF_CHUNK = 512
ATTN_LOOKAHEAD = 3
_NT = (((1,), (1,)), ((), ()))
_TN = (((0,), (0,)), ((), ()))


def _diff_attn_kernel(q_ref, k_ref, v_ref, lam_ref, g_ref, o_ref, *, lam_init):
    k = k_ref[...]
    v = v_ref[...]
    lane = lax.broadcasted_iota(jnp.int32, (1, LANES), 1)
    lp = lam_ref[...]
    lam = (jnp.exp(jnp.sum(lp[0:1] * lp[1:2], axis=-1, keepdims=True))
           - jnp.exp(jnp.sum(lp[2:3] * lp[3:4], axis=-1, keepdims=True)) + lam_init)
    n_chunks = SEQ // DIFF_CHUNK
    results = {}

    def logits(t):
        chunk, second = divmod(t, 2)
        q = q_ref[chunk * DIFF_CHUNK:(chunk + 1) * DIFF_CHUNK, :]
        keep = (lane >= DIFF_QK_DIM) if second else (lane < DIFF_QK_DIM)
        qm = jnp.where(keep, q, jnp.zeros_like(q))
        return lax.dot_general(k, qm, _NT, preferred_element_type=F32)

    def softmax_pv(t, st):
        e = jnp.exp2(st - jnp.max(st, axis=0, keepdims=True))
        denom = jnp.sum(e, axis=0, keepdims=True)
        ot = lax.dot_general(v, e.astype(BF16), _TN, preferred_element_type=F32)
        results[t] = (ot, denom)

    _software_pipeline(2 * n_chunks, logits, softmax_pv, ATTN_LOOKAHEAD)
    for chunk in range(n_chunks):
        (ot0, l0), (ot1, l1) = results[2 * chunk], results[2 * chunk + 1]
        o = (ot0 * (1.0 / l0) - ot1 * (lam / l1)).T
        o = o * lax.rsqrt(jnp.mean(o * o, axis=-1, keepdims=True) + EPS)
        o_ref[chunk * DIFF_CHUNK:(chunk + 1) * DIFF_CHUNK, :] = (
            o * g_ref[...] * (1.0 - lam_init)).astype(o_ref.dtype)


def _diff_attention(qk_a, rest, diff_lambda, subln, layer):
    lam_init = 0.8 - 0.6 * math.exp(-0.3 * layer)
    return pl.pallas_call(
        functools.partial(_diff_attn_kernel, lam_init=lam_init),
        grid=(BATCH, N_HEADS_DIFF),
        in_specs=[
            pl.BlockSpec((None, SEQ, LANES), lambda b, h: (h, b, 0)),
            pl.BlockSpec((None, SEQ, LANES), lambda b, h: (N_HEADS_DIFF + h, b, 0)),
            pl.BlockSpec((None, SEQ, LANES), lambda b, h: (VA_BLK + h, b, 0)),
            pl.BlockSpec((None, 4, DIFF_QK_DIM), lambda b, h: (layer, 0, 0)),
            pl.BlockSpec((None, 1, HEAD_DIM), lambda b, h: (layer, 0, 0)),
        ],
        out_specs=pl.BlockSpec((SEQ, LANES), lambda b, h: (b, h)),
        out_shape=jax.ShapeDtypeStruct((TOKENS, DIFF_W), BF16),
        compiler_params=_params("arbitrary", "arbitrary"),
        name="diff_attn",
    )(qk_a, qk_a, rest, diff_lambda, subln.reshape(DEPTH, 1, HEAD_DIM))


SWA_BAND = 3 * BLOCK
SWA_QB = 16


def _swa_attn_kernel(q_ref, k_ref, v_ref, sink_ref, o_ref):
    step = pl.program_id(2)
    sink = sink_ref[...] * LOG2E

    def band_start(j):
        n = step * SWA_QB + j
        return n, pl.multiple_of(jnp.clip(n - 1, 0, SEQ // BLOCK - 3) * BLOCK, BLOCK)

    def logits(j):
        n, start = band_start(j)
        q = q_ref[:, j * BLOCK:(j + 1) * BLOCK, :].reshape(GQA_RATIO * BLOCK, HEAD_DIM)
        kb = k_ref[pl.ds(start, SWA_BAND), :]
        return lax.dot_general(kb, q, _NT, preferred_element_type=F32)

    def softmax_pv(j, st):
        n, start = band_start(j)
        kpos = start + lax.broadcasted_iota(jnp.int32, (SWA_BAND, BLOCK), 0)
        qpos = n * BLOCK + lax.broadcasted_iota(jnp.int32, (SWA_BAND, BLOCK), 1)
        bias = jnp.where(jnp.abs(kpos - qpos) <= WINDOW, 0.0, -jnp.inf).astype(F32)
        st = st + jnp.concatenate([bias] * GQA_RATIO, axis=1)
        mx = jnp.maximum(jnp.max(st, axis=0, keepdims=True), sink)
        e = jnp.exp2(st - mx)
        denom = jnp.sum(e, axis=0, keepdims=True) + jnp.exp2(sink - mx)
        vb = v_ref[pl.ds(start, SWA_BAND), :]
        ot = lax.dot_general(vb, e.astype(BF16), _TN, preferred_element_type=F32)
        ot = ot * (1.0 / denom)
        for r in range(GQA_RATIO):
            o_ref[j * BLOCK:(j + 1) * BLOCK, r * HEAD_DIM:(r + 1) * HEAD_DIM] = (
                ot[:, r * BLOCK:(r + 1) * BLOCK].T.astype(o_ref.dtype))

    _software_pipeline(SWA_QB, logits, softmax_pv, ATTN_LOOKAHEAD)


def _swa_attention(rest, sink):
    steps = SEQ // BLOCK // SWA_QB
    rows = SWA_QB * BLOCK
    sink_rows = jnp.repeat(sink, BLOCK).reshape(N_KV_SWA, 1, GQA_RATIO * BLOCK)
    return pl.pallas_call(
        _swa_attn_kernel,
        grid=(BATCH, N_KV_SWA, steps),
        in_specs=[
            pl.BlockSpec((GQA_RATIO, rows, LANES),
                         lambda b, g, n: (QB_BLK // GQA_RATIO + g, b * steps + n, 0)),
            pl.BlockSpec((None, SEQ, LANES), lambda b, g, n: (KB_BLK + g, b, 0)),
            pl.BlockSpec((None, SEQ, LANES), lambda b, g, n: (VB_BLK + g, b, 0)),
            pl.BlockSpec((None, 1, GQA_RATIO * BLOCK), lambda b, g, n: (g, 0, 0)),
        ],
        out_specs=pl.BlockSpec((rows, GQA_RATIO * HEAD_DIM), lambda b, g, n: (b * steps + n, g)),
        out_shape=jax.ShapeDtypeStruct((TOKENS, SWA_Q_W), BF16),
        compiler_params=_params("arbitrary", "arbitrary", "arbitrary"),
        name="swa_attn",
    )(rest, rest, rest, sink_rows)


def kernel(x, c, positions, ada_w, ada_b, norm_mix, w_in, diff_lambda, diff_subln, swa_sink,
           w_out, norm_mlp, w_up, w_down, final_norm):
    mod = _modulation(c, ada_w, ada_b)
    tables = _rope_tables(positions)
    xr = x.reshape(TOKENS, D_MODEL)
    for layer in range(DEPTH):
        base = layer * N_MOD
        h = _norm(xr, norm_mix[layer], mod, base + 1, base + 0)
        qk_a, rest = _in_proj(h, w_in, layer, tables)
        out_a = _diff_attention(qk_a, rest, diff_lambda, diff_subln, layer)
        out_b = _swa_attention(rest, swa_sink[layer])
        xr, xs, ss = _out_proj(out_a, out_b, w_out, layer, xr, mod, base + 2,
                               norm_mlp[layer].reshape(1, D_MODEL), base + 4)
        hidden = _up_proj(xs, ss, w_up, layer, mod, base + 3)
        xr = _down_proj(hidden, w_down, layer, xr, mod, base + 5)
    out = _norm(xr, final_norm, out_dtype=F32)
    return out.reshape(BATCH, SEQ, D_MODEL)
```

```python
import functools
import math

import jax
import jax.numpy as jnp
from jax import lax
from jax.experimental import pallas as pl
from jax.experimental.pallas import tpu as pltpu

D_MODEL = 2048
BATCH = 4
SEQ = 2048
DEPTH = 2
HEAD_DIM = 128
N_HEADS_DIFF = 8
DIFF_QK_DIM = 64
N_HEADS_SWA = 8
N_KV_SWA = 2
GQA_RATIO = N_HEADS_SWA // N_KV_SWA
WINDOW = 128
BLOCK = 128
D_FF = 4 * D_MODEL
ROPE_THETA = 10000.0
EPS = 1e-6
N_MOD = 6
DIFF_W = N_HEADS_DIFF * HEAD_DIM
SWA_Q_W = N_HEADS_SWA * HEAD_DIM
SWA_KV_W = N_KV_SWA * HEAD_DIM
IN_WIDTH = 3 * DIFF_W + SWA_Q_W + 2 * SWA_KV_W
TOKENS = BATCH * SEQ

LANES = 128
SUBLANES = 8
VMEM_LIMIT = 60 * 1024 * 1024

F32 = jnp.float32
BF16 = jnp.bfloat16
LOG2E = math.log2(math.e)


def _params(*sem):
    return pltpu.CompilerParams(dimension_semantics=sem, vmem_limit_bytes=VMEM_LIMIT)


def _software_pipeline(items, lookahead):
    pending = {}
    for t in range(len(items) + lookahead):
        if t < len(items):
            pending[t] = items[t][0]()
        if t >= lookahead:
            items[t - lookahead][1](pending.pop(t - lookahead))


MOD_TN = 1024


def _mod_kernel(c_ref, w_ref, b_ref, o_ref):
    c = c_ref[...]
    c_act = (c * jax.nn.sigmoid(c)).astype(BF16)
    o_ref[...] = jnp.dot(c_act, w_ref[...].astype(BF16), preferred_element_type=F32) + b_ref[...]


def _modulation(c, ada_w, ada_b):
    c_pad = jnp.pad(c, ((0, SUBLANES - BATCH), (0, 0)))
    per_chunk = D_MODEL // MOD_TN
    return pl.pallas_call(
        _mod_kernel,
        grid=(DEPTH, N_MOD * per_chunk),
        in_specs=[
            pl.BlockSpec((SUBLANES, D_MODEL), lambda l, j: (0, 0)),
            pl.BlockSpec((None, D_MODEL, MOD_TN), lambda l, j: (l, 0, j)),
            pl.BlockSpec((None, 1, MOD_TN), lambda l, j: (l, 0, j)),
        ],
        out_specs=pl.BlockSpec((None, SUBLANES, MOD_TN),
                               lambda l, j: (l * N_MOD + j // per_chunk, 0, j % per_chunk)),
        out_shape=jax.ShapeDtypeStruct((DEPTH * N_MOD, SUBLANES, D_MODEL), F32),
        compiler_params=_params("arbitrary", "arbitrary"),
        name="adaln_mod",
    )(c_pad, ada_w, ada_b.reshape(DEPTH, 1, N_MOD * D_MODEL))


ROPE_TM = 1024


def _rope_kernel(pos_ref, inv_a_ref, inv_b_ref, ca_ref, sa_ref, cb_ref, sb_ref):
    pos = pos_ref[...].astype(F32)
    lane = lax.broadcasted_iota(jnp.int32, (1, LANES), 1)
    ang_a = pos * inv_a_ref[...]
    ang_b = pos * inv_b_ref[...]
    sign_a = jnp.where((lane & (DIFF_QK_DIM // 2)) == 0, -1.0, 1.0).astype(F32)
    sign_b = jnp.where((lane & (HEAD_DIM // 2)) == 0, -1.0, 1.0).astype(F32)
    ca_ref[...] = jnp.cos(ang_a)
    sa_ref[...] = jnp.sin(ang_a) * sign_a
    cb_ref[...] = jnp.cos(ang_b)
    sb_ref[...] = jnp.sin(ang_b) * sign_b


def _rope_tables(positions):
    def inv_freq(dim):
        return ROPE_THETA ** (-jnp.arange(0, dim, 2, dtype=F32) / dim)
    inv_a = jnp.tile(inv_freq(DIFF_QK_DIM), LANES // (DIFF_QK_DIM // 2)).reshape(1, LANES)
    inv_b = jnp.tile(inv_freq(HEAD_DIM), LANES // (HEAD_DIM // 2)).reshape(1, LANES)
    tab = jax.ShapeDtypeStruct((TOKENS, LANES), F32)
    row = pl.BlockSpec((ROPE_TM, LANES), lambda m: (m, 0))
    const = pl.BlockSpec((1, LANES), lambda m: (0, 0))
    return pl.pallas_call(
        _rope_kernel,
        grid=(TOKENS // ROPE_TM,),
        in_specs=[pl.BlockSpec((ROPE_TM, 1), lambda m: (m, 0)), const, const],
        out_specs=[row, row, row, row],
        out_shape=[tab, tab, tab, tab],
        compiler_params=_params("arbitrary"),
        name="rope_tables",
    )(positions.reshape(TOKENS, 1), inv_a, inv_b)


NORM_TM = 1024


def _norm_kernel(x_ref, g_ref, *rest, modulated):
    x = x_ref[...]
    y = x * lax.rsqrt(jnp.mean(x * x, axis=-1, keepdims=True) + EPS)
    y = y * g_ref[...]
    if modulated:
        sc_ref, sh_ref, o_ref = rest
        b = (pl.program_id(0) * NORM_TM) // SEQ
        y = y * (1.0 + sc_ref[pl.ds(b, 1), :]) + sh_ref[pl.ds(b, 1), :]
    else:
        (o_ref,) = rest
    o_ref[...] = y.astype(o_ref.dtype)


def _norm(x, gain, mod=None, sc_idx=None, sh_idx=None, out_dtype=BF16):
    in_specs = [pl.BlockSpec((NORM_TM, D_MODEL), lambda m: (m, 0)),
                pl.BlockSpec((1, D_MODEL), lambda m: (0, 0))]
    args = [x, gain.reshape(1, D_MODEL)]
    if mod is not None:
        in_specs += [pl.BlockSpec((None, SUBLANES, D_MODEL), lambda m: (sc_idx, 0, 0)),
                     pl.BlockSpec((None, SUBLANES, D_MODEL), lambda m: (sh_idx, 0, 0))]
        args += [mod, mod]
    return pl.pallas_call(
        functools.partial(_norm_kernel, modulated=mod is not None),
        grid=(TOKENS // NORM_TM,),
        in_specs=in_specs,
        out_specs=pl.BlockSpec((NORM_TM, D_MODEL), lambda m: (m, 0)),
        out_shape=jax.ShapeDtypeStruct((TOKENS, D_MODEL), out_dtype),
        compiler_params=_params("arbitrary"),
        name="rmsnorm_mod" if mod is not None else "rmsnorm_final",
    )(*args)


MM_TK = 2048
MM_SUB = 512


def _swap_halves(x, half):
    if 2 * half == LANES:
        return pltpu.roll(x, half, 1)
    lane = lax.broadcasted_iota(jnp.int32, (1, LANES), 1)
    return jnp.where((lane & half) == 0,
                     pltpu.roll(x, LANES - half, 1), pltpu.roll(x, half, 1))


def _head_epilogue(acc, rows, row0, n, refs, outs, bias, *, half, is_rope, q_scale):
    cos_ref, sin_ref = refs
    (o_ref,) = outs
    for j in range(o_ref.shape[0]):
        x = acc[:, j * LANES:(j + 1) * LANES]
        rotate = is_rope(n, j)
        if rotate is not False:
            cos = cos_ref[rows, :]
            sin_signed = sin_ref[rows, :]
            if rotate is not True:
                cos = jnp.where(rotate, cos, 1.0)
                sin_signed = jnp.where(rotate, sin_signed, 0.0)
            x = (x * cos + _swap_halves(x, half) * sin_signed) * q_scale(n)
        o_ref[j, rows, :] = x.astype(o_ref.dtype)


def _gated_residual_epilogue(acc, rows, row0, n, refs, outs, bias):
    x_ref, g_ref = refs
    outs[0][rows, :] = x_ref[rows, :] + g_ref[pl.ds(row0 // SEQ, 1), :] * acc


def _gated_residual_prenorm_epilogue(acc, rows, row0, n, refs, outs, bias):
    x_ref, g_ref, gain_ref, sc_ref = refs
    x_out, xs_out, ss_out = outs
    b = row0 // SEQ
    xn = x_ref[rows, :] + g_ref[pl.ds(b, 1), :] * acc
    x_out[rows, :] = xn
    xs_out[rows, :] = (xn * (gain_ref[...] * (1.0 + sc_ref[pl.ds(b, 1), :]))).astype(xs_out.dtype)
    ss_out[rows, :] = jnp.broadcast_to(jnp.sum(xn * xn, axis=-1, keepdims=True),
                                       (rows.stop - rows.start, LANES))


def _relu2_epilogue(acc, rows, row0, n, refs, outs, bias):
    r = jnp.maximum(acc, 0.0)
    outs[0][rows, :] = (r * r).astype(outs[0].dtype)


def _postnorm_relu2_epilogue(acc, rows, row0, n, refs, outs, bias):
    ss_ref = refs[0]
    ss = ss_ref[0, rows, :]
    for i in range(1, ss_ref.shape[0]):
        ss = ss + ss_ref[i, rows, :]
    inv = lax.rsqrt(ss * (1.0 / D_MODEL) + EPS)
    inv = jnp.concatenate([inv] * (acc.shape[1] // LANES), axis=1)
    r = jnp.maximum(acc * inv + bias[pl.ds(row0 // SEQ, 1), :], 0.0)
    outs[0][rows, :] = (r * r).astype(outs[0].dtype)


def _matmul_kernel(*refs, n_a, n_extra, n_out, nk, tm, m_inner, epilogue, shift_idx):
    a_refs = refs[:n_a]
    w_ref = refs[n_a]
    extra = refs[n_a + 1:n_a + 1 + n_extra]
    outs = refs[n_a + 1 + n_extra:n_a + 1 + n_extra + n_out]
    scratch = refs[n_a + 1 + n_extra + n_out:]
    wb_ref = scratch[0]
    bias_ref = scratch[-1] if shift_idx is not None else None
    outer, n, k, m = (pl.program_id(i) for i in range(4))

    if nk > 1:
        acc_ref = scratch[1]

        first_nk = jnp.logical_and(n == 0, k == 0)
        @pl.when(jnp.logical_and(jnp.logical_and(outer == 0, m == 0), first_nk))
        def _():
            acc_ref[...] = jnp.zeros_like(acc_ref)

    @pl.when(m == 0)
    def _():
        wb_ref[...] = w_ref[...].astype(BF16)
        if shift_idx is not None:
            bias_ref[...] = jnp.dot(extra[shift_idx][...].astype(BF16), wb_ref[...],
                                    preferred_element_type=F32)

    def product(t):
        rows = slice(t * MM_SUB, (t + 1) * MM_SUB)
        part = None
        off = 0
        for a_ref in a_refs:
            ka = a_ref.shape[1]
            d = jnp.dot(a_ref[rows, :], wb_ref[off:off + ka, :], preferred_element_type=F32)
            part = d if part is None else part + d
            off += ka
        return part

    def finish(t, part):
        rows = slice(t * MM_SUB, (t + 1) * MM_SUB)
        local_row0 = m * tm + t * MM_SUB
        if nk > 1:
            arows = pl.ds(pl.multiple_of(local_row0, MM_SUB), MM_SUB)
            part = jnp.where(k == 0, 0.0, acc_ref[arows, :]) + part
            acc_ref[arows, :] = part
        epilogue(part, rows, outer * (m_inner * tm) + local_row0, n, extra, outs, bias_ref)

    _software_pipeline([(functools.partial(product, t), functools.partial(finish, t))
                        for t in range(tm // MM_SUB)], lookahead=1)


def _block(shape, index_map, m_inner):
    return pl.BlockSpec(
        shape, lambda o, n, k, m: index_map(n, k, o * m_inner + m, o * m_inner))


def _matmul(a_list, a_maps, w, layer, col0, n_cols, tm, tn, m_outer, epilogue, extra_args,
            extra_blocks, out_blocks, out_shapes, name, shift_idx=None):
    k_total = w.shape[1]
    assert col0 % tn == 0 and n_cols % tn == 0 and tm % MM_SUB == 0
    assert TOKENS % (tm * m_outer) == 0
    n_off = col0 // tn
    nk = k_total // MM_TK if len(a_list) == 1 else 1
    tk = k_total // nk
    m_inner = TOKENS // tm // m_outer
    grid = (m_outer, n_cols // tn, nk, m_inner)
    a_specs = [_block(shape, imap, m_inner) for shape, imap in a_maps]
    w_spec = pl.BlockSpec((None, tk, tn), lambda o, n, k, m: (layer, k, n + n_off))
    scratch = [pltpu.VMEM((tk, tn), BF16)]
    if nk > 1:
        scratch.append(pltpu.VMEM((m_inner * tm, tn), F32))
    if shift_idx is not None:
        assert nk == 1
        scratch.append(pltpu.VMEM((SUBLANES, tn), F32))
    return pl.pallas_call(
        functools.partial(_matmul_kernel, n_a=len(a_list), n_extra=len(extra_args),
                          n_out=len(out_blocks), nk=nk, tm=tm, m_inner=m_inner, epilogue=epilogue,
                          shift_idx=shift_idx),
        grid=grid,
        in_specs=a_specs + [w_spec] + [_block(s, f, m_inner) for s, f in extra_blocks],
        out_specs=[_block(s, f, m_inner) for s, f in out_blocks],
        out_shape=out_shapes,
        scratch_shapes=scratch,
        compiler_params=_params("arbitrary", "arbitrary", "arbitrary", "arbitrary"),
        name=name,
    )(*a_list, w, *extra_args)


def _head_proj(h, w_in, layer, col0, n_cols, tm, tn, tables, half, is_rope, q_scale, name):
    table_blocks = [((tm, LANES), lambda n, k, r, r0: (r, 0)) for _ in tables]
    epilogue = functools.partial(_head_epilogue, half=half, is_rope=is_rope, q_scale=q_scale)
    return _matmul(
        [h], [((tm, D_MODEL), lambda n, k, r, r0: (r, 0))], w_in, layer, col0, n_cols, tm, tn, 1,
        epilogue, list(tables), table_blocks,
        [((tn // LANES, tm, LANES), lambda n, k, r, r0: (n, r, 0))],
        [jax.ShapeDtypeStruct((n_cols // LANES, TOKENS, LANES), BF16)], name)[0]


DIFF_QK_TN = 1024
REST_TN = 512
REST_QB_TILE0 = DIFF_W // REST_TN
REST_KV_TILE = (DIFF_W + SWA_Q_W) // REST_TN
REST_KB_GROUPS = SWA_KV_W // LANES
VA_BLK = 0
QB_BLK = DIFF_W // LANES
KB_BLK = QB_BLK + SWA_Q_W // LANES
VB_BLK = KB_BLK + SWA_KV_W // LANES


def _in_proj(h, w_in, layer, tables):
    ca, sa, cb, sb = tables
    qk_a = _head_proj(
        h, w_in, layer, 0, 2 * DIFF_W, 2048, DIFF_QK_TN, (ca, sa), DIFF_QK_DIM // 2,
        lambda n, j: True,
        lambda n: jnp.where(n == 0, DIFF_QK_DIM ** -0.5 * LOG2E, 1.0), "in_proj_qk_diff")

    def is_qb(n):
        return jnp.logical_and(n >= REST_QB_TILE0, n < REST_KV_TILE)

    def rest_is_rope(n, j):
        return n >= REST_QB_TILE0 if j < REST_KB_GROUPS else is_qb(n)

    rest = _head_proj(
        h, w_in, layer, 2 * DIFF_W, IN_WIDTH - 2 * DIFF_W, 2048, REST_TN, (cb, sb), HEAD_DIM // 2,
        rest_is_rope,
        lambda n: jnp.where(is_qb(n), HEAD_DIM ** -0.5 * LOG2E, 1.0), "in_proj_rest")
    return qk_a, rest


def _gated_residual_matmul(a_list, a_maps, w, layer, x, mod, gate_idx, tm, tn, m_outer, name,
                           next_norm=None):
    last = (w.shape[1] // MM_TK if len(a_list) == 1 else 1) - 1

    def xo_map(n, k, r, r0):
        return (jnp.where(k == last, r, r0), n)

    def mod_row(idx):
        return ((None, SUBLANES, tn), lambda n, k, r, r0: (idx, 0, n))

    args = [x, mod]
    blocks = [((tm, tn), xo_map), mod_row(gate_idx)]
    out_blocks = [((tm, tn), xo_map)]
    out_shapes = [jax.ShapeDtypeStruct((TOKENS, D_MODEL), F32)]
    epilogue = _gated_residual_epilogue
    if next_norm is not None:
        assert last == 0
        gain, scale_idx = next_norm
        args += [gain, mod]
        blocks += [((1, tn), lambda n, k, r, r0: (0, n)), mod_row(scale_idx)]
        out_blocks += [((tm, tn), xo_map), ((None, tm, LANES), lambda n, k, r, r0: (n, r, 0))]
        out_shapes += [jax.ShapeDtypeStruct((TOKENS, D_MODEL), BF16),
                       jax.ShapeDtypeStruct((D_MODEL // tn, TOKENS, LANES), F32)]
        epilogue = _gated_residual_prenorm_epilogue
    return _matmul(a_list, a_maps, w, layer, 0, D_MODEL, tm, tn, m_outer, epilogue, args, blocks,
                   out_blocks, out_shapes, name)


UP_TILE = (2048, 1024)
OUT_TILE = (1024, 1024)
DOWN_TILE = (2048, 512)
DOWN_M_OUTER = 2
K_BLOCKS = D_FF // MM_TK


def _out_proj(out_a, out_b, w_out, layer, x, mod, gate_idx, mlp_gain, mlp_scale_idx):
    tm, tn = OUT_TILE
    a_maps = [((tm, a.shape[1]), lambda n, k, r, r0: (r, 0)) for a in (out_a, out_b)]
    return _gated_residual_matmul([out_a, out_b], a_maps, w_out, layer, x, mod, gate_idx, tm, tn, 1,
                                  "out_proj", next_norm=(mlp_gain, mlp_scale_idx))


def _up_proj(xs, ss, w_up, layer, mod, shift_idx):
    tm, tn = UP_TILE
    per_block = MM_TK // tn
    blocks = [((ss.shape[0], tm, LANES), lambda n, k, r, r0: (0, r, 0)),
              ((None, SUBLANES, D_MODEL), lambda n, k, r, r0: (shift_idx, 0, 0))]
    return _matmul(
        [xs], [((tm, D_MODEL), lambda n, k, r, r0: (r, 0))], w_up, layer, 0, D_FF, tm, tn, 1,
        _postnorm_relu2_epilogue, [ss, mod], blocks,
        [((None, tm, tn), lambda n, k, r, r0: (n // per_block, r, n % per_block))],
        [jax.ShapeDtypeStruct((K_BLOCKS, TOKENS, MM_TK), BF16)], "up_proj", shift_idx=1)[0]


def _down_proj(hidden, w_down, layer, x, mod, gate_idx):
    tm, tn = DOWN_TILE
    a_maps = [((None, tm, MM_TK), lambda n, k, r, r0: (k, r, 0))]
    return _gated_residual_matmul([hidden], a_maps, w_down, layer, x, mod, gate_idx, tm, tn,
                                  DOWN_M_OUTER, "down_proj")[0]


DIFF_CHUNK = 512
ATTN_LOOKAHEAD = 3
_NT = (((1,), (1,)), ((), ()))
_TN = (((0,), (0,)), ((), ()))


SWA_BAND = 3 * BLOCK
N_BLOCKS = SEQ // BLOCK
HEADS_PER_GROUP = N_HEADS_DIFF // N_KV_SWA
SWA_QB = N_BLOCKS // HEADS_PER_GROUP


def _attn_kernel(q_ref, k_ref, v_ref, lam_ref, g_ref, sq_ref, sk_ref, sv_ref, sink_ref,
                 o_ref, so_ref, *, lam_init):
    h = pl.program_id(1)
    k = k_ref[...]
    v = v_ref[...]
    lane = lax.broadcasted_iota(jnp.int32, (1, LANES), 1)
    lp = lam_ref[...]
    lam = (jnp.exp(jnp.sum(lp[0:1] * lp[1:2], axis=-1, keepdims=True))
           - jnp.exp(jnp.sum(lp[2:3] * lp[3:4], axis=-1, keepdims=True)) + lam_init)
    n_chunks = SEQ // DIFF_CHUNK
    results = {}

    def diff_logits(t):
        chunk, second = divmod(t, 2)
        q = q_ref[chunk * DIFF_CHUNK:(chunk + 1) * DIFF_CHUNK, :]
        keep = (lane >= DIFF_QK_DIM) if second else (lane < DIFF_QK_DIM)
        qm = jnp.where(keep, q, jnp.zeros_like(q))
        return lax.dot_general(k, qm, _NT, preferred_element_type=F32)

    def diff_softmax_pv(t, st):
        e = jnp.exp2(st - jnp.max(st, axis=0, keepdims=True))
        denom = jnp.sum(e, axis=0, keepdims=True)
        ot = lax.dot_general(v, e.astype(BF16), _TN, preferred_element_type=F32)
        results[t] = (ot, denom)

    sink = sink_ref[...] * LOG2E

    def band_start(j):
        n = (h % HEADS_PER_GROUP) * SWA_QB + j
        return n, pl.multiple_of(jnp.clip(n - 1, 0, N_BLOCKS - 3) * BLOCK, BLOCK)

    def swa_logits(j):
        n, start = band_start(j)
        q = sq_ref[:, j * BLOCK:(j + 1) * BLOCK, :].reshape(GQA_RATIO * BLOCK, HEAD_DIM)
        kb = sk_ref[pl.ds(start, SWA_BAND), :]
        return lax.dot_general(kb, q, _NT, preferred_element_type=F32)

    def swa_softmax_pv(j, st):
        n, start = band_start(j)
        kpos = start + lax.broadcasted_iota(jnp.int32, (SWA_BAND, BLOCK), 0)
        qpos = n * BLOCK + lax.broadcasted_iota(jnp.int32, (SWA_BAND, BLOCK), 1)
        bias = jnp.where(jnp.abs(kpos - qpos) <= WINDOW, 0.0, -jnp.inf).astype(F32)
        st = st + jnp.concatenate([bias] * GQA_RATIO, axis=1)
        mx = jnp.maximum(jnp.max(st, axis=0, keepdims=True), sink)
        e = jnp.exp2(st - mx)
        denom = jnp.sum(e, axis=0, keepdims=True) + jnp.exp2(sink - mx)
        vb = sv_ref[pl.ds(start, SWA_BAND), :]
        ot = lax.dot_general(vb, e.astype(BF16), _TN, preferred_element_type=F32)
        ot = ot * (1.0 / denom)
        for r in range(GQA_RATIO):
            so_ref[j * BLOCK:(j + 1) * BLOCK, r * HEAD_DIM:(r + 1) * HEAD_DIM] = (
                ot[:, r * BLOCK:(r + 1) * BLOCK].T.astype(so_ref.dtype))

    diff_items = [(functools.partial(diff_logits, t), functools.partial(diff_softmax_pv, t))
                  for t in range(2 * n_chunks)]
    swa_items = [(functools.partial(swa_logits, j), functools.partial(swa_softmax_pv, j))
                 for j in range(SWA_QB)]
    per_swa = len(diff_items) // len(swa_items)
    items = []
    for j, swa_item in enumerate(swa_items):
        items += diff_items[j * per_swa:(j + 1) * per_swa] + [swa_item]
    _software_pipeline(items, ATTN_LOOKAHEAD)

    for chunk in range(n_chunks):
        (ot0, l0), (ot1, l1) = results[2 * chunk], results[2 * chunk + 1]
        o = (ot0 * (1.0 / l0) - ot1 * (lam / l1)).T
        o = o * lax.rsqrt(jnp.mean(o * o, axis=-1, keepdims=True) + EPS)
        o_ref[chunk * DIFF_CHUNK:(chunk + 1) * DIFF_CHUNK, :] = (
            o * g_ref[...] * (1.0 - lam_init)).astype(o_ref.dtype)


def _attention(qk_a, rest, diff_lambda, subln, sink, layer):
    lam_init = 0.8 - 0.6 * math.exp(-0.3 * layer)
    rows = SWA_QB * BLOCK
    per_seq = SEQ // rows
    sink_rows = jnp.repeat(sink, BLOCK).reshape(N_KV_SWA, 1, GQA_RATIO * BLOCK)

    def group(h):
        return h // HEADS_PER_GROUP

    def seq_part(h):
        return h % HEADS_PER_GROUP

    return pl.pallas_call(
        functools.partial(_attn_kernel, lam_init=lam_init),
        grid=(BATCH, N_HEADS_DIFF),
        in_specs=[
            pl.BlockSpec((None, SEQ, LANES), lambda b, h: (h, b, 0)),
            pl.BlockSpec((None, SEQ, LANES), lambda b, h: (N_HEADS_DIFF + h, b, 0)),
            pl.BlockSpec((None, SEQ, LANES), lambda b, h: (VA_BLK + h, b, 0)),
            pl.BlockSpec((None, 4, DIFF_QK_DIM), lambda b, h: (layer, 0, 0)),
            pl.BlockSpec((None, 1, HEAD_DIM), lambda b, h: (layer, 0, 0)),
            pl.BlockSpec((GQA_RATIO, rows, LANES),
                         lambda b, h: (QB_BLK // GQA_RATIO + group(h), b * per_seq + seq_part(h), 0)),
            pl.BlockSpec((None, SEQ, LANES), lambda b, h: (KB_BLK + group(h), b, 0)),
            pl.BlockSpec((None, SEQ, LANES), lambda b, h: (VB_BLK + group(h), b, 0)),
            pl.BlockSpec((None, 1, GQA_RATIO * BLOCK), lambda b, h: (group(h), 0, 0)),
        ],
        out_specs=[
            pl.BlockSpec((SEQ, LANES), lambda b, h: (b, h)),
            pl.BlockSpec((rows, GQA_RATIO * HEAD_DIM),
                         lambda b, h: (b * per_seq + seq_part(h), group(h))),
        ],
        out_shape=[jax.ShapeDtypeStruct((TOKENS, DIFF_W), BF16),
                   jax.ShapeDtypeStruct((TOKENS, SWA_Q_W), BF16)],
        compiler_params=_params("arbitrary", "arbitrary"),
        name="attention",
    )(qk_a, qk_a, rest, diff_lambda, subln.reshape(DEPTH, 1, HEAD_DIM),
      rest, rest, rest, sink_rows)


def kernel(x, c, positions, ada_w, ada_b, norm_mix, w_in, diff_lambda, diff_subln, swa_sink,
           w_out, norm_mlp, w_up, w_down, final_norm):
    mod = _modulation(c, ada_w, ada_b)
    tables = _rope_tables(positions)
    xr = x.reshape(TOKENS, D_MODEL)
    for layer in range(DEPTH):
        base = layer * N_MOD
        h = _norm(xr, norm_mix[layer], mod, base + 1, base + 0)
        qk_a, rest = _in_proj(h, w_in, layer, tables)
        out_a, out_b = _attention(qk_a, rest, diff_lambda, diff_subln, swa_sink[layer], layer)
        xr, xs, ss = _out_proj(out_a, out_b, w_out, layer, xr, mod, base + 2,
                               norm_mlp[layer].reshape(1, D_MODEL), base + 4)
        hidden = _up_proj(xs, ss, w_up, layer, mod, base + 3)
        xr = _down_proj(hidden, w_down, layer, xr, mod, base + 5)
    out = _norm(xr, final_norm, out_dtype=F32)
    return out.reshape(BATCH, SEQ, D_MODEL)
```

```python
import functools
import math

import jax
import jax.numpy as jnp
from jax import lax
from jax.experimental import pallas as pl
from jax.experimental.pallas import tpu as pltpu

D_MODEL = 2048
BATCH = 4
SEQ = 2048
DEPTH = 2
HEAD_DIM = 128
N_HEADS_DIFF = 8
DIFF_QK_DIM = 64
N_HEADS_SWA = 8
N_KV_SWA = 2
GQA_RATIO = N_HEADS_SWA // N_KV_SWA
WINDOW = 128
BLOCK = 128
D_FF = 4 * D_MODEL
ROPE_THETA = 10000.0
EPS = 1e-6
N_MOD = 6
DIFF_W = N_HEADS_DIFF * HEAD_DIM
SWA_Q_W = N_HEADS_SWA * HEAD_DIM
SWA_KV_W = N_KV_SWA * HEAD_DIM
IN_WIDTH = 3 * DIFF_W + SWA_Q_W + 2 * SWA_KV_W
TOKENS = BATCH * SEQ

LANES = 128
SUBLANES = 8
VMEM_LIMIT = 56 * 1024 * 1024

F32 = jnp.float32
BF16 = jnp.bfloat16
LOG2E = math.log2(math.e)


def _params(*sem):
    return pltpu.CompilerParams(dimension_semantics=sem, vmem_limit_bytes=VMEM_LIMIT)


def _software_pipeline(n_items, produce, consume, lookahead):
    pending = {}
    for t in range(n_items + lookahead):
        if t < n_items:
            pending[t] = produce(t)
        if t >= lookahead:
            consume(t - lookahead, pending.pop(t - lookahead))


MOD_TN = 1024


def _mod_kernel(c_ref, w_ref, b_ref, o_ref):
    c = c_ref[...]
    c_act = (c * jax.nn.sigmoid(c)).astype(BF16)
    o_ref[...] = jnp.dot(c_act, w_ref[...].astype(BF16), preferred_element_type=F32) + b_ref[...]


def _modulation(c, ada_w, ada_b):
    c_pad = jnp.pad(c, ((0, SUBLANES - BATCH), (0, 0)))
    per_chunk = D_MODEL // MOD_TN
    return pl.pallas_call(
        _mod_kernel,
        grid=(DEPTH, N_MOD * per_chunk),
        in_specs=[
            pl.BlockSpec((SUBLANES, D_MODEL), lambda l, j: (0, 0)),
            pl.BlockSpec((None, D_MODEL, MOD_TN), lambda l, j: (l, 0, j)),
            pl.BlockSpec((None, 1, MOD_TN), lambda l, j: (l, 0, j)),
        ],
        out_specs=pl.BlockSpec((None, SUBLANES, MOD_TN),
                               lambda l, j: (l * N_MOD + j // per_chunk, 0, j % per_chunk)),
        out_shape=jax.ShapeDtypeStruct((DEPTH * N_MOD, SUBLANES, D_MODEL), F32),
        compiler_params=_params("arbitrary", "arbitrary"),
        name="adaln_mod",
    )(c_pad, ada_w, ada_b.reshape(DEPTH, 1, N_MOD * D_MODEL))


ROPE_TM = 1024


def _rope_kernel(pos_ref, inv_a_ref, inv_b_ref, ca_ref, sa_ref, cb_ref, sb_ref):
    pos = pos_ref[...].astype(F32)
    lane = lax.broadcasted_iota(jnp.int32, (1, LANES), 1)
    ang_a = pos * inv_a_ref[...]
    ang_b = pos * inv_b_ref[...]
    sign_a = jnp.where((lane & (DIFF_QK_DIM // 2)) == 0, -1.0, 1.0).astype(F32)
    sign_b = jnp.where((lane & (HEAD_DIM // 2)) == 0, -1.0, 1.0).astype(F32)
    ca_ref[...] = jnp.cos(ang_a)
    sa_ref[...] = jnp.sin(ang_a) * sign_a
    cb_ref[...] = jnp.cos(ang_b)
    sb_ref[...] = jnp.sin(ang_b) * sign_b


def _rope_tables(positions):
    def inv_freq(dim):
        return ROPE_THETA ** (-jnp.arange(0, dim, 2, dtype=F32) / dim)
    inv_a = jnp.tile(inv_freq(DIFF_QK_DIM), LANES // (DIFF_QK_DIM // 2)).reshape(1, LANES)
    inv_b = jnp.tile(inv_freq(HEAD_DIM), LANES // (HEAD_DIM // 2)).reshape(1, LANES)
    tab = jax.ShapeDtypeStruct((TOKENS, LANES), F32)
    row = pl.BlockSpec((ROPE_TM, LANES), lambda m: (m, 0))
    const = pl.BlockSpec((1, LANES), lambda m: (0, 0))
    return pl.pallas_call(
        _rope_kernel,
        grid=(TOKENS // ROPE_TM,),
        in_specs=[pl.BlockSpec((ROPE_TM, 1), lambda m: (m, 0)), const, const],
        out_specs=[row, row, row, row],
        out_shape=[tab, tab, tab, tab],
        compiler_params=_params("arbitrary"),
        name="rope_tables",
    )(positions.reshape(TOKENS, 1), inv_a, inv_b)


NORM_TM = 1024


def _norm_kernel(x_ref, g_ref, *rest, modulated):
    x = x_ref[...]
    y = x * lax.rsqrt(jnp.mean(x * x, axis=-1, keepdims=True) + EPS)
    y = y * g_ref[...]
    if modulated:
        sc_ref, sh_ref, o_ref = rest
        b = (pl.program_id(0) * NORM_TM) // SEQ
        y = y * (1.0 + sc_ref[pl.ds(b, 1), :]) + sh_ref[pl.ds(b, 1), :]
    else:
        (o_ref,) = rest
    o_ref[...] = y.astype(o_ref.dtype)


def _norm(x, gain, mod=None, sc_idx=None, sh_idx=None, out_dtype=BF16):
    in_specs = [pl.BlockSpec((NORM_TM, D_MODEL), lambda m: (m, 0)),
                pl.BlockSpec((1, D_MODEL), lambda m: (0, 0))]
    args = [x, gain.reshape(1, D_MODEL)]
    if mod is not None:
        in_specs += [pl.BlockSpec((None, SUBLANES, D_MODEL), lambda m: (sc_idx, 0, 0)),
                     pl.BlockSpec((None, SUBLANES, D_MODEL), lambda m: (sh_idx, 0, 0))]
        args += [mod, mod]
    return pl.pallas_call(
        functools.partial(_norm_kernel, modulated=mod is not None),
        grid=(TOKENS // NORM_TM,),
        in_specs=in_specs,
        out_specs=pl.BlockSpec((NORM_TM, D_MODEL), lambda m: (m, 0)),
        out_shape=jax.ShapeDtypeStruct((TOKENS, D_MODEL), out_dtype),
        compiler_params=_params("arbitrary"),
        name="rmsnorm_mod" if mod is not None else "rmsnorm_final",
    )(*args)


MM_TK = 2048
MM_SUB = 512


def _swap_halves(x, half):
    if 2 * half == LANES:
        return pltpu.roll(x, half, 1)
    lane = lax.broadcasted_iota(jnp.int32, (1, LANES), 1)
    return jnp.where((lane & half) == 0,
                     pltpu.roll(x, LANES - half, 1), pltpu.roll(x, half, 1))


def _head_epilogue(acc, rows, row0, n, refs, outs, bias, *, half, is_rope, q_scale):
    cos_ref, sin_ref = refs
    (o_ref,) = outs
    for j in range(o_ref.shape[0]):
        x = acc[:, j * LANES:(j + 1) * LANES]
        rotate = is_rope(n, j)
        if rotate is not False:
            cos = cos_ref[rows, :]
            sin_signed = sin_ref[rows, :]
            if rotate is not True:
                cos = jnp.where(rotate, cos, 1.0)
                sin_signed = jnp.where(rotate, sin_signed, 0.0)
            x = (x * cos + _swap_halves(x, half) * sin_signed) * q_scale(n)
        o_ref[j, rows, :] = x.astype(o_ref.dtype)


def _gated_residual_epilogue(acc, rows, row0, n, refs, outs, bias):
    x_ref, g_ref = refs
    outs[0][rows, :] = x_ref[rows, :] + g_ref[pl.ds(row0 // SEQ, 1), :] * acc


def _gated_residual_prenorm_epilogue(acc, rows, row0, n, refs, outs, bias):
    x_ref, g_ref, gain_ref, sc_ref = refs
    x_out, xs_out, ss_out = outs
    b = row0 // SEQ
    xn = x_ref[rows, :] + g_ref[pl.ds(b, 1), :] * acc
    x_out[rows, :] = xn
    xs_out[rows, :] = (xn * (gain_ref[...] * (1.0 + sc_ref[pl.ds(b, 1), :]))).astype(xs_out.dtype)
    ss_out[rows, :] = jnp.broadcast_to(jnp.sum(xn * xn, axis=-1, keepdims=True),
                                       (rows.stop - rows.start, LANES))


def _relu2_epilogue(acc, rows, row0, n, refs, outs, bias):
    r = jnp.maximum(acc, 0.0)
    outs[0][rows, :] = (r * r).astype(outs[0].dtype)


def _postnorm_relu2_epilogue(acc, rows, row0, n, refs, outs, bias):
    ss_ref = refs[0]
    ss = ss_ref[0, rows, :]
    for i in range(1, ss_ref.shape[0]):
        ss = ss + ss_ref[i, rows, :]
    inv = lax.rsqrt(ss * (1.0 / D_MODEL) + EPS)
    inv = jnp.concatenate([inv] * (acc.shape[1] // LANES), axis=1)
    r = jnp.maximum(acc * inv + bias[pl.ds(row0 // SEQ, 1), :], 0.0)
    outs[0][rows, :] = (r * r).astype(outs[0].dtype)


def _matmul_kernel(*refs, n_a, n_extra, n_out, nk, tm, m_inner, epilogue, shift_idx):
    a_refs = refs[:n_a]
    w_ref = refs[n_a]
    extra = refs[n_a + 1:n_a + 1 + n_extra]
    outs = refs[n_a + 1 + n_extra:n_a + 1 + n_extra + n_out]
    scratch = refs[n_a + 1 + n_extra + n_out:]
    wb_ref = scratch[0]
    bias_ref = scratch[-1] if shift_idx is not None else None
    outer, n, k, m = (pl.program_id(i) for i in range(4))

    if nk > 1:
        acc_ref = scratch[1]

        first_nk = jnp.logical_and(n == 0, k == 0)
        @pl.when(jnp.logical_and(jnp.logical_and(outer == 0, m == 0), first_nk))
        def _():
            acc_ref[...] = jnp.zeros_like(acc_ref)

    @pl.when(m == 0)
    def _():
        wb_ref[...] = w_ref[...].astype(BF16)
        if shift_idx is not None:
            bias_ref[...] = jnp.dot(extra[shift_idx][...].astype(BF16), wb_ref[...],
                                    preferred_element_type=F32)

    def product(t):
        rows = slice(t * MM_SUB, (t + 1) * MM_SUB)
        part = None
        off = 0
        for a_ref in a_refs:
            ka = a_ref.shape[1]
            d = jnp.dot(a_ref[rows, :], wb_ref[off:off + ka, :], preferred_element_type=F32)
            part = d if part is None else part + d
            off += ka
        return part

    def finish(t, part):
        rows = slice(t * MM_SUB, (t + 1) * MM_SUB)
        local_row0 = m * tm + t * MM_SUB
        if nk > 1:
            arows = pl.ds(pl.multiple_of(local_row0, MM_SUB), MM_SUB)
            part = jnp.where(k == 0, 0.0, acc_ref[arows, :]) + part
            acc_ref[arows, :] = part
        epilogue(part, rows, outer * (m_inner * tm) + local_row0, n, extra, outs, bias_ref)

    _software_pipeline(tm // MM_SUB, product, finish, lookahead=1)


def _block(shape, index_map, m_inner):
    return pl.BlockSpec(
        shape, lambda o, n, k, m: index_map(n, k, o * m_inner + m, o * m_inner))


def _matmul(a_list, a_maps, w, layer, col0, n_cols, tm, tn, m_outer, epilogue, extra_args,
            extra_blocks, out_blocks, out_shapes, name, shift_idx=None):
    k_total = w.shape[1]
    assert col0 % tn == 0 and n_cols % tn == 0 and tm % MM_SUB == 0
    assert TOKENS % (tm * m_outer) == 0
    n_off = col0 // tn
    nk = k_total // MM_TK if len(a_list) == 1 else 1
    tk = k_total // nk
    m_inner = TOKENS // tm // m_outer
    grid = (m_outer, n_cols // tn, nk, m_inner)
    a_specs = [_block(shape, imap, m_inner) for shape, imap in a_maps]
    w_spec = pl.BlockSpec((None, tk, tn), lambda o, n, k, m: (layer, k, n + n_off))
    scratch = [pltpu.VMEM((tk, tn), BF16)]
    if nk > 1:
        scratch.append(pltpu.VMEM((m_inner * tm, tn), F32))
    if shift_idx is not None:
        assert nk == 1
        scratch.append(pltpu.VMEM((SUBLANES, tn), F32))
    return pl.pallas_call(
        functools.partial(_matmul_kernel, n_a=len(a_list), n_extra=len(extra_args),
                          n_out=len(out_blocks), nk=nk, tm=tm, m_inner=m_inner, epilogue=epilogue,
                          shift_idx=shift_idx),
        grid=grid,
        in_specs=a_specs + [w_spec] + [_block(s, f, m_inner) for s, f in extra_blocks],
        out_specs=[_block(s, f, m_inner) for s, f in out_blocks],
        out_shape=out_shapes,
        scratch_shapes=scratch,
        compiler_params=_params("arbitrary", "arbitrary", "arbitrary", "arbitrary"),
        name=name,
    )(*a_list, w, *extra_args)


def _head_proj(h, w_in, layer, col0, n_cols, tm, tn, tables, half, is_rope, q_scale, name):
    table_blocks = [((tm, LANES), lambda n, k, r, r0: (r, 0)) for _ in tables]
    epilogue = functools.partial(_head_epilogue, half=half, is_rope=is_rope, q_scale=q_scale)
    return _matmul(
        [h], [((tm, D_MODEL), lambda n, k, r, r0: (r, 0))], w_in, layer, col0, n_cols, tm, tn, 1,
        epilogue, list(tables), table_blocks,
        [((tn // LANES, tm, LANES), lambda n, k, r, r0: (n, r, 0))],
        [jax.ShapeDtypeStruct((n_cols // LANES, TOKENS, LANES), BF16)], name)[0]


DIFF_QK_TN = 1024
REST_TN = 512
REST_QB_TILE0 = DIFF_W // REST_TN
REST_KV_TILE = (DIFF_W + SWA_Q_W) // REST_TN
REST_KB_GROUPS = SWA_KV_W // LANES
VA_BLK = 0
QB_BLK = DIFF_W // LANES
KB_BLK = QB_BLK + SWA_Q_W // LANES
VB_BLK = KB_BLK + SWA_KV_W // LANES


def _in_proj(h, w_in, layer, tables):
    ca, sa, cb, sb = tables
    qk_a = _head_proj(
        h, w_in, layer, 0, 2 * DIFF_W, 2048, DIFF_QK_TN, (ca, sa), DIFF_QK_DIM // 2,
        lambda n, j: True,
        lambda n: jnp.where(n == 0, DIFF_QK_DIM ** -0.5 * LOG2E, 1.0), "in_proj_qk_diff")

    def is_qb(n):
        return jnp.logical_and(n >= REST_QB_TILE0, n < REST_KV_TILE)

    def rest_is_rope(n, j):
        return n >= REST_QB_TILE0 if j < REST_KB_GROUPS else is_qb(n)

    rest = _head_proj(
        h, w_in, layer, 2 * DIFF_W, IN_WIDTH - 2 * DIFF_W, 2048, REST_TN, (cb, sb), HEAD_DIM // 2,
        rest_is_rope,
        lambda n: jnp.where(is_qb(n), HEAD_DIM ** -0.5 * LOG2E, 1.0), "in_proj_rest")
    return qk_a, rest


def _gated_residual_matmul(a_list, a_maps, w, layer, x, mod, gate_idx, tm, tn, m_outer, name,
                           next_norm=None):
    last = (w.shape[1] // MM_TK if len(a_list) == 1 else 1) - 1

    def xo_map(n, k, r, r0):
        return (jnp.where(k == last, r, r0), n)

    def mod_row(idx):
        return ((None, SUBLANES, tn), lambda n, k, r, r0: (idx, 0, n))

    args = [x, mod]
    blocks = [((tm, tn), xo_map), mod_row(gate_idx)]
    out_blocks = [((tm, tn), xo_map)]
    out_shapes = [jax.ShapeDtypeStruct((TOKENS, D_MODEL), F32)]
    epilogue = _gated_residual_epilogue
    if next_norm is not None:
        assert last == 0
        gain, scale_idx = next_norm
        args += [gain, mod]
        blocks += [((1, tn), lambda n, k, r, r0: (0, n)), mod_row(scale_idx)]
        out_blocks += [((tm, tn), xo_map), ((None, tm, LANES), lambda n, k, r, r0: (n, r, 0))]
        out_shapes += [jax.ShapeDtypeStruct((TOKENS, D_MODEL), BF16),
                       jax.ShapeDtypeStruct((D_MODEL // tn, TOKENS, LANES), F32)]
        epilogue = _gated_residual_prenorm_epilogue
    return _matmul(a_list, a_maps, w, layer, 0, D_MODEL, tm, tn, m_outer, epilogue, args, blocks,
                   out_blocks, out_shapes, name)


UP_TILE = (2048, 1024)
OUT_TILE = (1024, 1024)
DOWN_TILE = (2048, 512)
DOWN_M_OUTER = 2
K_BLOCKS = D_FF // MM_TK


def _out_proj(out_a, out_b, w_out, layer, x, mod, gate_idx, mlp_gain, mlp_scale_idx):
    tm, tn = OUT_TILE
    a_maps = [((tm, a.shape[1]), lambda n, k, r, r0: (r, 0)) for a in (out_a, out_b)]
    return _gated_residual_matmul([out_a, out_b], a_maps, w_out, layer, x, mod, gate_idx, tm, tn, 1,
                                  "out_proj", next_norm=(mlp_gain, mlp_scale_idx))


def _up_proj(xs, ss, w_up, layer, mod, shift_idx):
    tm, tn = UP_TILE
    per_block = MM_TK // tn
    blocks = [((ss.shape[0], tm, LANES), lambda n, k, r, r0: (0, r, 0)),
              ((None, SUBLANES, D_MODEL), lambda n, k, r, r0: (shift_idx, 0, 0))]
    return _matmul(
        [xs], [((tm, D_MODEL), lambda n, k, r, r0: (r, 0))], w_up, layer, 0, D_FF, tm, tn, 1,
        _postnorm_relu2_epilogue, [ss, mod], blocks,
        [((None, tm, tn), lambda n, k, r, r0: (n // per_block, r, n % per_block))],
        [jax.ShapeDtypeStruct((K_BLOCKS, TOKENS, MM_TK), BF16)], "up_proj", shift_idx=1)[0]


def _down_proj(hidden, w_down, layer, x, mod, gate_idx):
    tm, tn = DOWN_TILE
    a_maps = [((None, tm, MM_TK), lambda n, k, r, r0: (k, r, 0))]
    return _gated_residual_matmul([hidden], a_maps, w_down, layer, x, mod, gate_idx, tm, tn,
                                  DOWN_M_OUTER, "down_proj")[0]


DIFF_CHUNK = 512
DIFF_HEADS_PER_STEP = 2
ATTN_LOOKAHEAD = 3
_NT = (((1,), (1,)), ((), ()))
_TN = (((0,), (0,)), ((), ()))


def _diff_attn_kernel(q_ref, k_ref, v_ref, lam_ref, g_ref, o_ref, *, lam_init):
    lane = lax.broadcasted_iota(jnp.int32, (1, LANES), 1)
    lp = lam_ref[...]
    lam = (jnp.exp(jnp.sum(lp[0:1] * lp[1:2], axis=-1, keepdims=True))
           - jnp.exp(jnp.sum(lp[2:3] * lp[3:4], axis=-1, keepdims=True)) + lam_init)
    n_chunks = SEQ // DIFF_CHUNK
    per_head = 2 * n_chunks
    results = {}

    def logits(t):
        head, rem = divmod(t, per_head)
        chunk, second = divmod(rem, 2)
        q = q_ref[head, chunk * DIFF_CHUNK:(chunk + 1) * DIFF_CHUNK, :]
        keep = (lane >= DIFF_QK_DIM) if second else (lane < DIFF_QK_DIM)
        qm = jnp.where(keep, q, jnp.zeros_like(q))
        return lax.dot_general(k_ref[head], qm, _NT, preferred_element_type=F32)

    def softmax_pv(t, st):
        e = jnp.exp2(st - jnp.max(st, axis=0, keepdims=True))
        denom = jnp.sum(e, axis=0, keepdims=True)
        ot = lax.dot_general(v_ref[t // per_head], e.astype(BF16), _TN,
                             preferred_element_type=F32)
        results[t] = (ot, denom)

    _software_pipeline(DIFF_HEADS_PER_STEP * per_head, logits, softmax_pv, ATTN_LOOKAHEAD)
    for head in range(DIFF_HEADS_PER_STEP):
        for chunk in range(n_chunks):
            t = head * per_head + 2 * chunk
            (ot0, l0), (ot1, l1) = results[t], results[t + 1]
            o = (ot0 * (1.0 / l0) - ot1 * (lam / l1)).T
            o = o * lax.rsqrt(jnp.mean(o * o, axis=-1, keepdims=True) + EPS)
            o_ref[chunk * DIFF_CHUNK:(chunk + 1) * DIFF_CHUNK, head * HEAD_DIM:(head + 1) * HEAD_DIM] = (
                o * g_ref[...] * (1.0 - lam_init)).astype(o_ref.dtype)


def _diff_attention(qk_a, rest, diff_lambda, subln, layer):
    lam_init = 0.8 - 0.6 * math.exp(-0.3 * layer)
    hps = DIFF_HEADS_PER_STEP
    return pl.pallas_call(
        functools.partial(_diff_attn_kernel, lam_init=lam_init),
        grid=(BATCH, N_HEADS_DIFF // hps),
        in_specs=[
            pl.BlockSpec((hps, SEQ, LANES), lambda b, h: (h, b, 0)),
            pl.BlockSpec((hps, SEQ, LANES), lambda b, h: (N_HEADS_DIFF // hps + h, b, 0)),
            pl.BlockSpec((hps, SEQ, LANES), lambda b, h: (VA_BLK // hps + h, b, 0)),
            pl.BlockSpec((None, 4, DIFF_QK_DIM), lambda b, h: (layer, 0, 0)),
            pl.BlockSpec((None, 1, HEAD_DIM), lambda b, h: (layer, 0, 0)),
        ],
        out_specs=pl.BlockSpec((SEQ, hps * HEAD_DIM), lambda b, h: (b, h)),
        out_shape=jax.ShapeDtypeStruct((TOKENS, DIFF_W), BF16),
        compiler_params=_params("arbitrary", "arbitrary"),
        name="diff_attn",
    )(qk_a, qk_a, rest, diff_lambda, subln.reshape(DEPTH, 1, HEAD_DIM))


SWA_BAND = 3 * BLOCK
SWA_QB = 16


def _swa_attn_kernel(q_ref, k_ref, v_ref, sink_ref, o_ref):
    step = pl.program_id(2)
    sink = sink_ref[...] * LOG2E

    def band_start(j):
        n = step * SWA_QB + j
        return n, pl.multiple_of(jnp.clip(n - 1, 0, SEQ // BLOCK - 3) * BLOCK, BLOCK)

    def logits(j):
        n, start = band_start(j)
        q = q_ref[:, j * BLOCK:(j + 1) * BLOCK, :].reshape(GQA_RATIO * BLOCK, HEAD_DIM)
        kb = k_ref[pl.ds(start, SWA_BAND), :]
        return lax.dot_general(kb, q, _NT, preferred_element_type=F32)

    def softmax_pv(j, st):
        n, start = band_start(j)
        kpos = start + lax.broadcasted_iota(jnp.int32, (SWA_BAND, BLOCK), 0)
        qpos = n * BLOCK + lax.broadcasted_iota(jnp.int32, (SWA_BAND, BLOCK), 1)
        bias = jnp.where(jnp.abs(kpos - qpos) <= WINDOW, 0.0, -jnp.inf).astype(F32)
        st = st + jnp.concatenate([bias] * GQA_RATIO, axis=1)
        mx = jnp.maximum(jnp.max(st, axis=0, keepdims=True), sink)
        e = jnp.exp2(st - mx)
        denom = jnp.sum(e, axis=0, keepdims=True) + jnp.exp2(sink - mx)
        vb = v_ref[pl.ds(start, SWA_BAND), :]
        ot = lax.dot_general(vb, e.astype(BF16), _TN, preferred_element_type=F32)
        ot = ot * (1.0 / denom)
        for r in range(GQA_RATIO):
            o_ref[j * BLOCK:(j + 1) * BLOCK, r * HEAD_DIM:(r + 1) * HEAD_DIM] = (
                ot[:, r * BLOCK:(r + 1) * BLOCK].T.astype(o_ref.dtype))

    _software_pipeline(SWA_QB, logits, softmax_pv, ATTN_LOOKAHEAD)


def _swa_attention(rest, sink):
    steps = SEQ // BLOCK // SWA_QB
    rows = SWA_QB * BLOCK
    sink_rows = jnp.repeat(sink, BLOCK).reshape(N_KV_SWA, 1, GQA_RATIO * BLOCK)
    return pl.pallas_call(
        _swa_attn_kernel,
        grid=(BATCH, N_KV_SWA, steps),
        in_specs=[
            pl.BlockSpec((GQA_RATIO, rows, LANES),
                         lambda b, g, n: (QB_BLK // GQA_RATIO + g, b * steps + n, 0)),
            pl.BlockSpec((None, SEQ, LANES), lambda b, g, n: (KB_BLK + g, b, 0)),
            pl.BlockSpec((None, SEQ, LANES), lambda b, g, n: (VB_BLK + g, b, 0)),
            pl.BlockSpec((None, 1, GQA_RATIO * BLOCK), lambda b, g, n: (g, 0, 0)),
        ],
        out_specs=pl.BlockSpec((rows, GQA_RATIO * HEAD_DIM), lambda b, g, n: (b * steps + n, g)),
        out_shape=jax.ShapeDtypeStruct((TOKENS, SWA_Q_W), BF16),
        compiler_params=_params("arbitrary", "arbitrary", "arbitrary"),
        name="swa_attn",
    )(rest, rest, rest, sink_rows)


def kernel(x, c, positions, ada_w, ada_b, norm_mix, w_in, diff_lambda, diff_subln, swa_sink,
           w_out, norm_mlp, w_up, w_down, final_norm):
    mod = _modulation(c, ada_w, ada_b)
    tables = _rope_tables(positions)
    xr = x.reshape(TOKENS, D_MODEL)
    for layer in range(DEPTH):
        base = layer * N_MOD
        h = _norm(xr, norm_mix[layer], mod, base + 1, base + 0)
        qk_a, rest = _in_proj(h, w_in, layer, tables)
        out_a = _diff_attention(qk_a, rest, diff_lambda, diff_subln, layer)
        out_b = _swa_attention(rest, swa_sink[layer])
        xr, xs, ss = _out_proj(out_a, out_b, w_out, layer, xr, mod, base + 2,
                               norm_mlp[layer].reshape(1, D_MODEL), base + 4)
        hidden = _up_proj(xs, ss, w_up, layer, mod, base + 3)
        xr = _down_proj(hidden, w_down, layer, xr, mod, base + 5)
    out = _norm(xr, final_norm, out_dtype=F32)
    return out.reshape(BATCH, SEQ, D_MODEL)
```

```python
import functools
import math

import jax
import jax.numpy as jnp
from jax import lax
from jax.experimental import pallas as pl
from jax.experimental.pallas import tpu as pltpu

D_MODEL = 2048
BATCH = 4
SEQ = 2048
DEPTH = 2
HEAD_DIM = 128
N_HEADS_DIFF = 8
DIFF_QK_DIM = 64
N_HEADS_SWA = 8
N_KV_SWA = 2
GQA_RATIO = N_HEADS_SWA // N_KV_SWA
WINDOW = 128
BLOCK = 128
D_FF = 4 * D_MODEL
ROPE_THETA = 10000.0
EPS = 1e-6
N_MOD = 6
DIFF_W = N_HEADS_DIFF * HEAD_DIM
SWA_Q_W = N_HEADS_SWA * HEAD_DIM
SWA_KV_W = N_KV_SWA * HEAD_DIM
IN_WIDTH = 3 * DIFF_W + SWA_Q_W + 2 * SWA_KV_W
TOKENS = BATCH * SEQ

LANES = 128
SUBLANES = 8
VMEM_LIMIT = 56 * 1024 * 1024

F32 = jnp.float32
BF16 = jnp.bfloat16
LOG2E = math.log2(math.e)


def _params(*sem):
    return pltpu.CompilerParams(dimension_semantics=sem, vmem_limit_bytes=VMEM_LIMIT)


def _software_pipeline(n_items, produce, consume, lookahead):
    pending = {}
    for t in range(n_items + lookahead):
        if t < n_items:
            pending[t] = produce(t)
        if t >= lookahead:
            consume(t - lookahead, pending.pop(t - lookahead))


MOD_TN = 1024


def _mod_kernel(c_ref, w_ref, b_ref, o_ref):
    c = c_ref[...]
    c_act = (c * jax.nn.sigmoid(c)).astype(BF16)
    o_ref[...] = jnp.dot(c_act, w_ref[...].astype(BF16), preferred_element_type=F32) + b_ref[...]


def _modulation(c, ada_w, ada_b):
    c_pad = jnp.pad(c, ((0, SUBLANES - BATCH), (0, 0)))
    per_chunk = D_MODEL // MOD_TN
    return pl.pallas_call(
        _mod_kernel,
        grid=(DEPTH, N_MOD * per_chunk),
        in_specs=[
            pl.BlockSpec((SUBLANES, D_MODEL), lambda l, j: (0, 0)),
            pl.BlockSpec((None, D_MODEL, MOD_TN), lambda l, j: (l, 0, j)),
            pl.BlockSpec((None, 1, MOD_TN), lambda l, j: (l, 0, j)),
        ],
        out_specs=pl.BlockSpec((None, SUBLANES, MOD_TN),
                               lambda l, j: (l * N_MOD + j // per_chunk, 0, j % per_chunk)),
        out_shape=jax.ShapeDtypeStruct((DEPTH * N_MOD, SUBLANES, D_MODEL), F32),
        compiler_params=_params("arbitrary", "arbitrary"),
        name="adaln_mod",
    )(c_pad, ada_w, ada_b.reshape(DEPTH, 1, N_MOD * D_MODEL))


ROPE_TM = 1024


def _rope_kernel(pos_ref, inv_a_ref, inv_b_ref, ca_ref, sa_ref, cb_ref, sb_ref):
    pos = pos_ref[...].astype(F32)
    lane = lax.broadcasted_iota(jnp.int32, (1, LANES), 1)
    ang_a = pos * inv_a_ref[...]
    ang_b = pos * inv_b_ref[...]
    sign_a = jnp.where((lane & (DIFF_QK_DIM // 2)) == 0, -1.0, 1.0).astype(F32)
    sign_b = jnp.where((lane & (HEAD_DIM // 2)) == 0, -1.0, 1.0).astype(F32)
    ca_ref[...] = jnp.cos(ang_a)
    sa_ref[...] = jnp.sin(ang_a) * sign_a
    cb_ref[...] = jnp.cos(ang_b)
    sb_ref[...] = jnp.sin(ang_b) * sign_b


def _rope_tables(positions):
    def inv_freq(dim):
        return ROPE_THETA ** (-jnp.arange(0, dim, 2, dtype=F32) / dim)
    inv_a = jnp.tile(inv_freq(DIFF_QK_DIM), LANES // (DIFF_QK_DIM // 2)).reshape(1, LANES)
    inv_b = jnp.tile(inv_freq(HEAD_DIM), LANES // (HEAD_DIM // 2)).reshape(1, LANES)
    tab = jax.ShapeDtypeStruct((TOKENS, LANES), F32)
    row = pl.BlockSpec((ROPE_TM, LANES), lambda m: (m, 0))
    const = pl.BlockSpec((1, LANES), lambda m: (0, 0))
    return pl.pallas_call(
        _rope_kernel,
        grid=(TOKENS // ROPE_TM,),
        in_specs=[pl.BlockSpec((ROPE_TM, 1), lambda m: (m, 0)), const, const],
        out_specs=[row, row, row, row],
        out_shape=[tab, tab, tab, tab],
        compiler_params=_params("arbitrary"),
        name="rope_tables",
    )(positions.reshape(TOKENS, 1), inv_a, inv_b)


NORM_TM = 1024


def _norm_kernel(x_ref, g_ref, *rest, modulated):
    x = x_ref[...]
    y = x * lax.rsqrt(jnp.mean(x * x, axis=-1, keepdims=True) + EPS)
    y = y * g_ref[...]
    if modulated:
        sc_ref, sh_ref, o_ref = rest
        b = (pl.program_id(0) * NORM_TM) // SEQ
        y = y * (1.0 + sc_ref[pl.ds(b, 1), :]) + sh_ref[pl.ds(b, 1), :]
    else:
        (o_ref,) = rest
    o_ref[...] = y.astype(o_ref.dtype)


def _norm(x, gain, mod=None, sc_idx=None, sh_idx=None, out_dtype=BF16):
    in_specs = [pl.BlockSpec((NORM_TM, D_MODEL), lambda m: (m, 0)),
                pl.BlockSpec((1, D_MODEL), lambda m: (0, 0))]
    args = [x, gain.reshape(1, D_MODEL)]
    if mod is not None:
        in_specs += [pl.BlockSpec((None, SUBLANES, D_MODEL), lambda m: (sc_idx, 0, 0)),
                     pl.BlockSpec((None, SUBLANES, D_MODEL), lambda m: (sh_idx, 0, 0))]
        args += [mod, mod]
    return pl.pallas_call(
        functools.partial(_norm_kernel, modulated=mod is not None),
        grid=(TOKENS // NORM_TM,),
        in_specs=in_specs,
        out_specs=pl.BlockSpec((NORM_TM, D_MODEL), lambda m: (m, 0)),
        out_shape=jax.ShapeDtypeStruct((TOKENS, D_MODEL), out_dtype),
        compiler_params=_params("arbitrary"),
        name="rmsnorm_mod" if mod is not None else "rmsnorm_final",
    )(*args)


MM_TK = 2048
MM_SUB = 512


def _swap_halves(x, half):
    if 2 * half == LANES:
        return pltpu.roll(x, half, 1)
    lane = lax.broadcasted_iota(jnp.int32, (1, LANES), 1)
    return jnp.where((lane & half) == 0,
                     pltpu.roll(x, LANES - half, 1), pltpu.roll(x, half, 1))


def _head_epilogue(acc, rows, row0, n, refs, outs, bias, *, half, is_rope, q_scale):
    cos_ref, sin_ref = refs
    (o_ref,) = outs
    for j in range(o_ref.shape[0]):
        x = acc[:, j * LANES:(j + 1) * LANES]
        rotate = is_rope(n, j)
        if rotate is not False:
            cos = cos_ref[rows, :]
            sin_signed = sin_ref[rows, :]
            if rotate is not True:
                cos = jnp.where(rotate, cos, 1.0)
                sin_signed = jnp.where(rotate, sin_signed, 0.0)
            x = (x * cos + _swap_halves(x, half) * sin_signed) * q_scale(n)
        o_ref[j, rows, :] = x.astype(o_ref.dtype)


def _gated_residual_epilogue(acc, rows, row0, n, refs, outs, bias):
    x_ref, g_ref = refs
    outs[0][rows, :] = x_ref[rows, :] + g_ref[pl.ds(row0 // SEQ, 1), :] * acc


def _gated_residual_prenorm_epilogue(acc, rows, row0, n, refs, outs, bias):
    x_ref, g_ref, gain_ref, sc_ref = refs
    x_out, xs_out, ss_out = outs
    b = row0 // SEQ
    xn = x_ref[rows, :] + g_ref[pl.ds(b, 1), :] * acc
    x_out[rows, :] = xn
    xs_out[rows, :] = (xn * (gain_ref[...] * (1.0 + sc_ref[pl.ds(b, 1), :]))).astype(xs_out.dtype)
    ss_out[rows, :] = jnp.broadcast_to(jnp.sum(xn * xn, axis=-1, keepdims=True),
                                       (rows.stop - rows.start, LANES))


def _relu2_epilogue(acc, rows, row0, n, refs, outs, bias):
    r = jnp.maximum(acc, 0.0)
    outs[0][rows, :] = (r * r).astype(outs[0].dtype)


def _postnorm_relu2_epilogue(acc, rows, row0, n, refs, outs, bias):
    ss_ref = refs[0]
    ss = ss_ref[0, rows, :]
    for i in range(1, ss_ref.shape[0]):
        ss = ss + ss_ref[i, rows, :]
    inv = lax.rsqrt(ss * (1.0 / D_MODEL) + EPS)
    inv = jnp.concatenate([inv] * (acc.shape[1] // LANES), axis=1)
    r = jnp.maximum(acc * inv + bias[pl.ds(row0 // SEQ, 1), :], 0.0)
    outs[0][rows, :] = (r * r).astype(outs[0].dtype)


def _matmul_kernel(*refs, n_a, n_extra, n_out, nk, tm, m_inner, epilogue, shift_idx):
    a_refs = refs[:n_a]
    w_ref = refs[n_a]
    extra = refs[n_a + 1:n_a + 1 + n_extra]
    outs = refs[n_a + 1 + n_extra:n_a + 1 + n_extra + n_out]
    scratch = refs[n_a + 1 + n_extra + n_out:]
    wb_ref = scratch[0]
    bias_ref = scratch[-1] if shift_idx is not None else None
    outer, n, k, m = (pl.program_id(i) for i in range(4))

    acc_ref = scratch[1] if nk > 1 else None

    @pl.when(m == 0)
    def _():
        wb_ref[...] = w_ref[...].astype(BF16)
        if shift_idx is not None:
            bias_ref[...] = jnp.dot(extra[shift_idx][...].astype(BF16), wb_ref[...],
                                    preferred_element_type=F32)

    def product(t):
        rows = slice(t * MM_SUB, (t + 1) * MM_SUB)
        part = None
        off = 0
        for a_ref in a_refs:
            ka = a_ref.shape[1]
            d = jnp.dot(a_ref[rows, :], wb_ref[off:off + ka, :], preferred_element_type=F32)
            part = d if part is None else part + d
            off += ka
        return part

    def finish(k_pass, t, part):
        rows = slice(t * MM_SUB, (t + 1) * MM_SUB)
        local_row0 = m * tm + t * MM_SUB
        arows = pl.ds(pl.multiple_of(local_row0, MM_SUB), MM_SUB)
        if k_pass == "first":
            acc_ref[arows, :] = part
        elif k_pass == "middle":
            acc_ref[arows, :] += part
        else:
            if k_pass == "last":
                part = acc_ref[arows, :] + part
            epilogue(part, rows, outer * (m_inner * tm) + local_row0, n, extra, outs, bias_ref)

    def run(k_pass):
        _software_pipeline(tm // MM_SUB, product, functools.partial(finish, k_pass), lookahead=1)

    if nk == 1:
        run("only")
    else:
        pl.when(k == 0)(functools.partial(run, "first"))
        if nk > 2:
            pl.when(jnp.logical_and(k > 0, k < nk - 1))(functools.partial(run, "middle"))
        pl.when(k == nk - 1)(functools.partial(run, "last"))


def _block(shape, index_map, m_inner):
    return pl.BlockSpec(
        shape, lambda o, n, k, m: index_map(n, k, o * m_inner + m, o * m_inner))


def _matmul(a_list, a_maps, w, layer, col0, n_cols, tm, tn, m_outer, epilogue, extra_args,
            extra_blocks, out_blocks, out_shapes, name, shift_idx=None):
    k_total = w.shape[1]
    assert col0 % tn == 0 and n_cols % tn == 0 and tm % MM_SUB == 0
    assert TOKENS % (tm * m_outer) == 0
    n_off = col0 // tn
    nk = k_total // MM_TK if len(a_list) == 1 else 1
    tk = k_total // nk
    m_inner = TOKENS // tm // m_outer
    grid = (m_outer, n_cols // tn, nk, m_inner)
    a_specs = [_block(shape, imap, m_inner) for shape, imap in a_maps]
    w_spec = pl.BlockSpec((None, tk, tn), lambda o, n, k, m: (layer, k, n + n_off))
    scratch = [pltpu.VMEM((tk, tn), BF16)]
    if nk > 1:
        scratch.append(pltpu.VMEM((m_inner * tm, tn), F32))
    if shift_idx is not None:
        assert nk == 1
        scratch.append(pltpu.VMEM((SUBLANES, tn), F32))
    return pl.pallas_call(
        functools.partial(_matmul_kernel, n_a=len(a_list), n_extra=len(extra_args),
                          n_out=len(out_blocks), nk=nk, tm=tm, m_inner=m_inner, epilogue=epilogue,
                          shift_idx=shift_idx),
        grid=grid,
        in_specs=a_specs + [w_spec] + [_block(s, f, m_inner) for s, f in extra_blocks],
        out_specs=[_block(s, f, m_inner) for s, f in out_blocks],
        out_shape=out_shapes,
        scratch_shapes=scratch,
        compiler_params=_params("arbitrary", "arbitrary", "arbitrary", "arbitrary"),
        name=name,
    )(*a_list, w, *extra_args)


def _head_proj(h, w_in, layer, col0, n_cols, tm, tn, tables, half, is_rope, q_scale, name):
    table_blocks = [((tm, LANES), lambda n, k, r, r0: (r, 0)) for _ in tables]
    epilogue = functools.partial(_head_epilogue, half=half, is_rope=is_rope, q_scale=q_scale)
    return _matmul(
        [h], [((tm, D_MODEL), lambda n, k, r, r0: (r, 0))], w_in, layer, col0, n_cols, tm, tn, 1,
        epilogue, list(tables), table_blocks,
        [((tn // LANES, tm, LANES), lambda n, k, r, r0: (n, r, 0))],
        [jax.ShapeDtypeStruct((n_cols // LANES, TOKENS, LANES), BF16)], name)[0]


DIFF_QK_TN = 1024
REST_TN = 512
REST_QB_TILE0 = DIFF_W // REST_TN
REST_KV_TILE = (DIFF_W + SWA_Q_W) // REST_TN
REST_KB_GROUPS = SWA_KV_W // LANES
VA_BLK = 0
QB_BLK = DIFF_W // LANES
KB_BLK = QB_BLK + SWA_Q_W // LANES
VB_BLK = KB_BLK + SWA_KV_W // LANES


def _in_proj(h, w_in, layer, tables):
    ca, sa, cb, sb = tables
    qk_a = _head_proj(
        h, w_in, layer, 0, 2 * DIFF_W, 2048, DIFF_QK_TN, (ca, sa), DIFF_QK_DIM // 2,
        lambda n, j: True,
        lambda n: jnp.where(n == 0, DIFF_QK_DIM ** -0.5 * LOG2E, 1.0), "in_proj_qk_diff")

    def is_qb(n):
        return jnp.logical_and(n >= REST_QB_TILE0, n < REST_KV_TILE)

    def rest_is_rope(n, j):
        return n >= REST_QB_TILE0 if j < REST_KB_GROUPS else is_qb(n)

    rest = _head_proj(
        h, w_in, layer, 2 * DIFF_W, IN_WIDTH - 2 * DIFF_W, 2048, REST_TN, (cb, sb), HEAD_DIM // 2,
        rest_is_rope,
        lambda n: jnp.where(is_qb(n), HEAD_DIM ** -0.5 * LOG2E, 1.0), "in_proj_rest")
    return qk_a, rest


def _gated_residual_matmul(a_list, a_maps, w, layer, x, mod, gate_idx, tm, tn, m_outer, name,
                           next_norm=None):
    last = (w.shape[1] // MM_TK if len(a_list) == 1 else 1) - 1

    def xo_map(n, k, r, r0):
        return (jnp.where(k == last, r, r0), n)

    def mod_row(idx):
        return ((None, SUBLANES, tn), lambda n, k, r, r0: (idx, 0, n))

    args = [x, mod]
    blocks = [((tm, tn), xo_map), mod_row(gate_idx)]
    out_blocks = [((tm, tn), xo_map)]
    out_shapes = [jax.ShapeDtypeStruct((TOKENS, D_MODEL), F32)]
    epilogue = _gated_residual_epilogue
    if next_norm is not None:
        assert last == 0
        gain, scale_idx = next_norm
        args += [gain, mod]
        blocks += [((1, tn), lambda n, k, r, r0: (0, n)), mod_row(scale_idx)]
        out_blocks += [((tm, tn), xo_map), ((None, tm, LANES), lambda n, k, r, r0: (n, r, 0))]
        out_shapes += [jax.ShapeDtypeStruct((TOKENS, D_MODEL), BF16),
                       jax.ShapeDtypeStruct((D_MODEL // tn, TOKENS, LANES), F32)]
        epilogue = _gated_residual_prenorm_epilogue
    return _matmul(a_list, a_maps, w, layer, 0, D_MODEL, tm, tn, m_outer, epilogue, args, blocks,
                   out_blocks, out_shapes, name)


UP_TILE = (2048, 1024)
OUT_TILE = (1024, 1024)
DOWN_TILE = (2048, 512)
DOWN_M_OUTER = 2
K_BLOCKS = D_FF // MM_TK


def _out_proj(out_a, out_b, w_out, layer, x, mod, gate_idx, mlp_gain, mlp_scale_idx):
    tm, tn = OUT_TILE
    a_maps = [((tm, a.shape[1]), lambda n, k, r, r0: (r, 0)) for a in (out_a, out_b)]
    return _gated_residual_matmul([out_a, out_b], a_maps, w_out, layer, x, mod, gate_idx, tm, tn, 1,
                                  "out_proj", next_norm=(mlp_gain, mlp_scale_idx))


def _up_proj(xs, ss, w_up, layer, mod, shift_idx):
    tm, tn = UP_TILE
    per_block = MM_TK // tn
    blocks = [((ss.shape[0], tm, LANES), lambda n, k, r, r0: (0, r, 0)),
              ((None, SUBLANES, D_MODEL), lambda n, k, r, r0: (shift_idx, 0, 0))]
    return _matmul(
        [xs], [((tm, D_MODEL), lambda n, k, r, r0: (r, 0))], w_up, layer, 0, D_FF, tm, tn, 1,
        _postnorm_relu2_epilogue, [ss, mod], blocks,
        [((None, tm, tn), lambda n, k, r, r0: (n // per_block, r, n % per_block))],
        [jax.ShapeDtypeStruct((K_BLOCKS, TOKENS, MM_TK), BF16)], "up_proj", shift_idx=1)[0]


def _down_proj(hidden, w_down, layer, x, mod, gate_idx):
    tm, tn = DOWN_TILE
    a_maps = [((None, tm, MM_TK), lambda n, k, r, r0: (k, r, 0))]
    return _gated_residual_matmul([hidden], a_maps, w_down, layer, x, mod, gate_idx, tm, tn,
                                  DOWN_M_OUTER, "down_proj")[0]


DIFF_CHUNK = 512
DIFF_HEADS_PER_STEP = 1
ATTN_LOOKAHEAD = 3
_NT = (((1,), (1,)), ((), ()))
_TN = (((0,), (0,)), ((), ()))


def _diff_attn_kernel(q_ref, k_ref, v_ref, lam_ref, g_ref, o_ref, *, lam_init):
    lane = lax.broadcasted_iota(jnp.int32, (1, LANES), 1)
    lp = lam_ref[...]
    lam = (jnp.exp(jnp.sum(lp[0:1] * lp[1:2], axis=-1, keepdims=True))
           - jnp.exp(jnp.sum(lp[2:3] * lp[3:4], axis=-1, keepdims=True)) + lam_init)
    n_chunks = SEQ // DIFF_CHUNK
    per_head = 2 * n_chunks
    results = {}

    def logits(t):
        head, rem = divmod(t, per_head)
        chunk, second = divmod(rem, 2)
        q = q_ref[head, chunk * DIFF_CHUNK:(chunk + 1) * DIFF_CHUNK, :]
        keep = (lane >= DIFF_QK_DIM) if second else (lane < DIFF_QK_DIM)
        qm = jnp.where(keep, q, jnp.zeros_like(q))
        return lax.dot_general(k_ref[head], qm, _NT, preferred_element_type=F32)

    def softmax_pv(t, st):
        e = jnp.exp2(st - jnp.max(st, axis=0, keepdims=True))
        denom = jnp.sum(e, axis=0, keepdims=True)
        ot = lax.dot_general(v_ref[t // per_head], e.astype(BF16), _TN,
                             preferred_element_type=F32)
        results[t] = (ot, denom)

    _software_pipeline(DIFF_HEADS_PER_STEP * per_head, logits, softmax_pv, ATTN_LOOKAHEAD)
    for head in range(DIFF_HEADS_PER_STEP):
        for chunk in range(n_chunks):
            t = head * per_head + 2 * chunk
            (ot0, l0), (ot1, l1) = results[t], results[t + 1]
            o = (ot0 * (1.0 / l0) - ot1 * (lam / l1)).T
            o = o * lax.rsqrt(jnp.mean(o * o, axis=-1, keepdims=True) + EPS)
            o_ref[chunk * DIFF_CHUNK:(chunk + 1) * DIFF_CHUNK, head * HEAD_DIM:(head + 1) * HEAD_DIM] = (
                o * g_ref[...] * (1.0 - lam_init)).astype(o_ref.dtype)


def _diff_attention(qk_a, rest, diff_lambda, subln, layer):
    lam_init = 0.8 - 0.6 * math.exp(-0.3 * layer)
    hps = DIFF_HEADS_PER_STEP
    return pl.pallas_call(
        functools.partial(_diff_attn_kernel, lam_init=lam_init),
        grid=(BATCH, N_HEADS_DIFF // hps),
        in_specs=[
            pl.BlockSpec((hps, SEQ, LANES), lambda b, h: (h, b, 0)),
            pl.BlockSpec((hps, SEQ, LANES), lambda b, h: (N_HEADS_DIFF // hps + h, b, 0)),
            pl.BlockSpec((hps, SEQ, LANES), lambda b, h: (VA_BLK // hps + h, b, 0)),
            pl.BlockSpec((None, 4, DIFF_QK_DIM), lambda b, h: (layer, 0, 0)),
            pl.BlockSpec((None, 1, HEAD_DIM), lambda b, h: (layer, 0, 0)),
        ],
        out_specs=pl.BlockSpec((SEQ, hps * HEAD_DIM), lambda b, h: (b, h)),
        out_shape=jax.ShapeDtypeStruct((TOKENS, DIFF_W), BF16),
        compiler_params=_params("arbitrary", "arbitrary"),
        name="diff_attn",
    )(qk_a, qk_a, rest, diff_lambda, subln.reshape(DEPTH, 1, HEAD_DIM))


SWA_BAND = 3 * BLOCK
SWA_QB = 16


def _swa_attn_kernel(q_ref, k_ref, v_ref, sink_ref, o_ref):
    step = pl.program_id(2)
    sink = sink_ref[...] * LOG2E

    def band_start(j):
        n = step * SWA_QB + j
        return n, pl.multiple_of(jnp.clip(n - 1, 0, SEQ // BLOCK - 3) * BLOCK, BLOCK)

    def logits(j):
        n, start = band_start(j)
        q = q_ref[:, j * BLOCK:(j + 1) * BLOCK, :].reshape(GQA_RATIO * BLOCK, HEAD_DIM)
        kb = k_ref[pl.ds(start, SWA_BAND), :]
        return lax.dot_general(kb, q, _NT, preferred_element_type=F32)

    def softmax_pv(j, st):
        n, start = band_start(j)
        kpos = start + lax.broadcasted_iota(jnp.int32, (SWA_BAND, BLOCK), 0)
        qpos = n * BLOCK + lax.broadcasted_iota(jnp.int32, (SWA_BAND, BLOCK), 1)
        bias = jnp.where(jnp.abs(kpos - qpos) <= WINDOW, 0.0, -jnp.inf).astype(F32)
        st = st + jnp.concatenate([bias] * GQA_RATIO, axis=1)
        mx = jnp.maximum(jnp.max(st, axis=0, keepdims=True), sink)
        e = jnp.exp2(st - mx)
        denom = jnp.sum(e, axis=0, keepdims=True) + jnp.exp2(sink - mx)
        vb = v_ref[pl.ds(start, SWA_BAND), :]
        ot = lax.dot_general(vb, e.astype(BF16), _TN, preferred_element_type=F32)
        ot = ot * (1.0 / denom)
        for r in range(GQA_RATIO):
            o_ref[j * BLOCK:(j + 1) * BLOCK, r * HEAD_DIM:(r + 1) * HEAD_DIM] = (
                ot[:, r * BLOCK:(r + 1) * BLOCK].T.astype(o_ref.dtype))

    _software_pipeline(SWA_QB, logits, softmax_pv, ATTN_LOOKAHEAD)


def _swa_attention(rest, sink):
    steps = SEQ // BLOCK // SWA_QB
    rows = SWA_QB * BLOCK
    sink_rows = jnp.repeat(sink, BLOCK).reshape(N_KV_SWA, 1, GQA_RATIO * BLOCK)
    return pl.pallas_call(
        _swa_attn_kernel,
        grid=(BATCH, N_KV_SWA, steps),
        in_specs=[
            pl.BlockSpec((GQA_RATIO, rows, LANES),
                         lambda b, g, n: (QB_BLK // GQA_RATIO + g, b * steps + n, 0)),
            pl.BlockSpec((None, SEQ, LANES), lambda b, g, n: (KB_BLK + g, b, 0)),
            pl.BlockSpec((None, SEQ, LANES), lambda b, g, n: (VB_BLK + g, b, 0)),
            pl.BlockSpec((None, 1, GQA_RATIO * BLOCK), lambda b, g, n: (g, 0, 0)),
        ],
        out_specs=pl.BlockSpec((rows, GQA_RATIO * HEAD_DIM), lambda b, g, n: (b * steps + n, g)),
        out_shape=jax.ShapeDtypeStruct((TOKENS, SWA_Q_W), BF16),
        compiler_params=_params("arbitrary", "arbitrary", "arbitrary"),
        name="swa_attn",
    )(rest, rest, rest, sink_rows)


def kernel(x, c, positions, ada_w, ada_b, norm_mix, w_in, diff_lambda, diff_subln, swa_sink,
           w_out, norm_mlp, w_up, w_down, final_norm):
    mod = _modulation(c, ada_w, ada_b)
    tables = _rope_tables(positions)
    xr = x.reshape(TOKENS, D_MODEL)
    for layer in range(DEPTH):
        base = layer * N_MOD
        h = _norm(xr, norm_mix[layer], mod, base + 1, base + 0)
        qk_a, rest = _in_proj(h, w_in, layer, tables)
        out_a = _diff_attention(qk_a, rest, diff_lambda, diff_subln, layer)
        out_b = _swa_attention(rest, swa_sink[layer])
        xr, xs, ss = _out_proj(out_a, out_b, w_out, layer, xr, mod, base + 2,
                               norm_mlp[layer].reshape(1, D_MODEL), base + 4)
        hidden = _up_proj(xs, ss, w_up, layer, mod, base + 3)
        xr = _down_proj(hidden, w_down, layer, xr, mod, base + 5)
    out = _norm(xr, final_norm, out_dtype=F32)
    return out.reshape(BATCH, SEQ, D_MODEL)
```

```python
import functools
import math

import jax
import jax.numpy as jnp
from jax import lax
from jax.experimental import pallas as pl
from jax.experimental.pallas import tpu as pltpu

D_MODEL = 2048
BATCH = 4
SEQ = 2048
DEPTH = 2
HEAD_DIM = 128
N_HEADS_DIFF = 8
DIFF_QK_DIM = 64
N_HEADS_SWA = 8
N_KV_SWA = 2
GQA_RATIO = N_HEADS_SWA // N_KV_SWA
WINDOW = 128
BLOCK = 128
D_FF = 4 * D_MODEL
ROPE_THETA = 10000.0
EPS = 1e-6
N_MOD = 6
DIFF_W = N_HEADS_DIFF * HEAD_DIM
SWA_Q_W = N_HEADS_SWA * HEAD_DIM
SWA_KV_W = N_KV_SWA * HEAD_DIM
IN_WIDTH = 3 * DIFF_W + SWA_Q_W + 2 * SWA_KV_W
TOKENS = BATCH * SEQ

LANES = 128
SUBLANES = 8
VMEM_LIMIT = 56 * 1024 * 1024

F32 = jnp.float32
BF16 = jnp.bfloat16
LOG2E = math.log2(math.e)


def _params(*sem):
    return pltpu.CompilerParams(dimension_semantics=sem, vmem_limit_bytes=VMEM_LIMIT)


def _software_pipeline(n_items, produce, consume, lookahead):
    pending = {}
    for t in range(n_items + lookahead):
        if t < n_items:
            pending[t] = produce(t)
        if t >= lookahead:
            consume(t - lookahead, pending.pop(t - lookahead))


MOD_TN = 1024


def _mod_kernel(c_ref, w_ref, b_ref, o_ref):
    c = c_ref[...]
    c_act = (c * jax.nn.sigmoid(c)).astype(BF16)
    o_ref[...] = jnp.dot(c_act, w_ref[...].astype(BF16), preferred_element_type=F32) + b_ref[...]


def _modulation(c, ada_w, ada_b):
    c_pad = jnp.pad(c, ((0, SUBLANES - BATCH), (0, 0)))
    per_chunk = D_MODEL // MOD_TN
    return pl.pallas_call(
        _mod_kernel,
        grid=(DEPTH, N_MOD * per_chunk),
        in_specs=[
            pl.BlockSpec((SUBLANES, D_MODEL), lambda l, j: (0, 0)),
            pl.BlockSpec((None, D_MODEL, MOD_TN), lambda l, j: (l, 0, j)),
            pl.BlockSpec((None, 1, MOD_TN), lambda l, j: (l, 0, j)),
        ],
        out_specs=pl.BlockSpec((None, SUBLANES, MOD_TN),
                               lambda l, j: (l * N_MOD + j // per_chunk, 0, j % per_chunk)),
        out_shape=jax.ShapeDtypeStruct((DEPTH * N_MOD, SUBLANES, D_MODEL), F32),
        compiler_params=_params("arbitrary", "arbitrary"),
        name="adaln_mod",
    )(c_pad, ada_w, ada_b.reshape(DEPTH, 1, N_MOD * D_MODEL))


ROPE_TM = 1024


def _rope_kernel(pos_ref, inv_a_ref, inv_b_ref, ca_ref, sa_ref, cb_ref, sb_ref):
    pos = pos_ref[...].astype(F32)
    lane = lax.broadcasted_iota(jnp.int32, (1, LANES), 1)
    ang_a = pos * inv_a_ref[...]
    ang_b = pos * inv_b_ref[...]
    sign_a = jnp.where((lane & (DIFF_QK_DIM // 2)) == 0, -1.0, 1.0).astype(F32)
    sign_b = jnp.where((lane & (HEAD_DIM // 2)) == 0, -1.0, 1.0).astype(F32)
    ca_ref[...] = jnp.cos(ang_a)
    sa_ref[...] = jnp.sin(ang_a) * sign_a
    cb_ref[...] = jnp.cos(ang_b)
    sb_ref[...] = jnp.sin(ang_b) * sign_b


def _rope_tables(positions):
    def inv_freq(dim):
        return ROPE_THETA ** (-jnp.arange(0, dim, 2, dtype=F32) / dim)
    inv_a = jnp.tile(inv_freq(DIFF_QK_DIM), LANES // (DIFF_QK_DIM // 2)).reshape(1, LANES)
    inv_b = jnp.tile(inv_freq(HEAD_DIM), LANES // (HEAD_DIM // 2)).reshape(1, LANES)
    tab = jax.ShapeDtypeStruct((TOKENS, LANES), F32)
    row = pl.BlockSpec((ROPE_TM, LANES), lambda m: (m, 0))
    const = pl.BlockSpec((1, LANES), lambda m: (0, 0))
    return pl.pallas_call(
        _rope_kernel,
        grid=(TOKENS // ROPE_TM,),
        in_specs=[pl.BlockSpec((ROPE_TM, 1), lambda m: (m, 0)), const, const],
        out_specs=[row, row, row, row],
        out_shape=[tab, tab, tab, tab],
        compiler_params=_params("arbitrary"),
        name="rope_tables",
    )(positions.reshape(TOKENS, 1), inv_a, inv_b)


NORM_TM = 1024


def _norm_kernel(x_ref, g_ref, *rest, modulated):
    x = x_ref[...]
    y = x * lax.rsqrt(jnp.mean(x * x, axis=-1, keepdims=True) + EPS)
    y = y * g_ref[...]
    if modulated:
        sc_ref, sh_ref, o_ref = rest
        b = (pl.program_id(0) * NORM_TM) // SEQ
        y = y * (1.0 + sc_ref[pl.ds(b, 1), :]) + sh_ref[pl.ds(b, 1), :]
    else:
        (o_ref,) = rest
    o_ref[...] = y.astype(o_ref.dtype)


def _norm(x, gain, mod=None, sc_idx=None, sh_idx=None, out_dtype=BF16):
    in_specs = [pl.BlockSpec((NORM_TM, D_MODEL), lambda m: (m, 0)),
                pl.BlockSpec((1, D_MODEL), lambda m: (0, 0))]
    args = [x, gain.reshape(1, D_MODEL)]
    if mod is not None:
        in_specs += [pl.BlockSpec((None, SUBLANES, D_MODEL), lambda m: (sc_idx, 0, 0)),
                     pl.BlockSpec((None, SUBLANES, D_MODEL), lambda m: (sh_idx, 0, 0))]
        args += [mod, mod]
    return pl.pallas_call(
        functools.partial(_norm_kernel, modulated=mod is not None),
        grid=(TOKENS // NORM_TM,),
        in_specs=in_specs,
        out_specs=pl.BlockSpec((NORM_TM, D_MODEL), lambda m: (m, 0)),
        out_shape=jax.ShapeDtypeStruct((TOKENS, D_MODEL), out_dtype),
        compiler_params=_params("arbitrary"),
        name="rmsnorm_mod" if mod is not None else "rmsnorm_final",
    )(*args)


MM_TK = 2048
MM_SUB = 512


def _swap_halves(x, half):
    if 2 * half == LANES:
        return pltpu.roll(x, half, 1)
    lane = lax.broadcasted_iota(jnp.int32, (1, LANES), 1)
    return jnp.where((lane & half) == 0,
                     pltpu.roll(x, LANES - half, 1), pltpu.roll(x, half, 1))


def _head_epilogue(acc, rows, row0, n, refs, outs, bias, *, half, is_rope, q_scale):
    cos_ref, sin_ref = refs
    (o_ref,) = outs
    for j in range(o_ref.shape[0]):
        x = acc[:, j * LANES:(j + 1) * LANES]
        rotate = is_rope(n, j)
        if rotate is not False:
            cos = cos_ref[rows, :]
            sin_signed = sin_ref[rows, :]
            if rotate is not True:
                cos = jnp.where(rotate, cos, 1.0)
                sin_signed = jnp.where(rotate, sin_signed, 0.0)
            x = (x * cos + _swap_halves(x, half) * sin_signed) * q_scale(n, j)
        o_ref[j, rows, :] = x.astype(o_ref.dtype)


def _gated_residual_epilogue(acc, rows, row0, n, refs, outs, bias):
    x_ref, g_ref = refs
    outs[0][rows, :] = x_ref[rows, :] + g_ref[pl.ds(row0 // SEQ, 1), :] * acc


def _gated_residual_prenorm_epilogue(acc, rows, row0, n, refs, outs, bias):
    x_ref, g_ref, gain_ref, sc_ref = refs
    x_out, xs_out, ss_out = outs
    b = row0 // SEQ
    xn = x_ref[rows, :] + g_ref[pl.ds(b, 1), :] * acc
    x_out[rows, :] = xn
    xs_out[rows, :] = (xn * (gain_ref[...] * (1.0 + sc_ref[pl.ds(b, 1), :]))).astype(xs_out.dtype)
    ss_out[rows, :] = jnp.broadcast_to(jnp.sum(xn * xn, axis=-1, keepdims=True),
                                       (rows.stop - rows.start, LANES))


def _relu2_epilogue(acc, rows, row0, n, refs, outs, bias):
    r = jnp.maximum(acc, 0.0)
    outs[0][rows, :] = (r * r).astype(outs[0].dtype)


def _postnorm_relu2_epilogue(acc, rows, row0, n, refs, outs, bias):
    ss_ref = refs[0]
    ss = ss_ref[0, rows, :]
    for i in range(1, ss_ref.shape[0]):
        ss = ss + ss_ref[i, rows, :]
    inv = lax.rsqrt(ss * (1.0 / D_MODEL) + EPS)
    inv = jnp.concatenate([inv] * (acc.shape[1] // LANES), axis=1)
    r = jnp.maximum(acc * inv + bias[pl.ds(row0 // SEQ, 1), :], 0.0)
    outs[0][rows, :] = (r * r).astype(outs[0].dtype)


def _matmul_kernel(*refs, n_a, n_extra, n_out, nk, tm, m_inner, epilogue, shift_idx):
    a_refs = refs[:n_a]
    w_ref = refs[n_a]
    extra = refs[n_a + 1:n_a + 1 + n_extra]
    outs = refs[n_a + 1 + n_extra:n_a + 1 + n_extra + n_out]
    scratch = refs[n_a + 1 + n_extra + n_out:]
    wb_ref = scratch[0]
    bias_ref = scratch[-1] if shift_idx is not None else None
    outer, n, k, m = (pl.program_id(i) for i in range(4))

    acc_ref = scratch[1] if nk > 1 else None

    @pl.when(m == 0)
    def _():
        wb_ref[...] = w_ref[...].astype(BF16)
        if shift_idx is not None:
            bias_ref[...] = jnp.dot(extra[shift_idx][...].astype(BF16), wb_ref[...],
                                    preferred_element_type=F32)

    def product(t):
        rows = slice(t * MM_SUB, (t + 1) * MM_SUB)
        part = None
        off = 0
        for a_ref in a_refs:
            ka = a_ref.shape[1]
            d = jnp.dot(a_ref[rows, :], wb_ref[off:off + ka, :], preferred_element_type=F32)
            part = d if part is None else part + d
            off += ka
        return part

    def finish(k_pass, t, part):
        rows = slice(t * MM_SUB, (t + 1) * MM_SUB)
        local_row0 = m * tm + t * MM_SUB
        arows = pl.ds(pl.multiple_of(local_row0, MM_SUB), MM_SUB)
        if k_pass == "first":
            acc_ref[arows, :] = part
        elif k_pass == "middle":
            acc_ref[arows, :] += part
        else:
            if k_pass == "last":
                part = acc_ref[arows, :] + part
            epilogue(part, rows, outer * (m_inner * tm) + local_row0, n, extra, outs, bias_ref)

    def run(k_pass):
        _software_pipeline(tm // MM_SUB, product, functools.partial(finish, k_pass), lookahead=1)

    if nk == 1:
        run("only")
    else:
        pl.when(k == 0)(functools.partial(run, "first"))
        if nk > 2:
            pl.when(jnp.logical_and(k > 0, k < nk - 1))(functools.partial(run, "middle"))
        pl.when(k == nk - 1)(functools.partial(run, "last"))


def _block(shape, index_map, m_inner):
    return pl.BlockSpec(
        shape, lambda o, n, k, m: index_map(n, k, o * m_inner + m, o * m_inner))


def _matmul(a_list, a_maps, w, layer, col0, n_cols, tm, tn, m_outer, epilogue, extra_args,
            extra_blocks, out_blocks, out_shapes, name, shift_idx=None):
    k_total = w.shape[1]
    assert col0 % tn == 0 and n_cols % tn == 0 and tm % MM_SUB == 0
    assert TOKENS % (tm * m_outer) == 0
    n_off = col0 // tn
    nk = k_total // MM_TK if len(a_list) == 1 else 1
    tk = k_total // nk
    m_inner = TOKENS // tm // m_outer
    grid = (m_outer, n_cols // tn, nk, m_inner)
    a_specs = [_block(shape, imap, m_inner) for shape, imap in a_maps]
    w_spec = pl.BlockSpec((None, tk, tn), lambda o, n, k, m: (layer, k, n + n_off))
    scratch = [pltpu.VMEM((tk, tn), BF16)]
    if nk > 1:
        scratch.append(pltpu.VMEM((m_inner * tm, tn), F32))
    if shift_idx is not None:
        assert nk == 1
        scratch.append(pltpu.VMEM((SUBLANES, tn), F32))
    return pl.pallas_call(
        functools.partial(_matmul_kernel, n_a=len(a_list), n_extra=len(extra_args),
                          n_out=len(out_blocks), nk=nk, tm=tm, m_inner=m_inner, epilogue=epilogue,
                          shift_idx=shift_idx),
        grid=grid,
        in_specs=a_specs + [w_spec] + [_block(s, f, m_inner) for s, f in extra_blocks],
        out_specs=[_block(s, f, m_inner) for s, f in out_blocks],
        out_shape=out_shapes,
        scratch_shapes=scratch,
        compiler_params=_params("arbitrary", "arbitrary", "arbitrary", "arbitrary"),
        name=name,
    )(*a_list, w, *extra_args)


def _head_proj(h, w_in, layer, col0, n_cols, tm, tn, tables, half, is_rope, q_scale, name):
    table_blocks = [((tm, LANES), lambda n, k, r, r0: (r, 0)) for _ in tables]
    epilogue = functools.partial(_head_epilogue, half=half, is_rope=is_rope, q_scale=q_scale)
    return _matmul(
        [h], [((tm, D_MODEL), lambda n, k, r, r0: (r, 0))], w_in, layer, col0, n_cols, tm, tn, 1,
        epilogue, list(tables), table_blocks,
        [((tn // LANES, tm, LANES), lambda n, k, r, r0: (n, r, 0))],
        [jax.ShapeDtypeStruct((n_cols // LANES, TOKENS, LANES), BF16)], name)[0]


DIFF_IN_TILE = (2048, DIFF_W)
SWA_IN_WIDTH = SWA_Q_W + 2 * SWA_KV_W
SWA_IN_TILE = (1024, SWA_IN_WIDTH)
KA_BLK = DIFF_W // LANES
VA_BLK = 2 * DIFF_W // LANES
QB_BLK = 0
KB_BLK = SWA_Q_W // LANES
VB_BLK = KB_BLK + SWA_KV_W // LANES


def _in_proj(h, w_in, layer, tables):
    ca, sa, cb, sb = tables
    diff = _head_proj(
        h, w_in, layer, 0, 3 * DIFF_W, *DIFF_IN_TILE, (ca, sa), DIFF_QK_DIM // 2,
        lambda n, j: n < 2,
        lambda n, j: jnp.where(n == 0, DIFF_QK_DIM ** -0.5 * LOG2E, 1.0), "in_proj_diff")
    swa = _head_proj(
        h, w_in, layer, 3 * DIFF_W, SWA_IN_WIDTH, *SWA_IN_TILE, (cb, sb), HEAD_DIM // 2,
        lambda n, j: j < VB_BLK,
        lambda n, j: HEAD_DIM ** -0.5 * LOG2E if j < KB_BLK else 1.0, "in_proj_swa")
    return diff, swa


def _gated_residual_matmul(a_list, a_maps, w, layer, x, mod, gate_idx, tm, tn, m_outer, name,
                           next_norm=None):
    last = (w.shape[1] // MM_TK if len(a_list) == 1 else 1) - 1

    def xo_map(n, k, r, r0):
        return (jnp.where(k == last, r, r0), n)

    def mod_row(idx):
        return ((None, SUBLANES, tn), lambda n, k, r, r0: (idx, 0, n))

    args = [x, mod]
    blocks = [((tm, tn), xo_map), mod_row(gate_idx)]
    out_blocks = [((tm, tn), xo_map)]
    out_shapes = [jax.ShapeDtypeStruct((TOKENS, D_MODEL), F32)]
    epilogue = _gated_residual_epilogue
    if next_norm is not None:
        assert last == 0
        gain, scale_idx = next_norm
        args += [gain, mod]
        blocks += [((1, tn), lambda n, k, r, r0: (0, n)), mod_row(scale_idx)]
        out_blocks += [((tm, tn), xo_map), ((None, tm, LANES), lambda n, k, r, r0: (n, r, 0))]
        out_shapes += [jax.ShapeDtypeStruct((TOKENS, D_MODEL), BF16),
                       jax.ShapeDtypeStruct((D_MODEL // tn, TOKENS, LANES), F32)]
        epilogue = _gated_residual_prenorm_epilogue
    return _matmul(a_list, a_maps, w, layer, 0, D_MODEL, tm, tn, m_outer, epilogue, args, blocks,
                   out_blocks, out_shapes, name)


UP_TILE = (2048, 1024)
OUT_TILE = (1024, 1024)
DOWN_TILE = (2048, 512)
DOWN_M_OUTER = 2
K_BLOCKS = D_FF // MM_TK


def _out_proj(out_a, out_b, w_out, layer, x, mod, gate_idx, mlp_gain, mlp_scale_idx):
    tm, tn = OUT_TILE
    a_maps = [((tm, a.shape[1]), lambda n, k, r, r0: (r, 0)) for a in (out_a, out_b)]
    return _gated_residual_matmul([out_a, out_b], a_maps, w_out, layer, x, mod, gate_idx, tm, tn, 1,
                                  "out_proj", next_norm=(mlp_gain, mlp_scale_idx))


def _up_proj(xs, ss, w_up, layer, mod, shift_idx):
    tm, tn = UP_TILE
    per_block = MM_TK // tn
    blocks = [((ss.shape[0], tm, LANES), lambda n, k, r, r0: (0, r, 0)),
              ((None, SUBLANES, D_MODEL), lambda n, k, r, r0: (shift_idx, 0, 0))]
    return _matmul(
        [xs], [((tm, D_MODEL), lambda n, k, r, r0: (r, 0))], w_up, layer, 0, D_FF, tm, tn, 1,
        _postnorm_relu2_epilogue, [ss, mod], blocks,
        [((None, tm, tn), lambda n, k, r, r0: (n // per_block, r, n % per_block))],
        [jax.ShapeDtypeStruct((K_BLOCKS, TOKENS, MM_TK), BF16)], "up_proj", shift_idx=1)[0]


def _down_proj(hidden, w_down, layer, x, mod, gate_idx):
    tm, tn = DOWN_TILE
    a_maps = [((None, tm, MM_TK), lambda n, k, r, r0: (k, r, 0))]
    return _gated_residual_matmul([hidden], a_maps, w_down, layer, x, mod, gate_idx, tm, tn,
                                  DOWN_M_OUTER, "down_proj")[0]


DIFF_CHUNK = 512
DIFF_HEADS_PER_STEP = 1
ATTN_LOOKAHEAD = 3
_NT = (((1,), (1,)), ((), ()))
_TN = (((0,), (0,)), ((), ()))


def _diff_attn_kernel(q_ref, k_ref, v_ref, lam_ref, g_ref, o_ref, *, lam_init):
    lane = lax.broadcasted_iota(jnp.int32, (1, LANES), 1)
    lp = lam_ref[...]
    lam = (jnp.exp(jnp.sum(lp[0:1] * lp[1:2], axis=-1, keepdims=True))
           - jnp.exp(jnp.sum(lp[2:3] * lp[3:4], axis=-1, keepdims=True)) + lam_init)
    n_chunks = SEQ // DIFF_CHUNK
    per_head = 2 * n_chunks
    results = {}

    def logits(t):
        head, rem = divmod(t, per_head)
        chunk, second = divmod(rem, 2)
        q = q_ref[head, chunk * DIFF_CHUNK:(chunk + 1) * DIFF_CHUNK, :]
        keep = (lane >= DIFF_QK_DIM) if second else (lane < DIFF_QK_DIM)
        qm = jnp.where(keep, q, jnp.zeros_like(q))
        return lax.dot_general(k_ref[head], qm, _NT, preferred_element_type=F32)

    def softmax_pv(t, st):
        e = jnp.exp2(st - jnp.max(st, axis=0, keepdims=True))
        denom = jnp.sum(e, axis=0, keepdims=True)
        ot = lax.dot_general(v_ref[t // per_head], e.astype(BF16), _TN,
                             preferred_element_type=F32)
        results[t] = (ot, denom)

    _software_pipeline(DIFF_HEADS_PER_STEP * per_head, logits, softmax_pv, ATTN_LOOKAHEAD)
    for head in range(DIFF_HEADS_PER_STEP):
        for chunk in range(n_chunks):
            t = head * per_head + 2 * chunk
            (ot0, l0), (ot1, l1) = results[t], results[t + 1]
            o = (ot0 * (1.0 / l0) - ot1 * (lam / l1)).T
            o = o * lax.rsqrt(jnp.mean(o * o, axis=-1, keepdims=True) + EPS)
            o_ref[chunk * DIFF_CHUNK:(chunk + 1) * DIFF_CHUNK, head * HEAD_DIM:(head + 1) * HEAD_DIM] = (
                o * g_ref[...] * (1.0 - lam_init)).astype(o_ref.dtype)


def _diff_attention(qkv, diff_lambda, subln, layer):
    lam_init = 0.8 - 0.6 * math.exp(-0.3 * layer)
    hps = DIFF_HEADS_PER_STEP
    return pl.pallas_call(
        functools.partial(_diff_attn_kernel, lam_init=lam_init),
        grid=(BATCH, N_HEADS_DIFF // hps),
        in_specs=[
            pl.BlockSpec((hps, SEQ, LANES), lambda b, h: (h, b, 0)),
            pl.BlockSpec((hps, SEQ, LANES), lambda b, h: (KA_BLK // hps + h, b, 0)),
            pl.BlockSpec((hps, SEQ, LANES), lambda b, h: (VA_BLK // hps + h, b, 0)),
            pl.BlockSpec((None, 4, DIFF_QK_DIM), lambda b, h: (layer, 0, 0)),
            pl.BlockSpec((None, 1, HEAD_DIM), lambda b, h: (layer, 0, 0)),
        ],
        out_specs=pl.BlockSpec((SEQ, hps * HEAD_DIM), lambda b, h: (b, h)),
        out_shape=jax.ShapeDtypeStruct((TOKENS, DIFF_W), BF16),
        compiler_params=_params("arbitrary", "arbitrary"),
        name="diff_attn",
    )(qkv, qkv, qkv, diff_lambda, subln.reshape(DEPTH, 1, HEAD_DIM))


SWA_BAND = 3 * BLOCK
SWA_QB = 16


def _swa_attn_kernel(q_ref, k_ref, v_ref, sink_ref, o_ref):
    step = pl.program_id(2)
    sink = sink_ref[...] * LOG2E

    def band_start(j):
        n = step * SWA_QB + j
        return n, pl.multiple_of(jnp.clip(n - 1, 0, SEQ // BLOCK - 3) * BLOCK, BLOCK)

    def logits(j):
        n, start = band_start(j)
        q = q_ref[:, j * BLOCK:(j + 1) * BLOCK, :].reshape(GQA_RATIO * BLOCK, HEAD_DIM)
        kb = k_ref[pl.ds(start, SWA_BAND), :]
        return lax.dot_general(kb, q, _NT, preferred_element_type=F32)

    def softmax_pv(j, st):
        n, start = band_start(j)
        kpos = start + lax.broadcasted_iota(jnp.int32, (SWA_BAND, BLOCK), 0)
        qpos = n * BLOCK + lax.broadcasted_iota(jnp.int32, (SWA_BAND, BLOCK), 1)
        bias = jnp.where(jnp.abs(kpos - qpos) <= WINDOW, 0.0, -jnp.inf).astype(F32)
        st = st + jnp.concatenate([bias] * GQA_RATIO, axis=1)
        mx = jnp.maximum(jnp.max(st, axis=0, keepdims=True), sink)
        e = jnp.exp2(st - mx)
        denom = jnp.sum(e, axis=0, keepdims=True) + jnp.exp2(sink - mx)
        vb = v_ref[pl.ds(start, SWA_BAND), :]
        ot = lax.dot_general(vb, e.astype(BF16), _TN, preferred_element_type=F32)
        ot = ot * (1.0 / denom)
        for r in range(GQA_RATIO):
            o_ref[j * BLOCK:(j + 1) * BLOCK, r * HEAD_DIM:(r + 1) * HEAD_DIM] = (
                ot[:, r * BLOCK:(r + 1) * BLOCK].T.astype(o_ref.dtype))

    _software_pipeline(SWA_QB, logits, softmax_pv, ATTN_LOOKAHEAD)


def _swa_attention(qkv, sink):
    steps = SEQ // BLOCK // SWA_QB
    rows = SWA_QB * BLOCK
    sink_rows = jnp.repeat(sink, BLOCK).reshape(N_KV_SWA, 1, GQA_RATIO * BLOCK)
    return pl.pallas_call(
        _swa_attn_kernel,
        grid=(BATCH, N_KV_SWA, steps),
        in_specs=[
            pl.BlockSpec((GQA_RATIO, rows, LANES),
                         lambda b, g, n: (QB_BLK // GQA_RATIO + g, b * steps + n, 0)),
            pl.BlockSpec((None, SEQ, LANES), lambda b, g, n: (KB_BLK + g, b, 0)),
            pl.BlockSpec((None, SEQ, LANES), lambda b, g, n: (VB_BLK + g, b, 0)),
            pl.BlockSpec((None, 1, GQA_RATIO * BLOCK), lambda b, g, n: (g, 0, 0)),
        ],
        out_specs=pl.BlockSpec((rows, GQA_RATIO * HEAD_DIM), lambda b, g, n: (b * steps + n, g)),
        out_shape=jax.ShapeDtypeStruct((TOKENS, SWA_Q_W), BF16),
        compiler_params=_params("arbitrary", "arbitrary", "arbitrary"),
        name="swa_attn",
    )(qkv, qkv, qkv, sink_rows)


def kernel(x, c, positions, ada_w, ada_b, norm_mix, w_in, diff_lambda, diff_subln, swa_sink,
           w_out, norm_mlp, w_up, w_down, final_norm):
    mod = _modulation(c, ada_w, ada_b)
    tables = _rope_tables(positions)
    xr = x.reshape(TOKENS, D_MODEL)
    for layer in range(DEPTH):
        base = layer * N_MOD
        h = _norm(xr, norm_mix[layer], mod, base + 1, base + 0)
        qkv_diff, qkv_swa = _in_proj(h, w_in, layer, tables)
        out_a = _diff_attention(qkv_diff, diff_lambda, diff_subln, layer)
        out_b = _swa_attention(qkv_swa, swa_sink[layer])
        xr, xs, ss = _out_proj(out_a, out_b, w_out, layer, xr, mod, base + 2,
                               norm_mlp[layer].reshape(1, D_MODEL), base + 4)
        hidden = _up_proj(xs, ss, w_up, layer, mod, base + 3)
        xr = _down_proj(hidden, w_down, layer, xr, mod, base + 5)
    out = _norm(xr, final_norm, out_dtype=F32)
    return out.reshape(BATCH, SEQ, D_MODEL)
```

```python
import functools
import math

import jax
import jax.numpy as jnp
from jax import lax
from jax.experimental import pallas as pl
from jax.experimental.pallas import tpu as pltpu

D_MODEL = 2048
BATCH = 4
SEQ = 2048
DEPTH = 2
HEAD_DIM = 128
N_HEADS_DIFF = 8
DIFF_QK_DIM = 64
N_HEADS_SWA = 8
N_KV_SWA = 2
GQA_RATIO = N_HEADS_SWA // N_KV_SWA
WINDOW = 128
BLOCK = 128
D_FF = 4 * D_MODEL
ROPE_THETA = 10000.0
EPS = 1e-6
N_MOD = 6
DIFF_W = N_HEADS_DIFF * HEAD_DIM
SWA_Q_W = N_HEADS_SWA * HEAD_DIM
SWA_KV_W = N_KV_SWA * HEAD_DIM
IN_WIDTH = 3 * DIFF_W + SWA_Q_W + 2 * SWA_KV_W
TOKENS = BATCH * SEQ

LANES = 128
SUBLANES = 8
VMEM_LIMIT = 56 * 1024 * 1024

F32 = jnp.float32
BF16 = jnp.bfloat16
LOG2E = math.log2(math.e)


def _params(*sem):
    return pltpu.CompilerParams(dimension_semantics=sem, vmem_limit_bytes=VMEM_LIMIT)


def _software_pipeline(n_items, produce, consume, lookahead):
    pending = {}
    for t in range(n_items + lookahead):
        if t < n_items:
            pending[t] = produce(t)
        if t >= lookahead:
            consume(t - lookahead, pending.pop(t - lookahead))


MOD_TN = 1024


def _mod_kernel(c_ref, w_ref, b_ref, o_ref):
    c = c_ref[...]
    c_act = (c * jax.nn.sigmoid(c)).astype(BF16)
    o_ref[...] = jnp.dot(c_act, w_ref[...].astype(BF16), preferred_element_type=F32) + b_ref[...]


def _modulation(c, ada_w, ada_b):
    c_pad = jnp.pad(c, ((0, SUBLANES - BATCH), (0, 0)))
    per_chunk = D_MODEL // MOD_TN
    return pl.pallas_call(
        _mod_kernel,
        grid=(DEPTH, N_MOD * per_chunk),
        in_specs=[
            pl.BlockSpec((SUBLANES, D_MODEL), lambda l, j: (0, 0)),
            pl.BlockSpec((None, D_MODEL, MOD_TN), lambda l, j: (l, 0, j)),
            pl.BlockSpec((None, 1, MOD_TN), lambda l, j: (l, 0, j)),
        ],
        out_specs=pl.BlockSpec((None, SUBLANES, MOD_TN),
                               lambda l, j: (l * N_MOD + j // per_chunk, 0, j % per_chunk)),
        out_shape=jax.ShapeDtypeStruct((DEPTH * N_MOD, SUBLANES, D_MODEL), F32),
        compiler_params=_params("arbitrary", "arbitrary"),
        name="adaln_mod",
    )(c_pad, ada_w, ada_b.reshape(DEPTH, 1, N_MOD * D_MODEL))


ROPE_TM = 1024


def _rope_kernel(pos_ref, inv_a_ref, inv_b_ref, ca_ref, sa_ref, cb_ref, sb_ref):
    pos = pos_ref[...].astype(F32)
    lane = lax.broadcasted_iota(jnp.int32, (1, LANES), 1)
    ang_a = pos * inv_a_ref[...]
    ang_b = pos * inv_b_ref[...]
    sign_a = jnp.where((lane & (DIFF_QK_DIM // 2)) == 0, -1.0, 1.0).astype(F32)
    sign_b = jnp.where((lane & (HEAD_DIM // 2)) == 0, -1.0, 1.0).astype(F32)
    ca_ref[...] = jnp.cos(ang_a)
    sa_ref[...] = jnp.sin(ang_a) * sign_a
    cb_ref[...] = jnp.cos(ang_b)
    sb_ref[...] = jnp.sin(ang_b) * sign_b


def _rope_tables(positions):
    def inv_freq(dim):
        return ROPE_THETA ** (-jnp.arange(0, dim, 2, dtype=F32) / dim)
    inv_a = jnp.tile(inv_freq(DIFF_QK_DIM), LANES // (DIFF_QK_DIM // 2)).reshape(1, LANES)
    inv_b = jnp.tile(inv_freq(HEAD_DIM), LANES // (HEAD_DIM // 2)).reshape(1, LANES)
    tab = jax.ShapeDtypeStruct((TOKENS, LANES), F32)
    row = pl.BlockSpec((ROPE_TM, LANES), lambda m: (m, 0))
    const = pl.BlockSpec((1, LANES), lambda m: (0, 0))
    return pl.pallas_call(
        _rope_kernel,
        grid=(TOKENS // ROPE_TM,),
        in_specs=[pl.BlockSpec((ROPE_TM, 1), lambda m: (m, 0)), const, const],
        out_specs=[row, row, row, row],
        out_shape=[tab, tab, tab, tab],
        compiler_params=_params("arbitrary"),
        name="rope_tables",
    )(positions.reshape(TOKENS, 1), inv_a, inv_b)


NORM_TM = 1024


def _norm_kernel(x_ref, g_ref, *rest, modulated):
    x = x_ref[...]
    y = x * lax.rsqrt(jnp.mean(x * x, axis=-1, keepdims=True) + EPS)
    y = y * g_ref[...]
    if modulated:
        sc_ref, sh_ref, o_ref = rest
        b = (pl.program_id(0) * NORM_TM) // SEQ
        y = y * (1.0 + sc_ref[pl.ds(b, 1), :]) + sh_ref[pl.ds(b, 1), :]
    else:
        (o_ref,) = rest
    o_ref[...] = y.astype(o_ref.dtype)


def _norm(x, gain, mod=None, sc_idx=None, sh_idx=None, out_dtype=BF16):
    in_specs = [pl.BlockSpec((NORM_TM, D_MODEL), lambda m: (m, 0)),
                pl.BlockSpec((1, D_MODEL), lambda m: (0, 0))]
    args = [x, gain.reshape(1, D_MODEL)]
    if mod is not None:
        in_specs += [pl.BlockSpec((None, SUBLANES, D_MODEL), lambda m: (sc_idx, 0, 0)),
                     pl.BlockSpec((None, SUBLANES, D_MODEL), lambda m: (sh_idx, 0, 0))]
        args += [mod, mod]
    return pl.pallas_call(
        functools.partial(_norm_kernel, modulated=mod is not None),
        grid=(TOKENS // NORM_TM,),
        in_specs=in_specs,
        out_specs=pl.BlockSpec((NORM_TM, D_MODEL), lambda m: (m, 0)),
        out_shape=jax.ShapeDtypeStruct((TOKENS, D_MODEL), out_dtype),
        compiler_params=_params("arbitrary"),
        name="rmsnorm_mod" if mod is not None else "rmsnorm_final",
    )(*args)


MM_SUB = 512


def _swap_halves(x, half):
    if 2 * half == LANES:
        return pltpu.roll(x, half, 1)
    lane = lax.broadcasted_iota(jnp.int32, (1, LANES), 1)
    return jnp.where((lane & half) == 0,
                     pltpu.roll(x, LANES - half, 1), pltpu.roll(x, half, 1))


def _head_epilogue(acc, rows, row0, n, refs, outs, bias, *, half, is_rope, q_scale):
    cos_ref, sin_ref = refs
    (o_ref,) = outs
    for j in range(o_ref.shape[0]):
        x = acc[:, j * LANES:(j + 1) * LANES]
        rotate = is_rope(n, j)
        if rotate is not False:
            cos = cos_ref[rows, :]
            sin_signed = sin_ref[rows, :]
            if rotate is not True:
                cos = jnp.where(rotate, cos, 1.0)
                sin_signed = jnp.where(rotate, sin_signed, 0.0)
            x = (x * cos + _swap_halves(x, half) * sin_signed) * q_scale(n, j)
        o_ref[j, rows, :] = x.astype(o_ref.dtype)


def _gated_residual_epilogue(acc, rows, row0, n, refs, outs, bias):
    x_ref, g_ref = refs
    outs[0][rows, :] = x_ref[rows, :] + g_ref[pl.ds(row0 // SEQ, 1), :] * acc


def _gated_residual_prenorm_epilogue(acc, rows, row0, n, refs, outs, bias):
    x_ref, g_ref, gain_ref, sc_ref = refs
    x_out, xs_out, ss_out = outs
    b = row0 // SEQ
    xn = x_ref[rows, :] + g_ref[pl.ds(b, 1), :] * acc
    x_out[rows, :] = xn
    xs_out[rows, :] = (xn * (gain_ref[...] * (1.0 + sc_ref[pl.ds(b, 1), :]))).astype(xs_out.dtype)
    ss_out[rows, :] = jnp.broadcast_to(jnp.sum(xn * xn, axis=-1, keepdims=True),
                                       (rows.stop - rows.start, LANES))


def _relu2_epilogue(acc, rows, row0, n, refs, outs, bias):
    r = jnp.maximum(acc, 0.0)
    outs[0][rows, :] = (r * r).astype(outs[0].dtype)


def _postnorm_relu2_epilogue(acc, rows, row0, n, refs, outs, bias):
    ss_ref = refs[0]
    ss = ss_ref[0, rows, :]
    for i in range(1, ss_ref.shape[0]):
        ss = ss + ss_ref[i, rows, :]
    inv = lax.rsqrt(ss * (1.0 / D_MODEL) + EPS)
    inv = jnp.concatenate([inv] * (acc.shape[1] // LANES), axis=1)
    r = jnp.maximum(acc * inv + bias[pl.ds(row0 // SEQ, 1), :], 0.0)
    outs[0][rows, :] = (r * r).astype(outs[0].dtype)


def _matmul_kernel(*refs, n_a, n_extra, n_out, nk, tm, m_inner, epilogue, shift_idx):
    a_refs = refs[:n_a]
    w_ref = refs[n_a]
    extra = refs[n_a + 1:n_a + 1 + n_extra]
    outs = refs[n_a + 1 + n_extra:n_a + 1 + n_extra + n_out]
    scratch = refs[n_a + 1 + n_extra + n_out:]
    wb_ref = scratch[0]
    bias_ref = scratch[-1] if shift_idx is not None else None
    outer, n, k, m = (pl.program_id(i) for i in range(4))

    acc_ref = scratch[1] if nk > 1 else None

    @pl.when(m == 0)
    def _():
        wb_ref[...] = w_ref[...].astype(BF16)
        if shift_idx is not None:
            bias_ref[...] = jnp.dot(extra[shift_idx][...].astype(BF16), wb_ref[...],
                                    preferred_element_type=F32)

    def product(t):
        rows = slice(t * MM_SUB, (t + 1) * MM_SUB)
        part = None
        off = 0
        for a_ref in a_refs:
            ka = a_ref.shape[1]
            d = jnp.dot(a_ref[rows, :], wb_ref[off:off + ka, :], preferred_element_type=F32)
            part = d if part is None else part + d
            off += ka
        return part

    def finish(k_pass, t, part):
        rows = slice(t * MM_SUB, (t + 1) * MM_SUB)
        local_row0 = m * tm + t * MM_SUB
        arows = pl.ds(pl.multiple_of(local_row0, MM_SUB), MM_SUB)
        if k_pass == "first":
            acc_ref[arows, :] = part
        elif k_pass == "middle":
            acc_ref[arows, :] += part
        else:
            if k_pass == "last":
                part = acc_ref[arows, :] + part
            epilogue(part, rows, outer * (m_inner * tm) + local_row0, n, extra, outs, bias_ref)

    def run(k_pass):
        _software_pipeline(tm // MM_SUB, product, functools.partial(finish, k_pass), lookahead=1)

    if nk == 1:
        run("only")
    else:
        pl.when(k == 0)(functools.partial(run, "first"))
        if nk > 2:
            pl.when(jnp.logical_and(k > 0, k < nk - 1))(functools.partial(run, "middle"))
        pl.when(k == nk - 1)(functools.partial(run, "last"))


def _block(shape, index_map, m_inner):
    return pl.BlockSpec(
        shape, lambda o, n, k, m: index_map(n, k, o * m_inner + m, o * m_inner))


def _matmul(a_list, a_maps, w, layer, col0, n_cols, tm, tn, m_outer, epilogue, extra_args,
            extra_blocks, out_blocks, out_shapes, name, shift_idx=None, nk=1):
    k_total = w.shape[1]
    assert col0 % tn == 0 and n_cols % tn == 0 and tm % MM_SUB == 0
    assert TOKENS % (tm * m_outer) == 0 and k_total % nk == 0
    n_off = col0 // tn
    tk = k_total // nk
    m_inner = TOKENS // tm // m_outer
    grid = (m_outer, n_cols // tn, nk, m_inner)
    a_specs = [_block(shape, imap, m_inner) for shape, imap in a_maps]
    w_spec = pl.BlockSpec((None, tk, tn), lambda o, n, k, m: (layer, k, n + n_off))
    scratch = [pltpu.VMEM((tk, tn), BF16)]
    if nk > 1:
        scratch.append(pltpu.VMEM((m_inner * tm, tn), F32))
    if shift_idx is not None:
        assert nk == 1
        scratch.append(pltpu.VMEM((SUBLANES, tn), F32))
    return pl.pallas_call(
        functools.partial(_matmul_kernel, n_a=len(a_list), n_extra=len(extra_args),
                          n_out=len(out_blocks), nk=nk, tm=tm, m_inner=m_inner, epilogue=epilogue,
                          shift_idx=shift_idx),
        grid=grid,
        in_specs=a_specs + [w_spec] + [_block(s, f, m_inner) for s, f in extra_blocks],
        out_specs=[_block(s, f, m_inner) for s, f in out_blocks],
        out_shape=out_shapes,
        scratch_shapes=scratch,
        compiler_params=_params("arbitrary", "arbitrary", "arbitrary", "arbitrary"),
        name=name,
    )(*a_list, w, *extra_args)


def _head_proj(h, w_in, layer, col0, n_cols, tm, tn, tables, half, is_rope, q_scale, name):
    table_blocks = [((tm, LANES), lambda n, k, r, r0: (r, 0)) for _ in tables]
    epilogue = functools.partial(_head_epilogue, half=half, is_rope=is_rope, q_scale=q_scale)
    return _matmul(
        [h], [((tm, D_MODEL), lambda n, k, r, r0: (r, 0))], w_in, layer, col0, n_cols, tm, tn, 1,
        epilogue, list(tables), table_blocks,
        [((tn // LANES, tm, LANES), lambda n, k, r, r0: (n, r, 0))],
        [jax.ShapeDtypeStruct((n_cols // LANES, TOKENS, LANES), BF16)], name)[0]


DIFF_IN_TILE = (2048, DIFF_W)
SWA_IN_WIDTH = SWA_Q_W + 2 * SWA_KV_W
SWA_IN_TILE = (1024, SWA_IN_WIDTH)
KA_BLK = DIFF_W // LANES
VA_BLK = 2 * DIFF_W // LANES
QB_BLK = 0
KB_BLK = SWA_Q_W // LANES
VB_BLK = KB_BLK + SWA_KV_W // LANES


def _in_proj(h, w_in, layer, tables):
    ca, sa, cb, sb = tables
    diff = _head_proj(
        h, w_in, layer, 0, 3 * DIFF_W, *DIFF_IN_TILE, (ca, sa), DIFF_QK_DIM // 2,
        lambda n, j: n < 2,
        lambda n, j: jnp.where(n == 0, DIFF_QK_DIM ** -0.5 * LOG2E, 1.0), "in_proj_diff")
    swa = _head_proj(
        h, w_in, layer, 3 * DIFF_W, SWA_IN_WIDTH, *SWA_IN_TILE, (cb, sb), HEAD_DIM // 2,
        lambda n, j: j < VB_BLK,
        lambda n, j: HEAD_DIM ** -0.5 * LOG2E if j < KB_BLK else 1.0, "in_proj_swa")
    return diff, swa


def _gated_residual_matmul(a_list, a_maps, w, layer, x, mod, gate_idx, tm, tn, m_outer, name,
                           next_norm=None, nk=1):
    last = nk - 1

    def xo_map(n, k, r, r0):
        return (jnp.where(k == last, r, r0), n)

    def mod_row(idx):
        return ((None, SUBLANES, tn), lambda n, k, r, r0: (idx, 0, n))

    args = [x, mod]
    blocks = [((tm, tn), xo_map), mod_row(gate_idx)]
    out_blocks = [((tm, tn), xo_map)]
    out_shapes = [jax.ShapeDtypeStruct((TOKENS, D_MODEL), F32)]
    epilogue = _gated_residual_epilogue
    if next_norm is not None:
        assert last == 0
        gain, scale_idx = next_norm
        args += [gain, mod]
        blocks += [((1, tn), lambda n, k, r, r0: (0, n)), mod_row(scale_idx)]
        out_blocks += [((tm, tn), xo_map), ((None, tm, LANES), lambda n, k, r, r0: (n, r, 0))]
        out_shapes += [jax.ShapeDtypeStruct((TOKENS, D_MODEL), BF16),
                       jax.ShapeDtypeStruct((D_MODEL // tn, TOKENS, LANES), F32)]
        epilogue = _gated_residual_prenorm_epilogue
    return _matmul(a_list, a_maps, w, layer, 0, D_MODEL, tm, tn, m_outer, epilogue, args, blocks,
                   out_blocks, out_shapes, name, nk=nk)


UP_TILE = (2048, 1024)
OUT_TILE = (1024, 1024)
DOWN_TILE = (1024, 512)
DOWN_M_OUTER = 2
DOWN_NK = 2
DOWN_TK = D_FF // DOWN_NK


def _out_proj(out_a, out_b, w_out, layer, x, mod, gate_idx, mlp_gain, mlp_scale_idx):
    tm, tn = OUT_TILE
    a_maps = [((tm, a.shape[1]), lambda n, k, r, r0: (r, 0)) for a in (out_a, out_b)]
    return _gated_residual_matmul([out_a, out_b], a_maps, w_out, layer, x, mod, gate_idx, tm, tn, 1,
                                  "out_proj", next_norm=(mlp_gain, mlp_scale_idx))


def _up_proj(xs, ss, w_up, layer, mod, shift_idx):
    tm, tn = UP_TILE
    per_block = DOWN_TK // tn
    blocks = [((ss.shape[0], tm, LANES), lambda n, k, r, r0: (0, r, 0)),
              ((None, SUBLANES, D_MODEL), lambda n, k, r, r0: (shift_idx, 0, 0))]
    return _matmul(
        [xs], [((tm, D_MODEL), lambda n, k, r, r0: (r, 0))], w_up, layer, 0, D_FF, tm, tn, 1,
        _postnorm_relu2_epilogue, [ss, mod], blocks,
        [((None, tm, tn), lambda n, k, r, r0: (n // per_block, r, n % per_block))],
        [jax.ShapeDtypeStruct((DOWN_NK, TOKENS, DOWN_TK), BF16)], "up_proj", shift_idx=1)[0]


def _down_proj(hidden, w_down, layer, x, mod, gate_idx):
    tm, tn = DOWN_TILE
    a_maps = [((None, tm, DOWN_TK), lambda n, k, r, r0: (k, r, 0))]
    return _gated_residual_matmul([hidden], a_maps, w_down, layer, x, mod, gate_idx, tm, tn,
                                  DOWN_M_OUTER, "down_proj", nk=DOWN_NK)[0]


DIFF_CHUNK = 512
DIFF_HEADS_PER_STEP = 1
ATTN_LOOKAHEAD = 3
_NT = (((1,), (1,)), ((), ()))
_TN = (((0,), (0,)), ((), ()))


def _diff_attn_kernel(q_ref, k_ref, v_ref, lam_ref, g_ref, o_ref, *, lam_init):
    lane = lax.broadcasted_iota(jnp.int32, (1, LANES), 1)
    lp = lam_ref[...]
    lam = (jnp.exp(jnp.sum(lp[0:1] * lp[1:2], axis=-1, keepdims=True))
           - jnp.exp(jnp.sum(lp[2:3] * lp[3:4], axis=-1, keepdims=True)) + lam_init)
    n_chunks = SEQ // DIFF_CHUNK
    per_head = 2 * n_chunks
    results = {}

    def logits(t):
        head, rem = divmod(t, per_head)
        chunk, second = divmod(rem, 2)
        q = q_ref[head, chunk * DIFF_CHUNK:(chunk + 1) * DIFF_CHUNK, :]
        keep = (lane >= DIFF_QK_DIM) if second else (lane < DIFF_QK_DIM)
        qm = jnp.where(keep, q, jnp.zeros_like(q))
        return lax.dot_general(k_ref[head], qm, _NT, preferred_element_type=F32)

    def softmax_pv(t, st):
        e = jnp.exp2(st - jnp.max(st, axis=0, keepdims=True))
        denom = jnp.sum(e, axis=0, keepdims=True)
        ot = lax.dot_general(v_ref[t // per_head], e.astype(BF16), _TN,
                             preferred_element_type=F32)
        results[t] = (ot, denom)

    _software_pipeline(DIFF_HEADS_PER_STEP * per_head, logits, softmax_pv, ATTN_LOOKAHEAD)
    for head in range(DIFF_HEADS_PER_STEP):
        for chunk in range(n_chunks):
            t = head * per_head + 2 * chunk
            (ot0, l0), (ot1, l1) = results[t], results[t + 1]
            o = (ot0 * (1.0 / l0) - ot1 * (lam / l1)).T
            o = o * lax.rsqrt(jnp.mean(o * o, axis=-1, keepdims=True) + EPS)
            o_ref[chunk * DIFF_CHUNK:(chunk + 1) * DIFF_CHUNK, head * HEAD_DIM:(head + 1) * HEAD_DIM] = (
                o * g_ref[...] * (1.0 - lam_init)).astype(o_ref.dtype)


def _diff_attention(qkv, diff_lambda, subln, layer):
    lam_init = 0.8 - 0.6 * math.exp(-0.3 * layer)
    hps = DIFF_HEADS_PER_STEP
    return pl.pallas_call(
        functools.partial(_diff_attn_kernel, lam_init=lam_init),
        grid=(BATCH, N_HEADS_DIFF // hps),
        in_specs=[
            pl.BlockSpec((hps, SEQ, LANES), lambda b, h: (h, b, 0)),
            pl.BlockSpec((hps, SEQ, LANES), lambda b, h: (KA_BLK // hps + h, b, 0)),
            pl.BlockSpec((hps, SEQ, LANES), lambda b, h: (VA_BLK // hps + h, b, 0)),
            pl.BlockSpec((None, 4, DIFF_QK_DIM), lambda b, h: (layer, 0, 0)),
            pl.BlockSpec((None, 1, HEAD_DIM), lambda b, h: (layer, 0, 0)),
        ],
        out_specs=pl.BlockSpec((SEQ, hps * HEAD_DIM), lambda b, h: (b, h)),
        out_shape=jax.ShapeDtypeStruct((TOKENS, DIFF_W), BF16),
        compiler_params=_params("arbitrary", "arbitrary"),
        name="diff_attn",
    )(qkv, qkv, qkv, diff_lambda, subln.reshape(DEPTH, 1, HEAD_DIM))


SWA_BAND = 3 * BLOCK
SWA_QB = 16


def _swa_attn_kernel(q_ref, k_ref, v_ref, sink_ref, o_ref):
    step = pl.program_id(2)
    sink = sink_ref[...] * LOG2E

    def band_start(j):
        n = step * SWA_QB + j
        return n, pl.multiple_of(jnp.clip(n - 1, 0, SEQ // BLOCK - 3) * BLOCK, BLOCK)

    def logits(j):
        n, start = band_start(j)
        q = q_ref[:, j * BLOCK:(j + 1) * BLOCK, :].reshape(GQA_RATIO * BLOCK, HEAD_DIM)
        kb = k_ref[pl.ds(start, SWA_BAND), :]
        return lax.dot_general(kb, q, _NT, preferred_element_type=F32)

    def softmax_pv(j, st):
        n, start = band_start(j)
        kpos = start + lax.broadcasted_iota(jnp.int32, (SWA_BAND, BLOCK), 0)
        qpos = n * BLOCK + lax.broadcasted_iota(jnp.int32, (SWA_BAND, BLOCK), 1)
        bias = jnp.where(jnp.abs(kpos - qpos) <= WINDOW, 0.0, -jnp.inf).astype(F32)
        st = st + jnp.concatenate([bias] * GQA_RATIO, axis=1)
        mx = jnp.maximum(jnp.max(st, axis=0, keepdims=True), sink)
        e = jnp.exp2(st - mx)
        denom = jnp.sum(e, axis=0, keepdims=True) + jnp.exp2(sink - mx)
        vb = v_ref[pl.ds(start, SWA_BAND), :]
        ot = lax.dot_general(vb, e.astype(BF16), _TN, preferred_element_type=F32)
        ot = ot * (1.0 / denom)
        for r in range(GQA_RATIO):
            o_ref[j * BLOCK:(j + 1) * BLOCK, r * HEAD_DIM:(r + 1) * HEAD_DIM] = (
                ot[:, r * BLOCK:(r + 1) * BLOCK].T.astype(o_ref.dtype))

    _software_pipeline(SWA_QB, logits, softmax_pv, ATTN_LOOKAHEAD)


def _swa_attention(qkv, sink):
    steps = SEQ // BLOCK // SWA_QB
    rows = SWA_QB * BLOCK
    sink_rows = jnp.repeat(sink, BLOCK).reshape(N_KV_SWA, 1, GQA_RATIO * BLOCK)
    return pl.pallas_call(
        _swa_attn_kernel,
        grid=(BATCH, N_KV_SWA, steps),
        in_specs=[
            pl.BlockSpec((GQA_RATIO, rows, LANES),
                         lambda b, g, n: (QB_BLK // GQA_RATIO + g, b * steps + n, 0)),
            pl.BlockSpec((None, SEQ, LANES), lambda b, g, n: (KB_BLK + g, b, 0)),
            pl.BlockSpec((None, SEQ, LANES), lambda b, g, n: (VB_BLK + g, b, 0)),
            pl.BlockSpec((None, 1, GQA_RATIO * BLOCK), lambda b, g, n: (g, 0, 0)),
        ],
        out_specs=pl.BlockSpec((rows, GQA_RATIO * HEAD_DIM), lambda b, g, n: (b * steps + n, g)),
        out_shape=jax.ShapeDtypeStruct((TOKENS, SWA_Q_W), BF16),
        compiler_params=_params("arbitrary", "arbitrary", "arbitrary"),
        name="swa_attn",
    )(qkv, qkv, qkv, sink_rows)


def kernel(x, c, positions, ada_w, ada_b, norm_mix, w_in, diff_lambda, diff_subln, swa_sink,
           w_out, norm_mlp, w_up, w_down, final_norm):
    mod = _modulation(c, ada_w, ada_b)
    tables = _rope_tables(positions)
    xr = x.reshape(TOKENS, D_MODEL)
    for layer in range(DEPTH):
        base = layer * N_MOD
        h = _norm(xr, norm_mix[layer], mod, base + 1, base + 0)
        qkv_diff, qkv_swa = _in_proj(h, w_in, layer, tables)
        out_a = _diff_attention(qkv_diff, diff_lambda, diff_subln, layer)
        out_b = _swa_attention(qkv_swa, swa_sink[layer])
        xr, xs, ss = _out_proj(out_a, out_b, w_out, layer, xr, mod, base + 2,
                               norm_mlp[layer].reshape(1, D_MODEL), base + 4)
        hidden = _up_proj(xs, ss, w_up, layer, mod, base + 3)
        xr = _down_proj(hidden, w_down, layer, xr, mod, base + 5)
    out = _norm(xr, final_norm, out_dtype=F32)
    return out.reshape(BATCH, SEQ, D_MODEL)
```

```python
import functools
import math

import jax
import jax.numpy as jnp
from jax import lax
from jax.experimental import pallas as pl
from jax.experimental.pallas import tpu as pltpu

D_MODEL = 2048
BATCH = 4
SEQ = 2048
DEPTH = 2
HEAD_DIM = 128
N_HEADS_DIFF = 8
DIFF_QK_DIM = 64
N_HEADS_SWA = 8
N_KV_SWA = 2
GQA_RATIO = N_HEADS_SWA // N_KV_SWA
WINDOW = 128
BLOCK = 128
D_FF = 4 * D_MODEL
ROPE_THETA = 10000.0
EPS = 1e-6
N_MOD = 6
DIFF_W = N_HEADS_DIFF * HEAD_DIM
SWA_Q_W = N_HEADS_SWA * HEAD_DIM
SWA_KV_W = N_KV_SWA * HEAD_DIM
IN_WIDTH = 3 * DIFF_W + SWA_Q_W + 2 * SWA_KV_W
TOKENS = BATCH * SEQ

LANES = 128
SUBLANES = 8
VMEM_LIMIT = 56 * 1024 * 1024

F32 = jnp.float32
BF16 = jnp.bfloat16
LOG2E = math.log2(math.e)


def _params(*sem):
    return pltpu.CompilerParams(dimension_semantics=sem, vmem_limit_bytes=VMEM_LIMIT)


def _software_pipeline(n_items, produce, consume, lookahead):
    pending = {}
    for t in range(n_items + lookahead):
        if t < n_items:
            pending[t] = produce(t)
        if t >= lookahead:
            consume(t - lookahead, pending.pop(t - lookahead))


MOD_TN = 2048


def _mod_kernel(c_ref, w_ref, b_ref, o_ref):
    c = c_ref[...]
    c_act = (c * jax.nn.sigmoid(c)).astype(BF16)
    o_ref[...] = jnp.dot(c_act, w_ref[...].astype(BF16), preferred_element_type=F32) + b_ref[...]


def _modulation(c, ada_w, ada_b):
    c_pad = jnp.pad(c, ((0, SUBLANES - BATCH), (0, 0)))
    per_chunk = D_MODEL // MOD_TN
    return pl.pallas_call(
        _mod_kernel,
        grid=(DEPTH, N_MOD * per_chunk),
        in_specs=[
            pl.BlockSpec((SUBLANES, D_MODEL), lambda l, j: (0, 0)),
            pl.BlockSpec((None, D_MODEL, MOD_TN), lambda l, j: (l, 0, j)),
            pl.BlockSpec((None, 1, MOD_TN), lambda l, j: (l, 0, j)),
        ],
        out_specs=pl.BlockSpec((None, SUBLANES, MOD_TN),
                               lambda l, j: (l * N_MOD + j // per_chunk, 0, j % per_chunk)),
        out_shape=jax.ShapeDtypeStruct((DEPTH * N_MOD, SUBLANES, D_MODEL), F32),
        compiler_params=_params("arbitrary", "arbitrary"),
        name="adaln_mod",
    )(c_pad, ada_w, ada_b.reshape(DEPTH, 1, N_MOD * D_MODEL))


ROPE_TM = 1024


ROPE_A = DIFF_QK_DIM // 2
ROPE_B = HEAD_DIM // 2


def _rope_kernel(pos_ref, inv_ref, ca_ref, sa_ref, cb_ref, sb_ref):
    pos = pos_ref[...].astype(F32)
    lane = lax.broadcasted_iota(jnp.int32, (1, LANES), 1)
    ang = pos * inv_ref[...]
    sign_a = jnp.where((lane & ROPE_A) == 0, -1.0, 1.0).astype(F32)
    sign_b = jnp.where((lane & ROPE_B) == 0, -1.0, 1.0).astype(F32)

    def spread_a(t):
        return jnp.where(lane < ROPE_A, t,
                         jnp.where(lane < 2 * ROPE_A, pltpu.roll(t, ROPE_A, 1),
                                   jnp.where(lane < 3 * ROPE_A, pltpu.roll(t, 2 * ROPE_A, 1),
                                             pltpu.roll(t, 3 * ROPE_A, 1))))

    def spread_b(t):
        return jnp.where(lane < ROPE_B, pltpu.roll(t, LANES - ROPE_A, 1), pltpu.roll(t, ROPE_A, 1))

    c = jnp.cos(ang)
    s = jnp.sin(ang)
    ca_ref[...] = spread_a(c)
    sa_ref[...] = spread_a(s) * sign_a
    cb_ref[...] = spread_b(c)
    sb_ref[...] = spread_b(s) * sign_b


def _rope_tables(positions):
    def inv_freq(dim):
        return ROPE_THETA ** (-jnp.arange(0, dim, 2, dtype=F32) / dim)
    inv_b = inv_freq(HEAD_DIM)
    inv = jnp.concatenate([inv_freq(DIFF_QK_DIM), inv_b, inv_b[:LANES - ROPE_A - ROPE_B]])
    tab = jax.ShapeDtypeStruct((TOKENS, LANES), F32)
    row = pl.BlockSpec((ROPE_TM, LANES), lambda m: (m, 0))
    return pl.pallas_call(
        _rope_kernel,
        grid=(TOKENS // ROPE_TM,),
        in_specs=[pl.BlockSpec((ROPE_TM, 1), lambda m: (m, 0)),
                  pl.BlockSpec((1, LANES), lambda m: (0, 0))],
        out_specs=[row, row, row, row],
        out_shape=[tab, tab, tab, tab],
        compiler_params=_params("arbitrary"),
        name="rope_tables",
    )(positions.reshape(TOKENS, 1), inv.reshape(1, LANES))


NORM_TM = 1024


def _norm_kernel(x_ref, g_ref, *rest, modulated):
    x = x_ref[...]
    y = x * lax.rsqrt(jnp.mean(x * x, axis=-1, keepdims=True) + EPS)
    y = y * g_ref[...]
    if modulated:
        sc_ref, sh_ref, o_ref = rest
        b = (pl.program_id(0) * NORM_TM) // SEQ
        y = y * (1.0 + sc_ref[pl.ds(b, 1), :]) + sh_ref[pl.ds(b, 1), :]
    else:
        (o_ref,) = rest
    o_ref[...] = y.astype(o_ref.dtype)


def _norm(x, gain, mod=None, sc_idx=None, sh_idx=None, out_dtype=BF16):
    in_specs = [pl.BlockSpec((NORM_TM, D_MODEL), lambda m: (m, 0)),
                pl.BlockSpec((1, D_MODEL), lambda m: (0, 0))]
    args = [x, gain.reshape(1, D_MODEL)]
    if mod is not None:
        in_specs += [pl.BlockSpec((None, SUBLANES, D_MODEL), lambda m: (sc_idx, 0, 0)),
                     pl.BlockSpec((None, SUBLANES, D_MODEL), lambda m: (sh_idx, 0, 0))]
        args += [mod, mod]
    return pl.pallas_call(
        functools.partial(_norm_kernel, modulated=mod is not None),
        grid=(TOKENS // NORM_TM,),
        in_specs=in_specs,
        out_specs=pl.BlockSpec((NORM_TM, D_MODEL), lambda m: (m, 0)),
        out_shape=jax.ShapeDtypeStruct((TOKENS, D_MODEL), out_dtype),
        compiler_params=_params("arbitrary"),
        name="rmsnorm_mod" if mod is not None else "rmsnorm_final",
    )(*args)


MM_SUB = 512


def _swap_halves(x, half):
    if 2 * half == LANES:
        return pltpu.roll(x, half, 1)
    lane = lax.broadcasted_iota(jnp.int32, (1, LANES), 1)
    return jnp.where((lane & half) == 0,
                     pltpu.roll(x, LANES - half, 1), pltpu.roll(x, half, 1))


def _head_epilogue(acc, rows, row0, n, refs, outs, bias, *, half, is_rope, q_scale):
    cos_ref, sin_ref = refs
    (o_ref,) = outs
    for j in range(o_ref.shape[0]):
        x = acc[:, j * LANES:(j + 1) * LANES]
        rotate = is_rope(n, j)
        if rotate is not False:
            cos = cos_ref[rows, :]
            sin_signed = sin_ref[rows, :]
            if rotate is not True:
                cos = jnp.where(rotate, cos, 1.0)
                sin_signed = jnp.where(rotate, sin_signed, 0.0)
            x = (x * cos + _swap_halves(x, half) * sin_signed) * q_scale(n, j)
        o_ref[j, rows, :] = x.astype(o_ref.dtype)


def _gated_residual_epilogue(acc, rows, row0, n, refs, outs, bias):
    x_ref, g_ref = refs
    outs[0][rows, :] = x_ref[rows, :] + g_ref[pl.ds(row0 // SEQ, 1), :] * acc


def _gated_residual_prenorm_epilogue(acc, rows, row0, n, refs, outs, bias):
    x_ref, g_ref, gain_ref, sc_ref = refs
    x_out, xs_out, ss_out = outs
    b = row0 // SEQ
    xn = x_ref[rows, :] + g_ref[pl.ds(b, 1), :] * acc
    x_out[rows, :] = xn
    xs_out[rows, :] = (xn * (gain_ref[...] * (1.0 + sc_ref[pl.ds(b, 1), :]))).astype(xs_out.dtype)
    ss_out[rows, :] = jnp.broadcast_to(jnp.sum(xn * xn, axis=-1, keepdims=True),
                                       (rows.stop - rows.start, LANES))


def _relu2_epilogue(acc, rows, row0, n, refs, outs, bias):
    r = jnp.maximum(acc, 0.0)
    outs[0][rows, :] = (r * r).astype(outs[0].dtype)


def _postnorm_relu2_epilogue(acc, rows, row0, n, refs, outs, bias):
    ss_ref = refs[0]
    ss = ss_ref[0, rows, :]
    for i in range(1, ss_ref.shape[0]):
        ss = ss + ss_ref[i, rows, :]
    inv = lax.rsqrt(ss * (1.0 / D_MODEL) + EPS)
    inv = jnp.concatenate([inv] * (acc.shape[1] // LANES), axis=1)
    r = jnp.maximum(acc * inv + bias[pl.ds(row0 // SEQ, 1), :], 0.0)
    outs[0][rows, :] = (r * r).astype(outs[0].dtype)


def _matmul_kernel(*refs, n_a, n_extra, n_out, nk, tm, m_inner, epilogue, shift_idx):
    a_refs = refs[:n_a]
    w_ref = refs[n_a]
    extra = refs[n_a + 1:n_a + 1 + n_extra]
    outs = refs[n_a + 1 + n_extra:n_a + 1 + n_extra + n_out]
    scratch = refs[n_a + 1 + n_extra + n_out:]
    wb_ref = scratch[0]
    bias_ref = scratch[-1] if shift_idx is not None else None
    outer, n, k, m = (pl.program_id(i) for i in range(4))

    acc_ref = scratch[1] if nk > 1 else None

    @pl.when(m == 0)
    def _():
        wb_ref[...] = w_ref[...].astype(BF16)
        if shift_idx is not None:
            bias_ref[...] = jnp.dot(extra[shift_idx][...].astype(BF16), wb_ref[...],
                                    preferred_element_type=F32)

    def product(t):
        rows = slice(t * MM_SUB, (t + 1) * MM_SUB)
        part = None
        off = 0
        for a_ref in a_refs:
            ka = a_ref.shape[1]
            d = jnp.dot(a_ref[rows, :], wb_ref[off:off + ka, :], preferred_element_type=F32)
            part = d if part is None else part + d
            off += ka
        return part

    def finish(k_pass, t, part):
        rows = slice(t * MM_SUB, (t + 1) * MM_SUB)
        local_row0 = m * tm + t * MM_SUB
        arows = pl.ds(pl.multiple_of(local_row0, MM_SUB), MM_SUB)
        if k_pass == "first":
            acc_ref[arows, :] = part
        elif k_pass == "middle":
            acc_ref[arows, :] += part
        else:
            if k_pass == "last":
                part = acc_ref[arows, :] + part
            epilogue(part, rows, outer * (m_inner * tm) + local_row0, n, extra, outs, bias_ref)

    def run(k_pass):
        _software_pipeline(tm // MM_SUB, product, functools.partial(finish, k_pass), lookahead=1)

    if nk == 1:
        run("only")
    else:
        pl.when(k == 0)(functools.partial(run, "first"))
        if nk > 2:
            pl.when(jnp.logical_and(k > 0, k < nk - 1))(functools.partial(run, "middle"))
        pl.when(k == nk - 1)(functools.partial(run, "last"))


def _block(shape, index_map, m_inner):
    return pl.BlockSpec(
        shape, lambda o, n, k, m: index_map(n, k, o * m_inner + m, o * m_inner))


def _matmul(a_list, a_maps, w, layer, col0, n_cols, tm, tn, m_outer, epilogue, extra_args,
            extra_blocks, out_blocks, out_shapes, name, shift_idx=None, nk=1):
    k_total = w.shape[1]
    assert col0 % tn == 0 and n_cols % tn == 0 and tm % MM_SUB == 0
    assert TOKENS % (tm * m_outer) == 0 and k_total % nk == 0
    n_off = col0 // tn
    tk = k_total // nk
    m_inner = TOKENS // tm // m_outer
    grid = (m_outer, n_cols // tn, nk, m_inner)
    a_specs = [_block(shape, imap, m_inner) for shape, imap in a_maps]
    w_spec = pl.BlockSpec((None, tk, tn), lambda o, n, k, m: (layer, k, n + n_off))
    scratch = [pltpu.VMEM((tk, tn), BF16)]
    if nk > 1:
        scratch.append(pltpu.VMEM((m_inner * tm, tn), F32))
    if shift_idx is not None:
        assert nk == 1
        scratch.append(pltpu.VMEM((SUBLANES, tn), F32))
    return pl.pallas_call(
        functools.partial(_matmul_kernel, n_a=len(a_list), n_extra=len(extra_args),
                          n_out=len(out_blocks), nk=nk, tm=tm, m_inner=m_inner, epilogue=epilogue,
                          shift_idx=shift_idx),
        grid=grid,
        in_specs=a_specs + [w_spec] + [_block(s, f, m_inner) for s, f in extra_blocks],
        out_specs=[_block(s, f, m_inner) for s, f in out_blocks],
        out_shape=out_shapes,
        scratch_shapes=scratch,
        compiler_params=_params("arbitrary", "arbitrary", "arbitrary", "arbitrary"),
        name=name,
    )(*a_list, w, *extra_args)


def _head_proj(h, w_in, layer, col0, n_cols, tm, tn, tables, half, is_rope, q_scale, name):
    table_blocks = [((tm, LANES), lambda n, k, r, r0: (r, 0)) for _ in tables]
    epilogue = functools.partial(_head_epilogue, half=half, is_rope=is_rope, q_scale=q_scale)
    return _matmul(
        [h], [((tm, D_MODEL), lambda n, k, r, r0: (r, 0))], w_in, layer, col0, n_cols, tm, tn, 1,
        epilogue, list(tables), table_blocks,
        [((tn // LANES, tm, LANES), lambda n, k, r, r0: (n, r, 0))],
        [jax.ShapeDtypeStruct((n_cols // LANES, TOKENS, LANES), BF16)], name)[0]


DIFF_IN_TILE = (2048, DIFF_W)
SWA_IN_WIDTH = SWA_Q_W + 2 * SWA_KV_W
SWA_IN_TILE = (1024, SWA_IN_WIDTH)
KA_BLK = DIFF_W // LANES
VA_BLK = 2 * DIFF_W // LANES
QB_BLK = 0
KB_BLK = SWA_Q_W // LANES
VB_BLK = KB_BLK + SWA_KV_W // LANES


def _in_proj(h, w_in, layer, tables):
    ca, sa, cb, sb = tables
    diff = _head_proj(
        h, w_in, layer, 0, 3 * DIFF_W, *DIFF_IN_TILE, (ca, sa), DIFF_QK_DIM // 2,
        lambda n, j: n < 2,
        lambda n, j: jnp.where(n == 0, DIFF_QK_DIM ** -0.5 * LOG2E, 1.0), "in_proj_diff")
    swa = _head_proj(
        h, w_in, layer, 3 * DIFF_W, SWA_IN_WIDTH, *SWA_IN_TILE, (cb, sb), HEAD_DIM // 2,
        lambda n, j: j < VB_BLK,
        lambda n, j: HEAD_DIM ** -0.5 * LOG2E if j < KB_BLK else 1.0, "in_proj_swa")
    return diff, swa


def _gated_residual_matmul(a_list, a_maps, w, layer, x, mod, gate_idx, tm, tn, m_outer, name,
                           next_norm=None, nk=1):
    last = nk - 1

    def xo_map(n, k, r, r0):
        return (jnp.where(k == last, r, r0), n)

    def mod_row(idx):
        return ((None, SUBLANES, tn), lambda n, k, r, r0: (idx, 0, n))

    args = [x, mod]
    blocks = [((tm, tn), xo_map), mod_row(gate_idx)]
    out_blocks = [((tm, tn), xo_map)]
    out_shapes = [jax.ShapeDtypeStruct((TOKENS, D_MODEL), F32)]
    epilogue = _gated_residual_epilogue
    if next_norm is not None:
        assert last == 0
        gain, scale_idx = next_norm
        args += [gain, mod]
        blocks += [((1, tn), lambda n, k, r, r0: (0, n)), mod_row(scale_idx)]
        out_blocks += [((tm, tn), xo_map), ((None, tm, LANES), lambda n, k, r, r0: (n, r, 0))]
        out_shapes += [jax.ShapeDtypeStruct((TOKENS, D_MODEL), BF16),
                       jax.ShapeDtypeStruct((D_MODEL // tn, TOKENS, LANES), F32)]
        epilogue = _gated_residual_prenorm_epilogue
    return _matmul(a_list, a_maps, w, layer, 0, D_MODEL, tm, tn, m_outer, epilogue, args, blocks,
                   out_blocks, out_shapes, name, nk=nk)


UP_TILE = (2048, 1024)
OUT_TILE = (1024, 1024)
DOWN_TILE = (2048, 512)
DOWN_M_OUTER = 2
DOWN_NK = 4
DOWN_TK = D_FF // DOWN_NK


def _out_proj(out_a, out_b, w_out, layer, x, mod, gate_idx, mlp_gain, mlp_scale_idx):
    tm, tn = OUT_TILE
    a_maps = [((tm, a.shape[1]), lambda n, k, r, r0: (r, 0)) for a in (out_a, out_b)]
    return _gated_residual_matmul([out_a, out_b], a_maps, w_out, layer, x, mod, gate_idx, tm, tn, 1,
                                  "out_proj", next_norm=(mlp_gain, mlp_scale_idx))


def _up_proj(xs, ss, w_up, layer, mod, shift_idx):
    tm, tn = UP_TILE
    per_block = DOWN_TK // tn
    blocks = [((ss.shape[0], tm, LANES), lambda n, k, r, r0: (0, r, 0)),
              ((None, SUBLANES, D_MODEL), lambda n, k, r, r0: (shift_idx, 0, 0))]
    return _matmul(
        [xs], [((tm, D_MODEL), lambda n, k, r, r0: (r, 0))], w_up, layer, 0, D_FF, tm, tn, 1,
        _postnorm_relu2_epilogue, [ss, mod], blocks,
        [((None, tm, tn), lambda n, k, r, r0: (n // per_block, r, n % per_block))],
        [jax.ShapeDtypeStruct((DOWN_NK, TOKENS, DOWN_TK), BF16)], "up_proj", shift_idx=1)[0]


def _down_proj(hidden, w_down, layer, x, mod, gate_idx):
    tm, tn = DOWN_TILE
    a_maps = [((None, tm, DOWN_TK), lambda n, k, r, r0: (k, r, 0))]
    return _gated_residual_matmul([hidden], a_maps, w_down, layer, x, mod, gate_idx, tm, tn,
                                  DOWN_M_OUTER, "down_proj", nk=DOWN_NK)[0]


DIFF_CHUNK = 512
DIFF_HEADS_PER_STEP = 1
ATTN_LOOKAHEAD = 3
_NT = (((1,), (1,)), ((), ()))
_TN = (((0,), (0,)), ((), ()))


def _diff_attn_kernel(q_ref, k_ref, v_ref, lam_ref, g_ref, o_ref, *, lam_init):
    lane = lax.broadcasted_iota(jnp.int32, (1, LANES), 1)
    lp = lam_ref[...]
    lam = (jnp.exp(jnp.sum(lp[0:1] * lp[1:2], axis=-1, keepdims=True))
           - jnp.exp(jnp.sum(lp[2:3] * lp[3:4], axis=-1, keepdims=True)) + lam_init)
    n_chunks = SEQ // DIFF_CHUNK
    per_head = 2 * n_chunks
    results = {}

    def logits(t):
        head, rem = divmod(t, per_head)
        chunk, second = divmod(rem, 2)
        q = q_ref[head, chunk * DIFF_CHUNK:(chunk + 1) * DIFF_CHUNK, :]
        keep = (lane >= DIFF_QK_DIM) if second else (lane < DIFF_QK_DIM)
        qm = jnp.where(keep, q, jnp.zeros_like(q))
        return lax.dot_general(k_ref[head], qm, _NT, preferred_element_type=F32)

    def softmax_pv(t, st):
        e = jnp.exp2(st - jnp.max(st, axis=0, keepdims=True))
        denom = jnp.sum(e, axis=0, keepdims=True)
        ot = lax.dot_general(v_ref[t // per_head], e.astype(BF16), _TN,
                             preferred_element_type=F32)
        results[t] = (ot, denom)

    _software_pipeline(DIFF_HEADS_PER_STEP * per_head, logits, softmax_pv, ATTN_LOOKAHEAD)
    for head in range(DIFF_HEADS_PER_STEP):
        for chunk in range(n_chunks):
            t = head * per_head + 2 * chunk
            (ot0, l0), (ot1, l1) = results[t], results[t + 1]
            o = (ot0 * (1.0 / l0) - ot1 * (lam / l1)).T
            o = o * lax.rsqrt(jnp.mean(o * o, axis=-1, keepdims=True) + EPS)
            o_ref[chunk * DIFF_CHUNK:(chunk + 1) * DIFF_CHUNK, head * HEAD_DIM:(head + 1) * HEAD_DIM] = (
                o * g_ref[...] * (1.0 - lam_init)).astype(o_ref.dtype)


def _diff_attention(qkv, diff_lambda, subln, layer):
    lam_init = 0.8 - 0.6 * math.exp(-0.3 * layer)
    hps = DIFF_HEADS_PER_STEP
    return pl.pallas_call(
        functools.partial(_diff_attn_kernel, lam_init=lam_init),
        grid=(BATCH, N_HEADS_DIFF // hps),
        in_specs=[
            pl.BlockSpec((hps, SEQ, LANES), lambda b, h: (h, b, 0)),
            pl.BlockSpec((hps, SEQ, LANES), lambda b, h: (KA_BLK // hps + h, b, 0)),
            pl.BlockSpec((hps, SEQ, LANES), lambda b, h: (VA_BLK // hps + h, b, 0)),
            pl.BlockSpec((None, 4, DIFF_QK_DIM), lambda b, h: (layer, 0, 0)),
            pl.BlockSpec((None, 1, HEAD_DIM), lambda b, h: (layer, 0, 0)),
        ],
        out_specs=pl.BlockSpec((SEQ, hps * HEAD_DIM), lambda b, h: (b, h)),
        out_shape=jax.ShapeDtypeStruct((TOKENS, DIFF_W), BF16),
        compiler_params=_params("arbitrary", "arbitrary"),
        name="diff_attn",
    )(qkv, qkv, qkv, diff_lambda, subln.reshape(DEPTH, 1, HEAD_DIM))


SWA_BAND = 3 * BLOCK
SWA_QB = 16


def _swa_attn_kernel(q_ref, k_ref, v_ref, sink_ref, o_ref):
    step = pl.program_id(2)
    sink = sink_ref[...] * LOG2E

    def band_start(j):
        n = step * SWA_QB + j
        return n, pl.multiple_of(jnp.clip(n - 1, 0, SEQ // BLOCK - 3) * BLOCK, BLOCK)

    def logits(j):
        n, start = band_start(j)
        q = q_ref[:, j * BLOCK:(j + 1) * BLOCK, :].reshape(GQA_RATIO * BLOCK, HEAD_DIM)
        kb = k_ref[pl.ds(start, SWA_BAND), :]
        return lax.dot_general(kb, q, _NT, preferred_element_type=F32)

    def softmax_pv(j, st):
        n, start = band_start(j)
        kpos = start + lax.broadcasted_iota(jnp.int32, (SWA_BAND, BLOCK), 0)
        qpos = n * BLOCK + lax.broadcasted_iota(jnp.int32, (SWA_BAND, BLOCK), 1)
        bias = jnp.where(jnp.abs(kpos - qpos) <= WINDOW, 0.0, -jnp.inf).astype(F32)
        st = st + jnp.concatenate([bias] * GQA_RATIO, axis=1)
        mx = jnp.maximum(jnp.max(st, axis=0, keepdims=True), sink)
        e = jnp.exp2(st - mx)
        denom = jnp.sum(e, axis=0, keepdims=True) + jnp.exp2(sink - mx)
        vb = v_ref[pl.ds(start, SWA_BAND), :]
        ot = lax.dot_general(vb, e.astype(BF16), _TN, preferred_element_type=F32)
        ot = ot * (1.0 / denom)
        for r in range(GQA_RATIO):
            o_ref[j * BLOCK:(j + 1) * BLOCK, r * HEAD_DIM:(r + 1) * HEAD_DIM] = (
                ot[:, r * BLOCK:(r + 1) * BLOCK].T.astype(o_ref.dtype))

    _software_pipeline(SWA_QB, logits, softmax_pv, ATTN_LOOKAHEAD)


def _swa_attention(qkv, sink):
    steps = SEQ // BLOCK // SWA_QB
    rows = SWA_QB * BLOCK
    sink_rows = jnp.repeat(sink, BLOCK).reshape(N_KV_SWA, 1, GQA_RATIO * BLOCK)
    return pl.pallas_call(
        _swa_attn_kernel,
        grid=(BATCH, N_KV_SWA, steps),
        in_specs=[
            pl.BlockSpec((GQA_RATIO, rows, LANES),
                         lambda b, g, n: (QB_BLK // GQA_RATIO + g, b * steps + n, 0)),
            pl.BlockSpec((None, SEQ, LANES), lambda b, g, n: (KB_BLK + g, b, 0)),
            pl.BlockSpec((None, SEQ, LANES), lambda b, g, n: (VB_BLK + g, b, 0)),
            pl.BlockSpec((None, 1, GQA_RATIO * BLOCK), lambda b, g, n: (g, 0, 0)),
        ],
        out_specs=pl.BlockSpec((rows, GQA_RATIO * HEAD_DIM), lambda b, g, n: (b * steps + n, g)),
        out_shape=jax.ShapeDtypeStruct((TOKENS, SWA_Q_W), BF16),
        compiler_params=_params("arbitrary", "arbitrary", "arbitrary"),
        name="swa_attn",
    )(qkv, qkv, qkv, sink_rows)


def kernel(x, c, positions, ada_w, ada_b, norm_mix, w_in, diff_lambda, diff_subln, swa_sink,
           w_out, norm_mlp, w_up, w_down, final_norm):
    mod = _modulation(c, ada_w, ada_b)
    tables = _rope_tables(positions)
    xr = x.reshape(TOKENS, D_MODEL)
    for layer in range(DEPTH):
        base = layer * N_MOD
        h = _norm(xr, norm_mix[layer], mod, base + 1, base + 0)
        qkv_diff, qkv_swa = _in_proj(h, w_in, layer, tables)
        out_a = _diff_attention(qkv_diff, diff_lambda, diff_subln, layer)
        out_b = _swa_attention(qkv_swa, swa_sink[layer])
        xr, xs, ss = _out_proj(out_a, out_b, w_out, layer, xr, mod, base + 2,
                               norm_mlp[layer].reshape(1, D_MODEL), base + 4)
        hidden = _up_proj(xs, ss, w_up, layer, mod, base + 3)
        xr = _down_proj(hidden, w_down, layer, xr, mod, base + 5)
    out = _norm(xr, final_norm, out_dtype=F32)
    return out.reshape(BATCH, SEQ, D_MODEL)
```

```python
import functools
import math

import jax
import jax.numpy as jnp
from jax import lax
from jax.experimental import pallas as pl
from jax.experimental.pallas import tpu as pltpu

D_MODEL = 2048
BATCH = 4
SEQ = 2048
DEPTH = 2
HEAD_DIM = 128
N_HEADS_DIFF = 8
DIFF_QK_DIM = 64
N_HEADS_SWA = 8
N_KV_SWA = 2
GQA_RATIO = N_HEADS_SWA // N_KV_SWA
WINDOW = 128
BLOCK = 128
D_FF = 4 * D_MODEL
ROPE_THETA = 10000.0
EPS = 1e-6
N_MOD = 6
DIFF_W = N_HEADS_DIFF * HEAD_DIM
SWA_Q_W = N_HEADS_SWA * HEAD_DIM
SWA_KV_W = N_KV_SWA * HEAD_DIM
IN_WIDTH = 3 * DIFF_W + SWA_Q_W + 2 * SWA_KV_W
TOKENS = BATCH * SEQ

LANES = 128
SUBLANES = 8
VMEM_LIMIT = 56 * 1024 * 1024

F32 = jnp.float32
BF16 = jnp.bfloat16
LOG2E = math.log2(math.e)


def _params(*sem):
    return pltpu.CompilerParams(dimension_semantics=sem, vmem_limit_bytes=VMEM_LIMIT)


def _software_pipeline(n_items, produce, consume, lookahead):
    pending = {}
    for t in range(n_items + lookahead):
        if t < n_items:
            pending[t] = produce(t)
        if t >= lookahead:
            consume(t - lookahead, pending.pop(t - lookahead))


MOD_TN = 2048


def _mod_kernel(c_ref, w_ref, b_ref, o_ref):
    c = c_ref[...]
    c_act = (c * jax.nn.sigmoid(c)).astype(BF16)
    o_ref[...] = jnp.dot(c_act, w_ref[...].astype(BF16), preferred_element_type=F32) + b_ref[...]


def _modulation(c, ada_w, ada_b):
    c_pad = jnp.pad(c, ((0, SUBLANES - BATCH), (0, 0)))
    per_chunk = D_MODEL // MOD_TN
    return pl.pallas_call(
        _mod_kernel,
        grid=(DEPTH, N_MOD * per_chunk),
        in_specs=[
            pl.BlockSpec((SUBLANES, D_MODEL), lambda l, j: (0, 0)),
            pl.BlockSpec((None, D_MODEL, MOD_TN), lambda l, j: (l, 0, j)),
            pl.BlockSpec((None, 1, MOD_TN), lambda l, j: (l, 0, j)),
        ],
        out_specs=pl.BlockSpec((None, SUBLANES, MOD_TN),
                               lambda l, j: (l * N_MOD + j // per_chunk, 0, j % per_chunk)),
        out_shape=jax.ShapeDtypeStruct((DEPTH * N_MOD, SUBLANES, D_MODEL), F32),
        compiler_params=_params("arbitrary", "arbitrary"),
        name="adaln_mod",
    )(c_pad, ada_w, ada_b.reshape(DEPTH, 1, N_MOD * D_MODEL))


ROPE_TM = 1024


ROPE_A = DIFF_QK_DIM // 2
ROPE_B = HEAD_DIM // 2


def _rope_kernel(pos_ref, inv_ref, ca_ref, sa_ref, cb_ref, sb_ref):
    pos = pos_ref[...].astype(F32)
    lane = lax.broadcasted_iota(jnp.int32, (1, LANES), 1)
    ang = pos * inv_ref[...]
    sign_a = jnp.where((lane & ROPE_A) == 0, -1.0, 1.0).astype(F32)
    sign_b = jnp.where((lane & ROPE_B) == 0, -1.0, 1.0).astype(F32)

    def spread_a(t):
        return jnp.where(lane < ROPE_A, t,
                         jnp.where(lane < 2 * ROPE_A, pltpu.roll(t, ROPE_A, 1),
                                   jnp.where(lane < 3 * ROPE_A, pltpu.roll(t, 2 * ROPE_A, 1),
                                             pltpu.roll(t, 3 * ROPE_A, 1))))

    def spread_b(t):
        return jnp.where(lane < ROPE_B, pltpu.roll(t, LANES - ROPE_A, 1), pltpu.roll(t, ROPE_A, 1))

    c = jnp.cos(ang)
    s = jnp.sin(ang)
    ca_ref[...] = spread_a(c)
    sa_ref[...] = spread_a(s) * sign_a
    cb_ref[...] = spread_b(c)
    sb_ref[...] = spread_b(s) * sign_b


def _rope_tables(positions):
    def inv_freq(dim):
        return ROPE_THETA ** (-jnp.arange(0, dim, 2, dtype=F32) / dim)
    inv_b = inv_freq(HEAD_DIM)
    inv = jnp.concatenate([inv_freq(DIFF_QK_DIM), inv_b, inv_b[:LANES - ROPE_A - ROPE_B]])
    tab = jax.ShapeDtypeStruct((TOKENS, LANES), F32)
    row = pl.BlockSpec((ROPE_TM, LANES), lambda m: (m, 0))
    return pl.pallas_call(
        _rope_kernel,
        grid=(TOKENS // ROPE_TM,),
        in_specs=[pl.BlockSpec((ROPE_TM, 1), lambda m: (m, 0)),
                  pl.BlockSpec((1, LANES), lambda m: (0, 0))],
        out_specs=[row, row, row, row],
        out_shape=[tab, tab, tab, tab],
        compiler_params=_params("arbitrary"),
        name="rope_tables",
    )(positions.reshape(TOKENS, 1), inv.reshape(1, LANES))


NORM_TM = 1024


def _norm_kernel(x_ref, g_ref, *rest, modulated):
    x = x_ref[...]
    y = x * lax.rsqrt(jnp.mean(x * x, axis=-1, keepdims=True) + EPS)
    y = y * g_ref[...]
    if modulated:
        sc_ref, sh_ref, o_ref = rest
        b = (pl.program_id(0) * NORM_TM) // SEQ
        y = y * (1.0 + sc_ref[pl.ds(b, 1), :]) + sh_ref[pl.ds(b, 1), :]
    else:
        (o_ref,) = rest
    o_ref[...] = y.astype(o_ref.dtype)


def _norm(x, gain, mod=None, sc_idx=None, sh_idx=None, out_dtype=BF16):
    in_specs = [pl.BlockSpec((NORM_TM, D_MODEL), lambda m: (m, 0)),
                pl.BlockSpec((1, D_MODEL), lambda m: (0, 0))]
    args = [x, gain.reshape(1, D_MODEL)]
    if mod is not None:
        in_specs += [pl.BlockSpec((None, SUBLANES, D_MODEL), lambda m: (sc_idx, 0, 0)),
                     pl.BlockSpec((None, SUBLANES, D_MODEL), lambda m: (sh_idx, 0, 0))]
        args += [mod, mod]
    return pl.pallas_call(
        functools.partial(_norm_kernel, modulated=mod is not None),
        grid=(TOKENS // NORM_TM,),
        in_specs=in_specs,
        out_specs=pl.BlockSpec((NORM_TM, D_MODEL), lambda m: (m, 0)),
        out_shape=jax.ShapeDtypeStruct((TOKENS, D_MODEL), out_dtype),
        compiler_params=_params("arbitrary"),
        name="rmsnorm_mod" if mod is not None else "rmsnorm_final",
    )(*args)


MM_SUB = 512


def _swap_halves(x, half):
    if 2 * half == LANES:
        return pltpu.roll(x, half, 1)
    lane = lax.broadcasted_iota(jnp.int32, (1, LANES), 1)
    return jnp.where((lane & half) == 0,
                     pltpu.roll(x, LANES - half, 1), pltpu.roll(x, half, 1))


def _head_epilogue(acc, rows, row0, n, refs, outs, bias, *, half, is_rope, q_scale):
    cos_ref, sin_ref = refs
    (o_ref,) = outs
    for j in range(o_ref.shape[0]):
        x = acc[:, j * LANES:(j + 1) * LANES]
        rotate = is_rope(n, j)
        if rotate is not False:
            cos = cos_ref[rows, :]
            sin_signed = sin_ref[rows, :]
            if rotate is not True:
                cos = jnp.where(rotate, cos, 1.0)
                sin_signed = jnp.where(rotate, sin_signed, 0.0)
            x = (x * cos + _swap_halves(x, half) * sin_signed) * q_scale(n, j)
        o_ref[j, rows, :] = x.astype(o_ref.dtype)


def _gated_residual_epilogue(acc, rows, row0, n, refs, outs, bias):
    x_ref, g_ref = refs
    outs[0][rows, :] = x_ref[rows, :] + g_ref[pl.ds(row0 // SEQ, 1), :] * acc


def _gated_residual_prenorm_epilogue(acc, rows, row0, n, refs, outs, bias):
    x_ref, g_ref, gain_ref, sc_ref = refs
    x_out, xs_out, ss_out = outs
    b = row0 // SEQ
    xn = x_ref[rows, :] + g_ref[pl.ds(b, 1), :] * acc
    x_out[rows, :] = xn
    xs_out[rows, :] = (xn * (gain_ref[...] * (1.0 + sc_ref[pl.ds(b, 1), :]))).astype(xs_out.dtype)
    ss_out[rows, :] = jnp.broadcast_to(jnp.sum(xn * xn, axis=-1, keepdims=True),
                                       (rows.stop - rows.start, LANES))


def _relu2_epilogue(acc, rows, row0, n, refs, outs, bias):
    r = jnp.maximum(acc, 0.0)
    outs[0][rows, :] = (r * r).astype(outs[0].dtype)


def _postnorm_relu2_epilogue(acc, rows, row0, n, refs, outs, bias):
    ss_ref = refs[0]
    ss = ss_ref[0, rows, :]
    for i in range(1, ss_ref.shape[0]):
        ss = ss + ss_ref[i, rows, :]
    inv = lax.rsqrt(ss * (1.0 / D_MODEL) + EPS)
    inv = jnp.concatenate([inv] * (acc.shape[1] // LANES), axis=1)
    r = jnp.maximum(acc * inv + bias[pl.ds(row0 // SEQ, 1), :], 0.0)
    outs[0][rows, :] = (r * r).astype(outs[0].dtype)


def _matmul_kernel(*refs, n_a, n_extra, n_out, nk, tm, m_inner, epilogue, shift_idx):
    a_refs = refs[:n_a]
    w_ref = refs[n_a]
    extra = refs[n_a + 1:n_a + 1 + n_extra]
    outs = refs[n_a + 1 + n_extra:n_a + 1 + n_extra + n_out]
    scratch = refs[n_a + 1 + n_extra + n_out:]
    wb_ref = scratch[0]
    bias_ref = scratch[-1] if shift_idx is not None else None
    outer, n, k, m = (pl.program_id(i) for i in range(4))

    acc_ref = scratch[1] if nk > 1 else None

    @pl.when(m == 0)
    def _():
        wb_ref[...] = w_ref[...].astype(BF16)
        if shift_idx is not None:
            bias_ref[...] = jnp.dot(extra[shift_idx][...].astype(BF16), wb_ref[...],
                                    preferred_element_type=F32)

    def product(t):
        rows = slice(t * MM_SUB, (t + 1) * MM_SUB)
        part = None
        off = 0
        for a_ref in a_refs:
            ka = a_ref.shape[1]
            d = jnp.dot(a_ref[rows, :], wb_ref[off:off + ka, :], preferred_element_type=F32)
            part = d if part is None else part + d
            off += ka
        return part

    def finish(k_pass, t, part):
        rows = slice(t * MM_SUB, (t + 1) * MM_SUB)
        local_row0 = m * tm + t * MM_SUB
        arows = pl.ds(pl.multiple_of(local_row0, MM_SUB), MM_SUB)
        if k_pass == "first":
            acc_ref[arows, :] = part
        elif k_pass == "middle":
            acc_ref[arows, :] += part
        else:
            if k_pass == "last":
                part = acc_ref[arows, :] + part
            epilogue(part, rows, outer * (m_inner * tm) + local_row0, n, extra, outs, bias_ref)

    def run(k_pass):
        _software_pipeline(tm // MM_SUB, product, functools.partial(finish, k_pass), lookahead=1)

    if nk == 1:
        run("only")
    else:
        pl.when(k == 0)(functools.partial(run, "first"))
        if nk > 2:
            pl.when(jnp.logical_and(k > 0, k < nk - 1))(functools.partial(run, "middle"))
        pl.when(k == nk - 1)(functools.partial(run, "last"))


def _block(shape, index_map, m_inner):
    return pl.BlockSpec(
        shape, lambda o, n, k, m: index_map(n, k, o * m_inner + m, o * m_inner))


def _matmul(a_list, a_maps, w, layer, col0, n_cols, tm, tn, m_outer, epilogue, extra_args,
            extra_blocks, out_blocks, out_shapes, name, shift_idx=None, nk=1):
    k_total = w.shape[1]
    assert col0 % tn == 0 and n_cols % tn == 0 and tm % MM_SUB == 0
    assert TOKENS % (tm * m_outer) == 0 and k_total % nk == 0
    n_off = col0 // tn
    tk = k_total // nk
    m_inner = TOKENS // tm // m_outer
    grid = (m_outer, n_cols // tn, nk, m_inner)
    a_specs = [_block(shape, imap, m_inner) for shape, imap in a_maps]
    w_spec = pl.BlockSpec((None, tk, tn), lambda o, n, k, m: (layer, k, n + n_off))
    scratch = [pltpu.VMEM((tk, tn), BF16)]
    if nk > 1:
        scratch.append(pltpu.VMEM((m_inner * tm, tn), F32))
    if shift_idx is not None:
        assert nk == 1
        scratch.append(pltpu.VMEM((SUBLANES, tn), F32))
    return pl.pallas_call(
        functools.partial(_matmul_kernel, n_a=len(a_list), n_extra=len(extra_args),
                          n_out=len(out_blocks), nk=nk, tm=tm, m_inner=m_inner, epilogue=epilogue,
                          shift_idx=shift_idx),
        grid=grid,
        in_specs=a_specs + [w_spec] + [_block(s, f, m_inner) for s, f in extra_blocks],
        out_specs=[_block(s, f, m_inner) for s, f in out_blocks],
        out_shape=out_shapes,
        scratch_shapes=scratch,
        compiler_params=_params("arbitrary", "arbitrary", "arbitrary", "arbitrary"),
        name=name,
    )(*a_list, w, *extra_args)


def _head_proj(h, w_in, layer, col0, n_cols, tm, tn, tables, half, is_rope, q_scale, name):
    table_blocks = [((tm, LANES), lambda n, k, r, r0: (r, 0)) for _ in tables]
    epilogue = functools.partial(_head_epilogue, half=half, is_rope=is_rope, q_scale=q_scale)
    return _matmul(
        [h], [((tm, D_MODEL), lambda n, k, r, r0: (r, 0))], w_in, layer, col0, n_cols, tm, tn, 1,
        epilogue, list(tables), table_blocks,
        [((tn // LANES, tm, LANES), lambda n, k, r, r0: (n, r, 0))],
        [jax.ShapeDtypeStruct((n_cols // LANES, TOKENS, LANES), BF16)], name)[0]


DIFF_IN_TILE = (2048, DIFF_W)
SWA_IN_WIDTH = SWA_Q_W + 2 * SWA_KV_W
SWA_IN_TILE = (1024, SWA_IN_WIDTH)
KA_BLK = DIFF_W // LANES
VA_BLK = 2 * DIFF_W // LANES
QB_BLK = 0
KB_BLK = SWA_Q_W // LANES
VB_BLK = KB_BLK + SWA_KV_W // LANES


def _in_proj(h, w_in, layer, tables):
    ca, sa, cb, sb = tables
    diff = _head_proj(
        h, w_in, layer, 0, 3 * DIFF_W, *DIFF_IN_TILE, (ca, sa), DIFF_QK_DIM // 2,
        lambda n, j: n < 2,
        lambda n, j: jnp.where(n == 0, DIFF_QK_DIM ** -0.5 * LOG2E, 1.0), "in_proj_diff")
    swa = _head_proj(
        h, w_in, layer, 3 * DIFF_W, SWA_IN_WIDTH, *SWA_IN_TILE, (cb, sb), HEAD_DIM // 2,
        lambda n, j: j < VB_BLK,
        lambda n, j: HEAD_DIM ** -0.5 * LOG2E if j < KB_BLK else 1.0, "in_proj_swa")
    return diff, swa


def _gated_residual_matmul(a_list, a_maps, w, layer, x, mod, gate_idx, tm, tn, m_outer, name,
                           next_norm=None, nk=1):
    last = nk - 1

    def xo_map(n, k, r, r0):
        return (jnp.where(k == last, r, r0), n)

    def mod_row(idx):
        return ((None, SUBLANES, tn), lambda n, k, r, r0: (idx, 0, n))

    args = [x, mod]
    blocks = [((tm, tn), xo_map), mod_row(gate_idx)]
    out_blocks = [((tm, tn), xo_map)]
    out_shapes = [jax.ShapeDtypeStruct((TOKENS, D_MODEL), F32)]
    epilogue = _gated_residual_epilogue
    if next_norm is not None:
        assert last == 0
        gain, scale_idx = next_norm
        args += [gain, mod]
        blocks += [((1, tn), lambda n, k, r, r0: (0, n)), mod_row(scale_idx)]
        out_blocks += [((tm, tn), xo_map), ((None, tm, LANES), lambda n, k, r, r0: (n, r, 0))]
        out_shapes += [jax.ShapeDtypeStruct((TOKENS, D_MODEL), BF16),
                       jax.ShapeDtypeStruct((D_MODEL // tn, TOKENS, LANES), F32)]
        epilogue = _gated_residual_prenorm_epilogue
    return _matmul(a_list, a_maps, w, layer, 0, D_MODEL, tm, tn, m_outer, epilogue, args, blocks,
                   out_blocks, out_shapes, name, nk=nk)


UP_TILE = (2048, 1024)
OUT_TILE = (1024, 1024)
DOWN_TILE = (2048, 512)
DOWN_M_OUTER = 2
DOWN_NK = 4
DOWN_TK = D_FF // DOWN_NK


def _out_proj(out_a, out_b, w_out, layer, x, mod, gate_idx, mlp_gain, mlp_scale_idx):
    tm, tn = OUT_TILE
    a_maps = [((tm, a.shape[1]), lambda n, k, r, r0: (r, 0)) for a in (out_a, out_b)]
    return _gated_residual_matmul([out_a, out_b], a_maps, w_out, layer, x, mod, gate_idx, tm, tn, 1,
                                  "out_proj", next_norm=(mlp_gain, mlp_scale_idx))


def _up_proj(xs, ss, w_up, layer, mod, shift_idx):
    tm, tn = UP_TILE
    per_block = DOWN_TK // tn
    blocks = [((ss.shape[0], tm, LANES), lambda n, k, r, r0: (0, r, 0)),
              ((None, SUBLANES, D_MODEL), lambda n, k, r, r0: (shift_idx, 0, 0))]
    return _matmul(
        [xs], [((tm, D_MODEL), lambda n, k, r, r0: (r, 0))], w_up, layer, 0, D_FF, tm, tn, 1,
        _postnorm_relu2_epilogue, [ss, mod], blocks,
        [((None, tm, tn), lambda n, k, r, r0: (n // per_block, r, n % per_block))],
        [jax.ShapeDtypeStruct((DOWN_NK, TOKENS, DOWN_TK), BF16)], "up_proj", shift_idx=1)[0]


def _down_proj(hidden, w_down, layer, x, mod, gate_idx):
    tm, tn = DOWN_TILE
    a_maps = [((None, tm, DOWN_TK), lambda n, k, r, r0: (k, r, 0))]
    return _gated_residual_matmul([hidden], a_maps, w_down, layer, x, mod, gate_idx, tm, tn,
                                  DOWN_M_OUTER, "down_proj", nk=DOWN_NK)[0]


DIFF_CHUNK = 1024
DIFF_HEADS_PER_STEP = 1
ATTN_LOOKAHEAD = 3
_NT = (((1,), (1,)), ((), ()))
_TN = (((0,), (0,)), ((), ()))


def _diff_attn_kernel(q_ref, k_ref, v_ref, lam_ref, g_ref, o_ref, *, lam_init):
    lane = lax.broadcasted_iota(jnp.int32, (1, LANES), 1)
    lp = lam_ref[...]
    lam = (jnp.exp(jnp.sum(lp[0:1] * lp[1:2], axis=-1, keepdims=True))
           - jnp.exp(jnp.sum(lp[2:3] * lp[3:4], axis=-1, keepdims=True)) + lam_init)
    n_chunks = SEQ // DIFF_CHUNK
    per_head = 2 * n_chunks
    results = {}

    def logits(t):
        head, rem = divmod(t, per_head)
        chunk, second = divmod(rem, 2)
        q = q_ref[head, chunk * DIFF_CHUNK:(chunk + 1) * DIFF_CHUNK, :]
        keep = (lane >= DIFF_QK_DIM) if second else (lane < DIFF_QK_DIM)
        qm = jnp.where(keep, q, jnp.zeros_like(q))
        return lax.dot_general(k_ref[head], qm, _NT, preferred_element_type=F32)

    def softmax_pv(t, st):
        e = jnp.exp2(st - jnp.max(st, axis=0, keepdims=True))
        denom = jnp.sum(e, axis=0, keepdims=True)
        ot = lax.dot_general(v_ref[t // per_head], e.astype(BF16), _TN,
                             preferred_element_type=F32)
        results[t] = (ot, denom)

    _software_pipeline(DIFF_HEADS_PER_STEP * per_head, logits, softmax_pv, ATTN_LOOKAHEAD)
    for head in range(DIFF_HEADS_PER_STEP):
        for chunk in range(n_chunks):
            t = head * per_head + 2 * chunk
            (ot0, l0), (ot1, l1) = results[t], results[t + 1]
            o = (ot0 * (1.0 / l0) - ot1 * (lam / l1)).T
            o = o * lax.rsqrt(jnp.mean(o * o, axis=-1, keepdims=True) + EPS)
            o_ref[chunk * DIFF_CHUNK:(chunk + 1) * DIFF_CHUNK, head * HEAD_DIM:(head + 1) * HEAD_DIM] = (
                o * g_ref[...] * (1.0 - lam_init)).astype(o_ref.dtype)


def _diff_attention(qkv, diff_lambda, subln, layer):
    lam_init = 0.8 - 0.6 * math.exp(-0.3 * layer)
    hps = DIFF_HEADS_PER_STEP
    return pl.pallas_call(
        functools.partial(_diff_attn_kernel, lam_init=lam_init),
        grid=(BATCH, N_HEADS_DIFF // hps),
        in_specs=[
            pl.BlockSpec((hps, SEQ, LANES), lambda b, h: (h, b, 0)),
            pl.BlockSpec((hps, SEQ, LANES), lambda b, h: (KA_BLK // hps + h, b, 0)),
            pl.BlockSpec((hps, SEQ, LANES), lambda b, h: (VA_BLK // hps + h, b, 0)),
            pl.BlockSpec((None, 4, DIFF_QK_DIM), lambda b, h: (layer, 0, 0)),
            pl.BlockSpec((None, 1, HEAD_DIM), lambda b, h: (layer, 0, 0)),
        ],
        out_specs=pl.BlockSpec((SEQ, hps * HEAD_DIM), lambda b, h: (b, h)),
        out_shape=jax.ShapeDtypeStruct((TOKENS, DIFF_W), BF16),
        compiler_params=_params("arbitrary", "arbitrary"),
        name="diff_attn",
    )(qkv, qkv, qkv, diff_lambda, subln.reshape(DEPTH, 1, HEAD_DIM))


SWA_BAND = 3 * BLOCK
SWA_QB = 16


def _swa_attn_kernel(q_ref, k_ref, v_ref, sink_ref, o_ref):
    step = pl.program_id(2)
    sink = sink_ref[...] * LOG2E

    def band_start(j):
        n = step * SWA_QB + j
        return n, pl.multiple_of(jnp.clip(n - 1, 0, SEQ // BLOCK - 3) * BLOCK, BLOCK)

    def logits(j):
        n, start = band_start(j)
        q = q_ref[:, j * BLOCK:(j + 1) * BLOCK, :].reshape(GQA_RATIO * BLOCK, HEAD_DIM)
        kb = k_ref[pl.ds(start, SWA_BAND), :]
        return lax.dot_general(kb, q, _NT, preferred_element_type=F32)

    def softmax_pv(j, st):
        n, start = band_start(j)
        kpos = start + lax.broadcasted_iota(jnp.int32, (SWA_BAND, BLOCK), 0)
        qpos = n * BLOCK + lax.broadcasted_iota(jnp.int32, (SWA_BAND, BLOCK), 1)
        bias = jnp.where(jnp.abs(kpos - qpos) <= WINDOW, 0.0, -jnp.inf).astype(F32)
        st = st + jnp.concatenate([bias] * GQA_RATIO, axis=1)
        mx = jnp.maximum(jnp.max(st, axis=0, keepdims=True), sink)
        e = jnp.exp2(st - mx)
        denom = jnp.sum(e, axis=0, keepdims=True) + jnp.exp2(sink - mx)
        vb = v_ref[pl.ds(start, SWA_BAND), :]
        ot = lax.dot_general(vb, e.astype(BF16), _TN, preferred_element_type=F32)
        ot = ot * (1.0 / denom)
        for r in range(GQA_RATIO):
            o_ref[j * BLOCK:(j + 1) * BLOCK, r * HEAD_DIM:(r + 1) * HEAD_DIM] = (
                ot[:, r * BLOCK:(r + 1) * BLOCK].T.astype(o_ref.dtype))

    _software_pipeline(SWA_QB, logits, softmax_pv, ATTN_LOOKAHEAD)


def _swa_attention(qkv, sink):
    steps = SEQ // BLOCK // SWA_QB
    rows = SWA_QB * BLOCK
    sink_rows = jnp.repeat(sink, BLOCK).reshape(N_KV_SWA, 1, GQA_RATIO * BLOCK)
    return pl.pallas_call(
        _swa_attn_kernel,
        grid=(BATCH, N_KV_SWA, steps),
        in_specs=[
            pl.BlockSpec((GQA_RATIO, rows, LANES),
                         lambda b, g, n: (QB_BLK // GQA_RATIO + g, b * steps + n, 0)),
            pl.BlockSpec((None, SEQ, LANES), lambda b, g, n: (KB_BLK + g, b, 0)),
            pl.BlockSpec((None, SEQ, LANES), lambda b, g, n: (VB_BLK + g, b, 0)),
            pl.BlockSpec((None, 1, GQA_RATIO * BLOCK), lambda b, g, n: (g, 0, 0)),
        ],
        out_specs=pl.BlockSpec((rows, GQA_RATIO * HEAD_DIM), lambda b, g, n: (b * steps + n, g)),
        out_shape=jax.ShapeDtypeStruct((TOKENS, SWA_Q_W), BF16),
        compiler_params=_params("arbitrary", "arbitrary", "arbitrary"),
        name="swa_attn",
    )(qkv, qkv, qkv, sink_rows)


def kernel(x, c, positions, ada_w, ada_b, norm_mix, w_in, diff_lambda, diff_subln, swa_sink,
           w_out, norm_mlp, w_up, w_down, final_norm):
    mod = _modulation(c, ada_w, ada_b)
    tables = _rope_tables(positions)
    xr = x.reshape(TOKENS, D_MODEL)
    for layer in range(DEPTH):
        base = layer * N_MOD
        h = _norm(xr, norm_mix[layer], mod, base + 1, base + 0)
        qkv_diff, qkv_swa = _in_proj(h, w_in, layer, tables)
        out_a = _diff_attention(qkv_diff, diff_lambda, diff_subln, layer)
        out_b = _swa_attention(qkv_swa, swa_sink[layer])
        xr, xs, ss = _out_proj(out_a, out_b, w_out, layer, xr, mod, base + 2,
                               norm_mlp[layer].reshape(1, D_MODEL), base + 4)
        hidden = _up_proj(xs, ss, w_up, layer, mod, base + 3)
        xr = _down_proj(hidden, w_down, layer, xr, mod, base + 5)
    out = _norm(xr, final_norm, out_dtype=F32)
    return out.reshape(BATCH, SEQ, D_MODEL)
```

```python
import functools
import math

import jax
import jax.numpy as jnp
from jax import lax
from jax.experimental import pallas as pl
from jax.experimental.pallas import tpu as pltpu

D_MODEL = 2048
BATCH = 4
SEQ = 2048
DEPTH = 2
HEAD_DIM = 128
N_HEADS_DIFF = 8
DIFF_QK_DIM = 64
N_HEADS_SWA = 8
N_KV_SWA = 2
GQA_RATIO = N_HEADS_SWA // N_KV_SWA
WINDOW = 128
BLOCK = 128
D_FF = 4 * D_MODEL
ROPE_THETA = 10000.0
EPS = 1e-6
N_MOD = 6
DIFF_W = N_HEADS_DIFF * HEAD_DIM
SWA_Q_W = N_HEADS_SWA * HEAD_DIM
SWA_KV_W = N_KV_SWA * HEAD_DIM
IN_WIDTH = 3 * DIFF_W + SWA_Q_W + 2 * SWA_KV_W
TOKENS = BATCH * SEQ

LANES = 128
SUBLANES = 8
VMEM_LIMIT = 56 * 1024 * 1024

F32 = jnp.float32
BF16 = jnp.bfloat16
LOG2E = math.log2(math.e)


def _params(*sem):
    return pltpu.CompilerParams(dimension_semantics=sem, vmem_limit_bytes=VMEM_LIMIT)


def _software_pipeline(n_items, produce, consume, lookahead):
    pending = {}
    for t in range(n_items + lookahead):
        if t < n_items:
            pending[t] = produce(t)
        if t >= lookahead:
            consume(t - lookahead, pending.pop(t - lookahead))


MOD_TN = 2048


def _mod_kernel(c_ref, w_ref, b_ref, o_ref):
    c = c_ref[...]
    c_act = (c * jax.nn.sigmoid(c)).astype(BF16)
    o_ref[...] = jnp.dot(c_act, w_ref[...].astype(BF16), preferred_element_type=F32) + b_ref[...]


def _modulation(c, ada_w, ada_b):
    c_pad = jnp.pad(c, ((0, SUBLANES - BATCH), (0, 0)))
    per_chunk = D_MODEL // MOD_TN
    return pl.pallas_call(
        _mod_kernel,
        grid=(DEPTH, N_MOD * per_chunk),
        in_specs=[
            pl.BlockSpec((SUBLANES, D_MODEL), lambda l, j: (0, 0)),
            pl.BlockSpec((None, D_MODEL, MOD_TN), lambda l, j: (l, 0, j)),
            pl.BlockSpec((None, 1, MOD_TN), lambda l, j: (l, 0, j)),
        ],
        out_specs=pl.BlockSpec((None, SUBLANES, MOD_TN),
                               lambda l, j: (l * N_MOD + j // per_chunk, 0, j % per_chunk)),
        out_shape=jax.ShapeDtypeStruct((DEPTH * N_MOD, SUBLANES, D_MODEL), F32),
        compiler_params=_params("arbitrary", "arbitrary"),
        name="adaln_mod",
    )(c_pad, ada_w, ada_b.reshape(DEPTH, 1, N_MOD * D_MODEL))


ROPE_TM = 1024


ROPE_A = DIFF_QK_DIM // 2
ROPE_B = HEAD_DIM // 2


def _rope_kernel(pos_ref, inv_ref, ca_ref, sa_ref, cb_ref, sb_ref):
    pos = pos_ref[...].astype(F32)
    lane = lax.broadcasted_iota(jnp.int32, (1, LANES), 1)
    ang = pos * inv_ref[...]
    sign_a = jnp.where((lane & ROPE_A) == 0, -1.0, 1.0).astype(F32)
    sign_b = jnp.where((lane & ROPE_B) == 0, -1.0, 1.0).astype(F32)

    def spread_a(t):
        return jnp.where(lane < ROPE_A, t,
                         jnp.where(lane < 2 * ROPE_A, pltpu.roll(t, ROPE_A, 1),
                                   jnp.where(lane < 3 * ROPE_A, pltpu.roll(t, 2 * ROPE_A, 1),
                                             pltpu.roll(t, 3 * ROPE_A, 1))))

    def spread_b(t):
        return jnp.where(lane < ROPE_B, pltpu.roll(t, LANES - ROPE_A, 1), pltpu.roll(t, ROPE_A, 1))

    c = jnp.cos(ang)
    s = jnp.sin(ang)
    ca_ref[...] = spread_a(c)
    sa_ref[...] = spread_a(s) * sign_a
    cb_ref[...] = spread_b(c)
    sb_ref[...] = spread_b(s) * sign_b


def _rope_tables(positions):
    def inv_freq(dim):
        return ROPE_THETA ** (-jnp.arange(0, dim, 2, dtype=F32) / dim)
    inv_b = inv_freq(HEAD_DIM)
    inv = jnp.concatenate([inv_freq(DIFF_QK_DIM), inv_b, inv_b[:LANES - ROPE_A - ROPE_B]])
    tab = jax.ShapeDtypeStruct((TOKENS, LANES), F32)
    row = pl.BlockSpec((ROPE_TM, LANES), lambda m: (m, 0))
    return pl.pallas_call(
        _rope_kernel,
        grid=(TOKENS // ROPE_TM,),
        in_specs=[pl.BlockSpec((ROPE_TM, 1), lambda m: (m, 0)),
                  pl.BlockSpec((1, LANES), lambda m: (0, 0))],
        out_specs=[row, row, row, row],
        out_shape=[tab, tab, tab, tab],
        compiler_params=_params("arbitrary"),
        name="rope_tables",
    )(positions.reshape(TOKENS, 1), inv.reshape(1, LANES))


NORM_TM = 1024


def _norm_kernel(x_ref, g_ref, *rest, modulated):
    x = x_ref[...]
    y = x * lax.rsqrt(jnp.mean(x * x, axis=-1, keepdims=True) + EPS)
    y = y * g_ref[...]
    if modulated:
        sc_ref, sh_ref, o_ref = rest
        b = (pl.program_id(0) * NORM_TM) // SEQ
        y = y * (1.0 + sc_ref[pl.ds(b, 1), :]) + sh_ref[pl.ds(b, 1), :]
    else:
        (o_ref,) = rest
    o_ref[...] = y.astype(o_ref.dtype)


def _norm(x, gain, mod=None, sc_idx=None, sh_idx=None, out_dtype=BF16):
    in_specs = [pl.BlockSpec((NORM_TM, D_MODEL), lambda m: (m, 0)),
                pl.BlockSpec((1, D_MODEL), lambda m: (0, 0))]
    args = [x, gain.reshape(1, D_MODEL)]
    if mod is not None:
        in_specs += [pl.BlockSpec((None, SUBLANES, D_MODEL), lambda m: (sc_idx, 0, 0)),
                     pl.BlockSpec((None, SUBLANES, D_MODEL), lambda m: (sh_idx, 0, 0))]
        args += [mod, mod]
    return pl.pallas_call(
        functools.partial(_norm_kernel, modulated=mod is not None),
        grid=(TOKENS // NORM_TM,),
        in_specs=in_specs,
        out_specs=pl.BlockSpec((NORM_TM, D_MODEL), lambda m: (m, 0)),
        out_shape=jax.ShapeDtypeStruct((TOKENS, D_MODEL), out_dtype),
        compiler_params=_params("arbitrary"),
        name="rmsnorm_mod" if mod is not None else "rmsnorm_final",
    )(*args)


MLP_SUB = 1024


def _swap_halves(x, half):
    if 2 * half == LANES:
        return pltpu.roll(x, half, 1)
    lane = lax.broadcasted_iota(jnp.int32, (1, LANES), 1)
    return jnp.where((lane & half) == 0,
                     pltpu.roll(x, LANES - half, 1), pltpu.roll(x, half, 1))


def _head_epilogue(acc, rows, row0, n, refs, outs, bias, *, half, is_rope, q_scale):
    cos_ref, sin_ref = refs
    (o_ref,) = outs
    for j in range(o_ref.shape[0]):
        x = acc[:, j * LANES:(j + 1) * LANES]
        rotate = is_rope(n, j)
        if rotate is not False:
            cos = cos_ref[rows, :]
            sin_signed = sin_ref[rows, :]
            if rotate is not True:
                cos = jnp.where(rotate, cos, 1.0)
                sin_signed = jnp.where(rotate, sin_signed, 0.0)
            x = (x * cos + _swap_halves(x, half) * sin_signed) * q_scale(n, j)
        o_ref[j, rows, :] = x.astype(o_ref.dtype)


def _gated_residual_epilogue(acc, rows, row0, n, refs, outs, bias):
    x_ref, g_ref = refs
    outs[0][rows, :] = x_ref[rows, :] + g_ref[pl.ds(row0 // SEQ, 1), :] * acc


def _gated_residual_prenorm_epilogue(acc, rows, row0, n, refs, outs, bias):
    x_ref, g_ref, gain_ref, sc_ref = refs
    x_out, xs_out, ss_out = outs
    b = row0 // SEQ
    xn = x_ref[rows, :] + g_ref[pl.ds(b, 1), :] * acc
    x_out[rows, :] = xn
    xs_out[rows, :] = (xn * (gain_ref[...] * (1.0 + sc_ref[pl.ds(b, 1), :]))).astype(xs_out.dtype)
    ss_out[rows, :] = jnp.broadcast_to(jnp.sum(xn * xn, axis=-1, keepdims=True),
                                       (rows.stop - rows.start, LANES))


def _relu2_epilogue(acc, rows, row0, n, refs, outs, bias):
    r = jnp.maximum(acc, 0.0)
    outs[0][rows, :] = (r * r).astype(outs[0].dtype)


def _postnorm_relu2_epilogue(acc, rows, row0, n, refs, outs, bias):
    ss_ref = refs[0]
    ss = ss_ref[0, rows, :]
    for i in range(1, ss_ref.shape[0]):
        ss = ss + ss_ref[i, rows, :]
    inv = lax.rsqrt(ss * (1.0 / D_MODEL) + EPS)
    inv = jnp.concatenate([inv] * (acc.shape[1] // LANES), axis=1)
    r = jnp.maximum(acc * inv + bias[pl.ds(row0 // SEQ, 1), :], 0.0)
    outs[0][rows, :] = (r * r).astype(outs[0].dtype)


def _matmul_kernel(*refs, n_a, n_extra, n_out, nk, tm, m_inner, epilogue, shift_idx, sub_rows):
    a_refs = refs[:n_a]
    w_ref = refs[n_a]
    extra = refs[n_a + 1:n_a + 1 + n_extra]
    outs = refs[n_a + 1 + n_extra:n_a + 1 + n_extra + n_out]
    scratch = refs[n_a + 1 + n_extra + n_out:]
    wb_ref = scratch[0]
    bias_ref = scratch[-1] if shift_idx is not None else None
    outer, n, k, m = (pl.program_id(i) for i in range(4))

    acc_ref = scratch[1] if nk > 1 else None
    sub = min(sub_rows, tm // 2)

    @pl.when(m == 0)
    def _():
        wb_ref[...] = w_ref[...].astype(BF16)
        if shift_idx is not None:
            bias_ref[...] = jnp.dot(extra[shift_idx][...].astype(BF16), wb_ref[...],
                                    preferred_element_type=F32)

    def product(t):
        rows = slice(t * sub, (t + 1) * sub)
        part = None
        off = 0
        for a_ref in a_refs:
            ka = a_ref.shape[1]
            d = jnp.dot(a_ref[rows, :], wb_ref[off:off + ka, :], preferred_element_type=F32)
            part = d if part is None else part + d
            off += ka
        return part

    def finish(k_pass, t, part):
        rows = slice(t * sub, (t + 1) * sub)
        local_row0 = m * tm + t * sub
        arows = pl.ds(pl.multiple_of(local_row0, sub), sub)
        if k_pass == "first":
            acc_ref[arows, :] = part
        elif k_pass == "middle":
            acc_ref[arows, :] += part
        else:
            if k_pass == "last":
                part = acc_ref[arows, :] + part
            epilogue(part, rows, outer * (m_inner * tm) + local_row0, n, extra, outs, bias_ref)

    def run(k_pass):
        _software_pipeline(tm // sub, product, functools.partial(finish, k_pass), lookahead=1)

    if nk == 1:
        run("only")
    else:
        pl.when(k == 0)(functools.partial(run, "first"))
        if nk > 2:
            pl.when(jnp.logical_and(k > 0, k < nk - 1))(functools.partial(run, "middle"))
        pl.when(k == nk - 1)(functools.partial(run, "last"))


def _block(shape, index_map, m_inner):
    return pl.BlockSpec(
        shape, lambda o, n, k, m: index_map(n, k, o * m_inner + m, o * m_inner))


def _matmul(a_list, a_maps, w, layer, col0, n_cols, tm, tn, m_outer, epilogue, extra_args,
            extra_blocks, out_blocks, out_shapes, name, shift_idx=None, nk=1, sub_rows=512):
    k_total = w.shape[1]
    assert col0 % tn == 0 and n_cols % tn == 0 and tm % min(sub_rows, tm // 2) == 0
    assert TOKENS % (tm * m_outer) == 0 and k_total % nk == 0
    n_off = col0 // tn
    tk = k_total // nk
    m_inner = TOKENS // tm // m_outer
    grid = (m_outer, n_cols // tn, nk, m_inner)
    a_specs = [_block(shape, imap, m_inner) for shape, imap in a_maps]
    w_spec = pl.BlockSpec((None, tk, tn), lambda o, n, k, m: (layer, k, n + n_off))
    scratch = [pltpu.VMEM((tk, tn), BF16)]
    if nk > 1:
        scratch.append(pltpu.VMEM((m_inner * tm, tn), F32))
    if shift_idx is not None:
        assert nk == 1
        scratch.append(pltpu.VMEM((SUBLANES, tn), F32))
    return pl.pallas_call(
        functools.partial(_matmul_kernel, n_a=len(a_list), n_extra=len(extra_args),
                          n_out=len(out_blocks), nk=nk, tm=tm, m_inner=m_inner, epilogue=epilogue,
                          shift_idx=shift_idx, sub_rows=sub_rows),
        grid=grid,
        in_specs=a_specs + [w_spec] + [_block(s, f, m_inner) for s, f in extra_blocks],
        out_specs=[_block(s, f, m_inner) for s, f in out_blocks],
        out_shape=out_shapes,
        scratch_shapes=scratch,
        compiler_params=_params("arbitrary", "arbitrary", "arbitrary", "arbitrary"),
        name=name,
    )(*a_list, w, *extra_args)


def _head_proj(h, w_in, layer, col0, n_cols, tm, tn, tables, half, is_rope, q_scale, name):
    table_blocks = [((tm, LANES), lambda n, k, r, r0: (r, 0)) for _ in tables]
    epilogue = functools.partial(_head_epilogue, half=half, is_rope=is_rope, q_scale=q_scale)
    return _matmul(
        [h], [((tm, D_MODEL), lambda n, k, r, r0: (r, 0))], w_in, layer, col0, n_cols, tm, tn, 1,
        epilogue, list(tables), table_blocks,
        [((tn // LANES, tm, LANES), lambda n, k, r, r0: (n, r, 0))],
        [jax.ShapeDtypeStruct((n_cols // LANES, TOKENS, LANES), BF16)], name)[0]


DIFF_IN_TILE = (2048, DIFF_W)
SWA_IN_WIDTH = SWA_Q_W + 2 * SWA_KV_W
SWA_IN_TILE = (1024, SWA_IN_WIDTH)
KA_BLK = DIFF_W // LANES
VA_BLK = 2 * DIFF_W // LANES
QB_BLK = 0
KB_BLK = SWA_Q_W // LANES
VB_BLK = KB_BLK + SWA_KV_W // LANES


def _in_proj(h, w_in, layer, tables):
    ca, sa, cb, sb = tables
    diff = _head_proj(
        h, w_in, layer, 0, 3 * DIFF_W, *DIFF_IN_TILE, (ca, sa), DIFF_QK_DIM // 2,
        lambda n, j: n < 2,
        lambda n, j: jnp.where(n == 0, DIFF_QK_DIM ** -0.5 * LOG2E, 1.0), "in_proj_diff")
    swa = _head_proj(
        h, w_in, layer, 3 * DIFF_W, SWA_IN_WIDTH, *SWA_IN_TILE, (cb, sb), HEAD_DIM // 2,
        lambda n, j: j < VB_BLK,
        lambda n, j: HEAD_DIM ** -0.5 * LOG2E if j < KB_BLK else 1.0, "in_proj_swa")
    return diff, swa


def _gated_residual_matmul(a_list, a_maps, w, layer, x, mod, gate_idx, tm, tn, m_outer, name,
                           next_norm=None, nk=1, sub_rows=512):
    last = nk - 1

    def xo_map(n, k, r, r0):
        return (jnp.where(k == last, r, r0), n)

    def mod_row(idx):
        return ((None, SUBLANES, tn), lambda n, k, r, r0: (idx, 0, n))

    args = [x, mod]
    blocks = [((tm, tn), xo_map), mod_row(gate_idx)]
    out_blocks = [((tm, tn), xo_map)]
    out_shapes = [jax.ShapeDtypeStruct((TOKENS, D_MODEL), F32)]
    epilogue = _gated_residual_epilogue
    if next_norm is not None:
        assert last == 0
        gain, scale_idx = next_norm
        args += [gain, mod]
        blocks += [((1, tn), lambda n, k, r, r0: (0, n)), mod_row(scale_idx)]
        out_blocks += [((tm, tn), xo_map), ((None, tm, LANES), lambda n, k, r, r0: (n, r, 0))]
        out_shapes += [jax.ShapeDtypeStruct((TOKENS, D_MODEL), BF16),
                       jax.ShapeDtypeStruct((D_MODEL // tn, TOKENS, LANES), F32)]
        epilogue = _gated_residual_prenorm_epilogue
    return _matmul(a_list, a_maps, w, layer, 0, D_MODEL, tm, tn, m_outer, epilogue, args, blocks,
                   out_blocks, out_shapes, name, nk=nk, sub_rows=sub_rows)


UP_TILE = (2048, 1024)
OUT_TILE = (1024, 1024)
DOWN_TILE = (2048, 512)
DOWN_M_OUTER = 2
DOWN_NK = 4
DOWN_TK = D_FF // DOWN_NK


def _out_proj(out_a, out_b, w_out, layer, x, mod, gate_idx, mlp_gain, mlp_scale_idx):
    tm, tn = OUT_TILE
    a_maps = [((tm, a.shape[1]), lambda n, k, r, r0: (r, 0)) for a in (out_a, out_b)]
    return _gated_residual_matmul([out_a, out_b], a_maps, w_out, layer, x, mod, gate_idx, tm, tn, 1,
                                  "out_proj", next_norm=(mlp_gain, mlp_scale_idx))


def _up_proj(xs, ss, w_up, layer, mod, shift_idx):
    tm, tn = UP_TILE
    per_block = DOWN_TK // tn
    blocks = [((ss.shape[0], tm, LANES), lambda n, k, r, r0: (0, r, 0)),
              ((None, SUBLANES, D_MODEL), lambda n, k, r, r0: (shift_idx, 0, 0))]
    return _matmul(
        [xs], [((tm, D_MODEL), lambda n, k, r, r0: (r, 0))], w_up, layer, 0, D_FF, tm, tn, 1,
        _postnorm_relu2_epilogue, [ss, mod], blocks,
        [((None, tm, tn), lambda n, k, r, r0: (n // per_block, r, n % per_block))],
        [jax.ShapeDtypeStruct((DOWN_NK, TOKENS, DOWN_TK), BF16)], "up_proj", shift_idx=1,
        sub_rows=MLP_SUB)[0]


def _down_proj(hidden, w_down, layer, x, mod, gate_idx):
    tm, tn = DOWN_TILE
    a_maps = [((None, tm, DOWN_TK), lambda n, k, r, r0: (k, r, 0))]
    return _gated_residual_matmul([hidden], a_maps, w_down, layer, x, mod, gate_idx, tm, tn,
                                  DOWN_M_OUTER, "down_proj", nk=DOWN_NK, sub_rows=MLP_SUB)[0]


DIFF_CHUNK = 2048
DIFF_HEADS_PER_STEP = 1
ATTN_LOOKAHEAD = 1
_NT = (((1,), (1,)), ((), ()))
_TN = (((0,), (0,)), ((), ()))


def _diff_attn_kernel(q_ref, k_ref, v_ref, lam_ref, g_ref, o_ref, *, lam_init):
    lane = lax.broadcasted_iota(jnp.int32, (1, LANES), 1)
    lp = lam_ref[...]
    lam = (jnp.exp(jnp.sum(lp[0:1] * lp[1:2], axis=-1, keepdims=True))
           - jnp.exp(jnp.sum(lp[2:3] * lp[3:4], axis=-1, keepdims=True)) + lam_init)
    n_chunks = SEQ // DIFF_CHUNK
    per_head = 2 * n_chunks
    results = {}

    def logits(t):
        head, rem = divmod(t, per_head)
        chunk, second = divmod(rem, 2)
        q = q_ref[head, chunk * DIFF_CHUNK:(chunk + 1) * DIFF_CHUNK, :]
        keep = (lane >= DIFF_QK_DIM) if second else (lane < DIFF_QK_DIM)
        qm = jnp.where(keep, q, jnp.zeros_like(q))
        return lax.dot_general(k_ref[head], qm, _NT, preferred_element_type=F32)

    def softmax_pv(t, st):
        e = jnp.exp2(st - jnp.max(st, axis=0, keepdims=True))
        denom = jnp.sum(e, axis=0, keepdims=True)
        ot = lax.dot_general(v_ref[t // per_head], e.astype(BF16), _TN,
                             preferred_element_type=F32)
        results[t] = (ot, denom)

    _software_pipeline(DIFF_HEADS_PER_STEP * per_head, logits, softmax_pv, ATTN_LOOKAHEAD)
    for head in range(DIFF_HEADS_PER_STEP):
        for chunk in range(n_chunks):
            t = head * per_head + 2 * chunk
            (ot0, l0), (ot1, l1) = results[t], results[t + 1]
            o = (ot0 * (1.0 / l0) - ot1 * (lam / l1)).T
            o = o * lax.rsqrt(jnp.mean(o * o, axis=-1, keepdims=True) + EPS)
            o_ref[chunk * DIFF_CHUNK:(chunk + 1) * DIFF_CHUNK, head * HEAD_DIM:(head + 1) * HEAD_DIM] = (
                o * g_ref[...] * (1.0 - lam_init)).astype(o_ref.dtype)


def _diff_attention(qkv, diff_lambda, subln, layer):
    lam_init = 0.8 - 0.6 * math.exp(-0.3 * layer)
    hps = DIFF_HEADS_PER_STEP
    return pl.pallas_call(
        functools.partial(_diff_attn_kernel, lam_init=lam_init),
        grid=(BATCH, N_HEADS_DIFF // hps),
        in_specs=[
            pl.BlockSpec((hps, SEQ, LANES), lambda b, h: (h, b, 0)),
            pl.BlockSpec((hps, SEQ, LANES), lambda b, h: (KA_BLK // hps + h, b, 0)),
            pl.BlockSpec((hps, SEQ, LANES), lambda b, h: (VA_BLK // hps + h, b, 0)),
            pl.BlockSpec((None, 4, DIFF_QK_DIM), lambda b, h: (layer, 0, 0)),
            pl.BlockSpec((None, 1, HEAD_DIM), lambda b, h: (layer, 0, 0)),
        ],
        out_specs=pl.BlockSpec((SEQ, hps * HEAD_DIM), lambda b, h: (b, h)),
        out_shape=jax.ShapeDtypeStruct((TOKENS, DIFF_W), BF16),
        compiler_params=_params("arbitrary", "arbitrary"),
        name="diff_attn",
    )(qkv, qkv, qkv, diff_lambda, subln.reshape(DEPTH, 1, HEAD_DIM))


SWA_BAND = 3 * BLOCK
SWA_QB = 16


def _swa_attn_kernel(q_ref, k_ref, v_ref, sink_ref, o_ref):
    step = pl.program_id(2)
    sink = sink_ref[...] * LOG2E

    def band_start(j):
        n = step * SWA_QB + j
        return n, pl.multiple_of(jnp.clip(n - 1, 0, SEQ // BLOCK - 3) * BLOCK, BLOCK)

    def logits(j):
        n, start = band_start(j)
        q = q_ref[:, j * BLOCK:(j + 1) * BLOCK, :].reshape(GQA_RATIO * BLOCK, HEAD_DIM)
        kb = k_ref[pl.ds(start, SWA_BAND), :]
        return lax.dot_general(kb, q, _NT, preferred_element_type=F32)

    def softmax_pv(j, st):
        n, start = band_start(j)
        kpos = start + lax.broadcasted_iota(jnp.int32, (SWA_BAND, BLOCK), 0)
        qpos = n * BLOCK + lax.broadcasted_iota(jnp.int32, (SWA_BAND, BLOCK), 1)
        bias = jnp.where(jnp.abs(kpos - qpos) <= WINDOW, 0.0, -jnp.inf).astype(F32)
        st = st + jnp.concatenate([bias] * GQA_RATIO, axis=1)
        mx = jnp.maximum(jnp.max(st, axis=0, keepdims=True), sink)
        e = jnp.exp2(st - mx)
        denom = jnp.sum(e, axis=0, keepdims=True) + jnp.exp2(sink - mx)
        vb = v_ref[pl.ds(start, SWA_BAND), :]
        ot = lax.dot_general(vb, e.astype(BF16), _TN, preferred_element_type=F32)
        ot = ot * (1.0 / denom)
        for r in range(GQA_RATIO):
            o_ref[j * BLOCK:(j + 1) * BLOCK, r * HEAD_DIM:(r + 1) * HEAD_DIM] = (
                ot[:, r * BLOCK:(r + 1) * BLOCK].T.astype(o_ref.dtype))

    _software_pipeline(SWA_QB, logits, softmax_pv, ATTN_LOOKAHEAD)


def _swa_attention(qkv, sink):
    steps = SEQ // BLOCK // SWA_QB
    rows = SWA_QB * BLOCK
    sink_rows = jnp.repeat(sink, BLOCK).reshape(N_KV_SWA, 1, GQA_RATIO * BLOCK)
    return pl.pallas_call(
        _swa_attn_kernel,
        grid=(BATCH, N_KV_SWA, steps),
        in_specs=[
            pl.BlockSpec((GQA_RATIO, rows, LANES),
                         lambda b, g, n: (QB_BLK // GQA_RATIO + g, b * steps + n, 0)),
            pl.BlockSpec((None, SEQ, LANES), lambda b, g, n: (KB_BLK + g, b, 0)),
            pl.BlockSpec((None, SEQ, LANES), lambda b, g, n: (VB_BLK + g, b, 0)),
            pl.BlockSpec((None, 1, GQA_RATIO * BLOCK), lambda b, g, n: (g, 0, 0)),
        ],
        out_specs=pl.BlockSpec((rows, GQA_RATIO * HEAD_DIM), lambda b, g, n: (b * steps + n, g)),
        out_shape=jax.ShapeDtypeStruct((TOKENS, SWA_Q_W), BF16),
        compiler_params=_params("arbitrary", "arbitrary", "arbitrary"),
        name="swa_attn",
    )(qkv, qkv, qkv, sink_rows)


def kernel(x, c, positions, ada_w, ada_b, norm_mix, w_in, diff_lambda, diff_subln, swa_sink,
           w_out, norm_mlp, w_up, w_down, final_norm):
    mod = _modulation(c, ada_w, ada_b)
    tables = _rope_tables(positions)
    xr = x.reshape(TOKENS, D_MODEL)
    for layer in range(DEPTH):
        base = layer * N_MOD
        h = _norm(xr, norm_mix[layer], mod, base + 1, base + 0)
        qkv_diff, qkv_swa = _in_proj(h, w_in, layer, tables)
        out_a = _diff_attention(qkv_diff, diff_lambda, diff_subln, layer)
        out_b = _swa_attention(qkv_swa, swa_sink[layer])
        xr, xs, ss = _out_proj(out_a, out_b, w_out, layer, xr, mod, base + 2,
                               norm_mlp[layer].reshape(1, D_MODEL), base + 4)
        hidden = _up_proj(xs, ss, w_up, layer, mod, base + 3)
        xr = _down_proj(hidden, w_down, layer, xr, mod, base + 5)
    out = _norm(xr, final_norm, out_dtype=F32)
    return out.reshape(BATCH, SEQ, D_MODEL)
```

```python
import functools
import math

import jax
import jax.numpy as jnp
from jax import lax
from jax.experimental import pallas as pl
from jax.experimental.pallas import tpu as pltpu

D_MODEL = 2048
BATCH = 4
SEQ = 2048
DEPTH = 2
HEAD_DIM = 128
N_HEADS_DIFF = 8
DIFF_QK_DIM = 64
N_HEADS_SWA = 8
N_KV_SWA = 2
GQA_RATIO = N_HEADS_SWA // N_KV_SWA
WINDOW = 128
BLOCK = 128
D_FF = 4 * D_MODEL
ROPE_THETA = 10000.0
EPS = 1e-6
N_MOD = 6
DIFF_W = N_HEADS_DIFF * HEAD_DIM
SWA_Q_W = N_HEADS_SWA * HEAD_DIM
SWA_KV_W = N_KV_SWA * HEAD_DIM
IN_WIDTH = 3 * DIFF_W + SWA_Q_W + 2 * SWA_KV_W
TOKENS = BATCH * SEQ

LANES = 128
SUBLANES = 8
VMEM_LIMIT = 56 * 1024 * 1024

F32 = jnp.float32
BF16 = jnp.bfloat16
LOG2E = math.log2(math.e)


def _params(*sem):
    return pltpu.CompilerParams(dimension_semantics=sem, vmem_limit_bytes=VMEM_LIMIT)


def _software_pipeline(n_items, produce, consume, lookahead):
    pending = {}
    for t in range(n_items + lookahead):
        if t < n_items:
            pending[t] = produce(t)
        if t >= lookahead:
            consume(t - lookahead, pending.pop(t - lookahead))


MOD_TN = 1024


def _mod_kernel(c_ref, w_ref, b_ref, o_ref):
    c = c_ref[...]
    c_act = (c * jax.nn.sigmoid(c)).astype(BF16)
    o_ref[...] = jnp.dot(c_act, w_ref[...].astype(BF16), preferred_element_type=F32) + b_ref[...]


def _modulation(c, ada_w, ada_b):
    c_pad = jnp.pad(c, ((0, SUBLANES - BATCH), (0, 0)))
    per_chunk = D_MODEL // MOD_TN
    return pl.pallas_call(
        _mod_kernel,
        grid=(DEPTH, N_MOD * per_chunk),
        in_specs=[
            pl.BlockSpec((SUBLANES, D_MODEL), lambda l, j: (0, 0)),
            pl.BlockSpec((None, D_MODEL, MOD_TN), lambda l, j: (l, 0, j)),
            pl.BlockSpec((None, 1, MOD_TN), lambda l, j: (l, 0, j)),
        ],
        out_specs=pl.BlockSpec((None, SUBLANES, MOD_TN),
                               lambda l, j: (l * N_MOD + j // per_chunk, 0, j % per_chunk)),
        out_shape=jax.ShapeDtypeStruct((DEPTH * N_MOD, SUBLANES, D_MODEL), F32),
        compiler_params=_params("arbitrary", "arbitrary"),
        name="adaln_mod",
    )(c_pad, ada_w, ada_b.reshape(DEPTH, 1, N_MOD * D_MODEL))


ROPE_TM = 1024


ROPE_A = DIFF_QK_DIM // 2
ROPE_B = HEAD_DIM // 2


def _rope_kernel(pos_ref, inv_ref, ca_ref, sa_ref, cb_ref, sb_ref):
    pos = pos_ref[...].astype(F32)
    lane = lax.broadcasted_iota(jnp.int32, (1, LANES), 1)
    ang = pos * inv_ref[...]
    sign_a = jnp.where((lane & ROPE_A) == 0, -1.0, 1.0).astype(F32)
    sign_b = jnp.where((lane & ROPE_B) == 0, -1.0, 1.0).astype(F32)

    def spread_a(t):
        return jnp.where(lane < ROPE_A, t,
                         jnp.where(lane < 2 * ROPE_A, pltpu.roll(t, ROPE_A, 1),
                                   jnp.where(lane < 3 * ROPE_A, pltpu.roll(t, 2 * ROPE_A, 1),
                                             pltpu.roll(t, 3 * ROPE_A, 1))))

    def spread_b(t):
        return jnp.where(lane < ROPE_B, pltpu.roll(t, LANES - ROPE_A, 1), pltpu.roll(t, ROPE_A, 1))

    c = jnp.cos(ang)
    s = jnp.sin(ang)
    ca_ref[...] = spread_a(c)
    sa_ref[...] = spread_a(s) * sign_a
    cb_ref[...] = spread_b(c)
    sb_ref[...] = spread_b(s) * sign_b


def _rope_tables(positions):
    def inv_freq(dim):
        return ROPE_THETA ** (-jnp.arange(0, dim, 2, dtype=F32) / dim)
    inv_b = inv_freq(HEAD_DIM)
    inv = jnp.concatenate([inv_freq(DIFF_QK_DIM), inv_b, inv_b[:LANES - ROPE_A - ROPE_B]])
    tab = jax.ShapeDtypeStruct((TOKENS, LANES), F32)
    row = pl.BlockSpec((ROPE_TM, LANES), lambda m: (m, 0))
    return pl.pallas_call(
        _rope_kernel,
        grid=(TOKENS // ROPE_TM,),
        in_specs=[pl.BlockSpec((ROPE_TM, 1), lambda m: (m, 0)),
                  pl.BlockSpec((1, LANES), lambda m: (0, 0))],
        out_specs=[row, row, row, row],
        out_shape=[tab, tab, tab, tab],
        compiler_params=_params("arbitrary"),
        name="rope_tables",
    )(positions.reshape(TOKENS, 1), inv.reshape(1, LANES))


NORM_TM = 1024


def _norm_kernel(x_ref, g_ref, *rest, modulated):
    x = x_ref[...]
    y = x * lax.rsqrt(jnp.mean(x * x, axis=-1, keepdims=True) + EPS)
    y = y * g_ref[...]
    if modulated:
        sc_ref, sh_ref, o_ref = rest
        b = (pl.program_id(0) * x_ref.shape[0]) // SEQ
        y = y * (1.0 + sc_ref[pl.ds(b, 1), :]) + sh_ref[pl.ds(b, 1), :]
    else:
        (o_ref,) = rest
    o_ref[...] = y.astype(o_ref.dtype)


def _norm(x, gain, mod=None, sc_idx=None, sh_idx=None, out_dtype=BF16):
    tm = NORM_TM
    in_specs = [pl.BlockSpec((tm, D_MODEL), lambda m: (m, 0)),
                pl.BlockSpec((1, D_MODEL), lambda m: (0, 0))]
    args = [x, gain.reshape(1, D_MODEL)]
    if mod is not None:
        in_specs += [pl.BlockSpec((None, SUBLANES, D_MODEL), lambda m: (sc_idx, 0, 0)),
                     pl.BlockSpec((None, SUBLANES, D_MODEL), lambda m: (sh_idx, 0, 0))]
        args += [mod, mod]
    return pl.pallas_call(
        functools.partial(_norm_kernel, modulated=mod is not None),
        grid=(TOKENS // tm,),
        in_specs=in_specs,
        out_specs=pl.BlockSpec((tm, D_MODEL), lambda m: (m, 0)),
        out_shape=jax.ShapeDtypeStruct((TOKENS, D_MODEL), out_dtype),
        compiler_params=_params("arbitrary"),
        name="rmsnorm_mod" if mod is not None else "rmsnorm_final",
    )(*args)


MM_SUB = 512


def _swap_halves(x, half):
    if 2 * half == LANES:
        return pltpu.roll(x, half, 1)
    lane = lax.broadcasted_iota(jnp.int32, (1, LANES), 1)
    return jnp.where((lane & half) == 0,
                     pltpu.roll(x, LANES - half, 1), pltpu.roll(x, half, 1))


def _head_epilogue(acc, rows, row0, n, refs, outs, bias, *, half, is_rope, q_scale):
    cos_ref, sin_ref = refs
    (o_ref,) = outs
    for j in range(o_ref.shape[0]):
        x = acc[:, j * LANES:(j + 1) * LANES]
        rotate = is_rope(n, j)
        if rotate is not False:
            cos = cos_ref[rows, :]
            sin_signed = sin_ref[rows, :]
            if rotate is not True:
                cos = jnp.where(rotate, cos, 1.0)
                sin_signed = jnp.where(rotate, sin_signed, 0.0)
            x = (x * cos + _swap_halves(x, half) * sin_signed) * q_scale(n, j)
        o_ref[j, rows, :] = x.astype(o_ref.dtype)


def _gated_residual_epilogue(acc, rows, row0, n, refs, outs, bias):
    x_ref, g_ref = refs
    outs[0][rows, :] = x_ref[rows, :] + g_ref[pl.ds(row0 // SEQ, 1), :] * acc


def _gated_residual_prenorm_epilogue(acc, rows, row0, n, refs, outs, bias):
    x_ref, g_ref, gain_ref, sc_ref = refs
    x_out, xs_out, ss_out = outs
    b = row0 // SEQ
    xn = x_ref[rows, :] + g_ref[pl.ds(b, 1), :] * acc
    x_out[rows, :] = xn
    xs_out[rows, :] = (xn * (gain_ref[...] * (1.0 + sc_ref[pl.ds(b, 1), :]))).astype(xs_out.dtype)
    ss_out[rows, :] = jnp.broadcast_to(jnp.sum(xn * xn, axis=-1, keepdims=True),
                                       (rows.stop - rows.start, LANES))


def _relu2_epilogue(acc, rows, row0, n, refs, outs, bias):
    r = jnp.maximum(acc, 0.0)
    outs[0][rows, :] = (r * r).astype(outs[0].dtype)


def _postnorm_relu2_epilogue(acc, rows, row0, n, refs, outs, bias):
    ss_ref = refs[0]
    ss = ss_ref[0, rows, :]
    for i in range(1, ss_ref.shape[0]):
        ss = ss + ss_ref[i, rows, :]
    inv = lax.rsqrt(ss * (1.0 / D_MODEL) + EPS)
    inv = jnp.concatenate([inv] * (acc.shape[1] // LANES), axis=1)
    r = jnp.maximum(acc * inv + bias[pl.ds(row0 // SEQ, 1), :], 0.0)
    outs[0][rows, :] = (r * r).astype(outs[0].dtype)


def _matmul_kernel(*refs, n_a, n_extra, n_out, nk, tm, m_inner, epilogue, shift_idx):
    a_refs = refs[:n_a]
    w_ref = refs[n_a]
    extra = refs[n_a + 1:n_a + 1 + n_extra]
    outs = refs[n_a + 1 + n_extra:n_a + 1 + n_extra + n_out]
    scratch = refs[n_a + 1 + n_extra + n_out:]
    wb_ref = scratch[0]
    bias_ref = scratch[-1] if shift_idx is not None else None
    outer, n, k, m = (pl.program_id(i) for i in range(4))

    acc_ref = scratch[1] if nk > 1 else None

    @pl.when(m == 0)
    def _():
        wb_ref[...] = w_ref[...].astype(BF16)
        if shift_idx is not None:
            bias_ref[...] = jnp.dot(extra[shift_idx][...].astype(BF16), wb_ref[...],
                                    preferred_element_type=F32)

    def product(t):
        rows = slice(t * MM_SUB, (t + 1) * MM_SUB)
        part = None
        off = 0
        for a_ref in a_refs:
            ka = a_ref.shape[1]
            d = jnp.dot(a_ref[rows, :], wb_ref[off:off + ka, :], preferred_element_type=F32)
            part = d if part is None else part + d
            off += ka
        return part

    def finish(k_pass, t, part):
        rows = slice(t * MM_SUB, (t + 1) * MM_SUB)
        local_row0 = m * tm + t * MM_SUB
        arows = pl.ds(pl.multiple_of(local_row0, MM_SUB), MM_SUB)
        if k_pass == "first":
            acc_ref[arows, :] = part
        elif k_pass == "middle":
            acc_ref[arows, :] += part
        else:
            if k_pass == "last":
                part = acc_ref[arows, :] + part
            epilogue(part, rows, outer * (m_inner * tm) + local_row0, n, extra, outs, bias_ref)

    def run(k_pass):
        _software_pipeline(tm // MM_SUB, product, functools.partial(finish, k_pass), lookahead=1)

    if nk == 1:
        run("only")
    else:
        pl.when(k == 0)(functools.partial(run, "first"))
        if nk > 2:
            pl.when(jnp.logical_and(k > 0, k < nk - 1))(functools.partial(run, "middle"))
        pl.when(k == nk - 1)(functools.partial(run, "last"))


def _block(shape, index_map, m_inner):
    return pl.BlockSpec(
        shape, lambda o, n, k, m: index_map(n, k, o * m_inner + m, o * m_inner))


def _matmul(a_list, a_maps, w, layer, col0, n_cols, tm, tn, m_outer, epilogue, extra_args,
            extra_blocks, out_blocks, out_shapes, name, shift_idx=None, nk=1):
    k_total = w.shape[1]
    assert col0 % tn == 0 and n_cols % tn == 0 and tm % MM_SUB == 0
    assert TOKENS % (tm * m_outer) == 0 and k_total % nk == 0
    n_off = col0 // tn
    tk = k_total // nk
    m_inner = TOKENS // tm // m_outer
    grid = (m_outer, n_cols // tn, nk, m_inner)
    a_specs = [_block(shape, imap, m_inner) for shape, imap in a_maps]
    w_spec = pl.BlockSpec((None, tk, tn), lambda o, n, k, m: (layer, k, n + n_off))
    scratch = [pltpu.VMEM((tk, tn), BF16)]
    if nk > 1:
        scratch.append(pltpu.VMEM((m_inner * tm, tn), F32))
    if shift_idx is not None:
        assert nk == 1
        scratch.append(pltpu.VMEM((SUBLANES, tn), F32))
    return pl.pallas_call(
        functools.partial(_matmul_kernel, n_a=len(a_list), n_extra=len(extra_args),
                          n_out=len(out_blocks), nk=nk, tm=tm, m_inner=m_inner, epilogue=epilogue,
                          shift_idx=shift_idx),
        grid=grid,
        in_specs=a_specs + [w_spec] + [_block(s, f, m_inner) for s, f in extra_blocks],
        out_specs=[_block(s, f, m_inner) for s, f in out_blocks],
        out_shape=out_shapes,
        scratch_shapes=scratch,
        compiler_params=_params("arbitrary", "arbitrary", "arbitrary", "arbitrary"),
        name=name,
    )(*a_list, w, *extra_args)


def _head_proj(h, w_in, layer, col0, n_cols, tm, tn, tables, half, is_rope, q_scale, name):
    table_blocks = [((tm, LANES), lambda n, k, r, r0: (r, 0)) for _ in tables]
    epilogue = functools.partial(_head_epilogue, half=half, is_rope=is_rope, q_scale=q_scale)
    return _matmul(
        [h], [((tm, D_MODEL), lambda n, k, r, r0: (r, 0))], w_in, layer, col0, n_cols, tm, tn, 1,
        epilogue, list(tables), table_blocks,
        [((tn // LANES, tm, LANES), lambda n, k, r, r0: (n, r, 0))],
        [jax.ShapeDtypeStruct((n_cols // LANES, TOKENS, LANES), BF16)], name)[0]


DIFF_IN_TILE = (2048, DIFF_W)
SWA_IN_WIDTH = SWA_Q_W + 2 * SWA_KV_W
SWA_IN_TILE = (1024, SWA_IN_WIDTH)
KA_BLK = DIFF_W // LANES
VA_BLK = 2 * DIFF_W // LANES
QB_BLK = 0
KB_BLK = SWA_Q_W // LANES
VB_BLK = KB_BLK + SWA_KV_W // LANES


def _in_proj(h, w_in, layer, tables):
    ca, sa, cb, sb = tables
    diff = _head_proj(
        h, w_in, layer, 0, 3 * DIFF_W, *DIFF_IN_TILE, (ca, sa), DIFF_QK_DIM // 2,
        lambda n, j: n < 2,
        lambda n, j: jnp.where(n == 0, DIFF_QK_DIM ** -0.5 * LOG2E, 1.0), "in_proj_diff")
    swa = _head_proj(
        h, w_in, layer, 3 * DIFF_W, SWA_IN_WIDTH, *SWA_IN_TILE, (cb, sb), HEAD_DIM // 2,
        lambda n, j: j < VB_BLK,
        lambda n, j: HEAD_DIM ** -0.5 * LOG2E if j < KB_BLK else 1.0, "in_proj_swa")
    return diff, swa


def _gated_residual_matmul(a_list, a_maps, w, layer, x, mod, gate_idx, tm, tn, m_outer, name,
                           next_norm=None, nk=1):
    last = nk - 1

    def xo_map(n, k, r, r0):
        return (jnp.where(k == last, r, r0), n)

    def mod_row(idx):
        return ((None, SUBLANES, tn), lambda n, k, r, r0: (idx, 0, n))

    args = [x, mod]
    blocks = [((tm, tn), xo_map), mod_row(gate_idx)]
    out_blocks = [((tm, tn), xo_map)]
    out_shapes = [jax.ShapeDtypeStruct((TOKENS, D_MODEL), F32)]
    epilogue = _gated_residual_epilogue
    if next_norm is not None:
        assert last == 0
        gain, scale_idx = next_norm
        args += [gain, mod]
        blocks += [((1, tn), lambda n, k, r, r0: (0, n)), mod_row(scale_idx)]
        out_blocks += [((tm, tn), xo_map), ((None, tm, LANES), lambda n, k, r, r0: (n, r, 0))]
        out_shapes += [jax.ShapeDtypeStruct((TOKENS, D_MODEL), BF16),
                       jax.ShapeDtypeStruct((D_MODEL // tn, TOKENS, LANES), F32)]
        epilogue = _gated_residual_prenorm_epilogue
    return _matmul(a_list, a_maps, w, layer, 0, D_MODEL, tm, tn, m_outer, epilogue, args, blocks,
                   out_blocks, out_shapes, name, nk=nk)


UP_TILE = (2048, 1024)
OUT_TILE = (1024, 1024)
DOWN_TILE = (2048, 512)
DOWN_M_OUTER = 2
DOWN_NK = 4
DOWN_TK = D_FF // DOWN_NK


def _out_proj(out_a, out_b, w_out, layer, x, mod, gate_idx, mlp_gain, mlp_scale_idx):
    tm, tn = OUT_TILE
    a_maps = [((tm, a.shape[1]), lambda n, k, r, r0: (r, 0)) for a in (out_a, out_b)]
    return _gated_residual_matmul([out_a, out_b], a_maps, w_out, layer, x, mod, gate_idx, tm, tn, 1,
                                  "out_proj", next_norm=(mlp_gain, mlp_scale_idx))


def _up_proj(xs, ss, w_up, layer, mod, shift_idx):
    tm, tn = UP_TILE
    per_block = DOWN_TK // tn
    blocks = [((ss.shape[0], tm, LANES), lambda n, k, r, r0: (0, r, 0)),
              ((None, SUBLANES, D_MODEL), lambda n, k, r, r0: (shift_idx, 0, 0))]
    return _matmul(
        [xs], [((tm, D_MODEL), lambda n, k, r, r0: (r, 0))], w_up, layer, 0, D_FF, tm, tn, 1,
        _postnorm_relu2_epilogue, [ss, mod], blocks,
        [((None, tm, tn), lambda n, k, r, r0: (n // per_block, r, n % per_block))],
        [jax.ShapeDtypeStruct((DOWN_NK, TOKENS, DOWN_TK), BF16)], "up_proj", shift_idx=1)[0]


def _down_proj(hidden, w_down, layer, x, mod, gate_idx):
    tm, tn = DOWN_TILE
    a_maps = [((None, tm, DOWN_TK), lambda n, k, r, r0: (k, r, 0))]
    return _gated_residual_matmul([hidden], a_maps, w_down, layer, x, mod, gate_idx, tm, tn,
                                  DOWN_M_OUTER, "down_proj", nk=DOWN_NK)[0]


DIFF_CHUNK = 1024
DIFF_HEADS_PER_STEP = 1
ATTN_LOOKAHEAD = 3
_NT = (((1,), (1,)), ((), ()))
_TN = (((0,), (0,)), ((), ()))


def _diff_attn_kernel(q_ref, k_ref, v_ref, lam_ref, g_ref, o_ref, *, lam_init):
    lane = lax.broadcasted_iota(jnp.int32, (1, LANES), 1)
    lp = lam_ref[...]
    lam = (jnp.exp(jnp.sum(lp[0:1] * lp[1:2], axis=-1, keepdims=True))
           - jnp.exp(jnp.sum(lp[2:3] * lp[3:4], axis=-1, keepdims=True)) + lam_init)
    n_chunks = SEQ // DIFF_CHUNK
    per_head = 2 * n_chunks
    results = {}

    def logits(t):
        head, rem = divmod(t, per_head)
        chunk, second = divmod(rem, 2)
        q = q_ref[head, chunk * DIFF_CHUNK:(chunk + 1) * DIFF_CHUNK, :]
        keep = (lane >= DIFF_QK_DIM) if second else (lane < DIFF_QK_DIM)
        qm = jnp.where(keep, q, jnp.zeros_like(q))
        return lax.dot_general(k_ref[head], qm, _NT, preferred_element_type=F32)

    def softmax_pv(t, st):
        e = jnp.exp2(st - jnp.max(st, axis=0, keepdims=True))
        denom = jnp.sum(e, axis=0, keepdims=True)
        ot = lax.dot_general(v_ref[t // per_head], e.astype(BF16), _TN,
                             preferred_element_type=F32)
        results[t] = (ot, denom)

    _software_pipeline(DIFF_HEADS_PER_STEP * per_head, logits, softmax_pv, ATTN_LOOKAHEAD)
    for head in range(DIFF_HEADS_PER_STEP):
        for chunk in range(n_chunks):
            t = head * per_head + 2 * chunk
            (ot0, l0), (ot1, l1) = results[t], results[t + 1]
            o = (ot0 * (1.0 / l0) - ot1 * (lam / l1)).T
            o = o * lax.rsqrt(jnp.mean(o * o, axis=-1, keepdims=True) + EPS)
            o_ref[chunk * DIFF_CHUNK:(chunk + 1) * DIFF_CHUNK, head * HEAD_DIM:(head + 1) * HEAD_DIM] = (
                o * g_ref[...] * (1.0 - lam_init)).astype(o_ref.dtype)


def _diff_attention(qkv, diff_lambda, subln, layer):
    lam_init = 0.8 - 0.6 * math.exp(-0.3 * layer)
    hps = DIFF_HEADS_PER_STEP
    return pl.pallas_call(
        functools.partial(_diff_attn_kernel, lam_init=lam_init),
        grid=(BATCH, N_HEADS_DIFF // hps),
        in_specs=[
            pl.BlockSpec((hps, SEQ, LANES), lambda b, h: (h, b, 0)),
            pl.BlockSpec((hps, SEQ, LANES), lambda b, h: (KA_BLK // hps + h, b, 0)),
            pl.BlockSpec((hps, SEQ, LANES), lambda b, h: (VA_BLK // hps + h, b, 0)),
            pl.BlockSpec((None, 4, DIFF_QK_DIM), lambda b, h: (layer, 0, 0)),
            pl.BlockSpec((None, 1, HEAD_DIM), lambda b, h: (layer, 0, 0)),
        ],
        out_specs=pl.BlockSpec((SEQ, hps * HEAD_DIM), lambda b, h: (b, h)),
        out_shape=jax.ShapeDtypeStruct((TOKENS, DIFF_W), BF16),
        compiler_params=_params("arbitrary", "arbitrary"),
        name="diff_attn",
    )(qkv, qkv, qkv, diff_lambda, subln.reshape(DEPTH, 1, HEAD_DIM))


SWA_BAND = 3 * BLOCK
SWA_QB = 16


def _swa_attn_kernel(q_ref, k_ref, v_ref, sink_ref, o_ref):
    step = pl.program_id(2)
    sink = sink_ref[...] * LOG2E

    def band_start(j):
        n = step * SWA_QB + j
        return n, pl.multiple_of(jnp.clip(n - 1, 0, SEQ // BLOCK - 3) * BLOCK, BLOCK)

    def logits(j):
        n, start = band_start(j)
        q = q_ref[:, j * BLOCK:(j + 1) * BLOCK, :].reshape(GQA_RATIO * BLOCK, HEAD_DIM)
        kb = k_ref[pl.ds(start, SWA_BAND), :]
        return lax.dot_general(kb, q, _NT, preferred_element_type=F32)

    def softmax_pv(j, st):
        n, start = band_start(j)
        kpos = start + lax.broadcasted_iota(jnp.int32, (SWA_BAND, BLOCK), 0)
        qpos = n * BLOCK + lax.broadcasted_iota(jnp.int32, (SWA_BAND, BLOCK), 1)
        bias = jnp.where(jnp.abs(kpos - qpos) <= WINDOW, 0.0, -jnp.inf).astype(F32)
        st = st + jnp.concatenate([bias] * GQA_RATIO, axis=1)
        mx = jnp.maximum(jnp.max(st, axis=0, keepdims=True), sink)
        e = jnp.exp2(st - mx)
        denom = jnp.sum(e, axis=0, keepdims=True) + jnp.exp2(sink - mx)
        vb = v_ref[pl.ds(start, SWA_BAND), :]
        ot = lax.dot_general(vb, e.astype(BF16), _TN, preferred_element_type=F32)
        ot = ot * (1.0 / denom)
        for r in range(GQA_RATIO):
            o_ref[j * BLOCK:(j + 1) * BLOCK, r * HEAD_DIM:(r + 1) * HEAD_DIM] = (
                ot[:, r * BLOCK:(r + 1) * BLOCK].T.astype(o_ref.dtype))

    _software_pipeline(SWA_QB, logits, softmax_pv, ATTN_LOOKAHEAD)


def _swa_attention(qkv, sink):
    steps = SEQ // BLOCK // SWA_QB
    rows = SWA_QB * BLOCK
    sink_rows = jnp.repeat(sink, BLOCK).reshape(N_KV_SWA, 1, GQA_RATIO * BLOCK)
    return pl.pallas_call(
        _swa_attn_kernel,
        grid=(BATCH, N_KV_SWA, steps),
        in_specs=[
            pl.BlockSpec((GQA_RATIO, rows, LANES),
                         lambda b, g, n: (QB_BLK // GQA_RATIO + g, b * steps + n, 0)),
            pl.BlockSpec((None, SEQ, LANES), lambda b, g, n: (KB_BLK + g, b, 0)),
            pl.BlockSpec((None, SEQ, LANES), lambda b, g, n: (VB_BLK + g, b, 0)),
            pl.BlockSpec((None, 1, GQA_RATIO * BLOCK), lambda b, g, n: (g, 0, 0)),
        ],
        out_specs=pl.BlockSpec((rows, GQA_RATIO * HEAD_DIM), lambda b, g, n: (b * steps + n, g)),
        out_shape=jax.ShapeDtypeStruct((TOKENS, SWA_Q_W), BF16),
        compiler_params=_params("arbitrary", "arbitrary", "arbitrary"),
        name="swa_attn",
    )(qkv, qkv, qkv, sink_rows)


def kernel(x, c, positions, ada_w, ada_b, norm_mix, w_in, diff_lambda, diff_subln, swa_sink,
           w_out, norm_mlp, w_up, w_down, final_norm):
    mod = _modulation(c, ada_w, ada_b)
    tables = _rope_tables(positions)
    xr = x.reshape(TOKENS, D_MODEL)
    for layer in range(DEPTH):
        base = layer * N_MOD
        h = _norm(xr, norm_mix[layer], mod, base + 1, base + 0)
        qkv_diff, qkv_swa = _in_proj(h, w_in, layer, tables)
        out_a = _diff_attention(qkv_diff, diff_lambda, diff_subln, layer)
        out_b = _swa_attention(qkv_swa, swa_sink[layer])
        xr, xs, ss = _out_proj(out_a, out_b, w_out, layer, xr, mod, base + 2,
                               norm_mlp[layer].reshape(1, D_MODEL), base + 4)
        hidden = _up_proj(xs, ss, w_up, layer, mod, base + 3)
        xr = _down_proj(hidden, w_down, layer, xr, mod, base + 5)
    out = _norm(xr, final_norm, out_dtype=F32)
    return out.reshape(BATCH, SEQ, D_MODEL)
```
